```python
import math
import jax, jax.numpy as jnp
from jax import lax
import numpy as np

D_MODEL = 1024
BATCH = 32
SEQ = 256
DEPTH = 4
DEC_BATCH = 4
DEC_SEQ = 2048
PAST_LEN = 512

GRID_W = 64
N_MIXERS = 3
N_CONV_LAYERS = (DEPTH + 2) // 3
N_ATTN_LAYERS = (DEPTH + 1) // 3
N_HYENA_LAYERS = DEPTH // 3
HEAD_DIM = 128
N_HEADS = D_MODEL // HEAD_DIM
N_KV_HEADS = N_HEADS // 4
ROPE_AXIS_DIM = HEAD_DIM // 2
ROPE_THETA = 10000.0
Q_BLOCK = 128
CONV_WIDTH = 31
SHORT_WIDTH = 3
HY_BANDS = 16
HY_EMB = 2 * HY_BANDS + 1
HY_FILTER_HIDDEN = 64
HY_DECAY_TARGET = 1e-2
HY_SHORT_PCT = 0.3
HY_LONG_PCT = 1.5
N_EXPERTS = 16
EXPERT_FF = 1024
EC_CAPACITY_FACTOR = 2
NORM_EPS = 1e-6

kernel_name = 'hybrid_conv_gqa_hyena_ecmoe_diffusion_step'

F32 = jnp.float32


def _rms_norm(x, g):
    xf = x.astype(F32)
    y = xf * lax.rsqrt(jnp.mean(xf * xf, axis=-1, keepdims=True) + NORM_EPS)
    return (y * g.astype(F32)).astype(x.dtype)


def _layer_norm(x, g, b):
    xf = x.astype(F32)
    xc = xf - jnp.mean(xf, axis=-1, keepdims=True)
    y = xc * lax.rsqrt(jnp.mean(xc * xc, axis=-1, keepdims=True) + NORM_EPS)
    return (y * g.astype(F32) + b.astype(F32)).astype(x.dtype)


def _depthwise_conv(x, w, b):
    pad = w.shape[0] // 2
    y = lax.conv_general_dilated(x, w[:, None, :].astype(x.dtype), window_strides=(1,),
                                 padding=[(pad, pad)], dimension_numbers=('NWC', 'WIO', 'NWC'),
                                 feature_group_count=x.shape[-1])
    return y + b


def _axial_rope(L):
    n_rows = L // GRID_W
    rows = jnp.repeat(jnp.arange(n_rows), GRID_W).astype(F32)
    cols = jnp.tile(jnp.arange(GRID_W), n_rows).astype(F32)
    inv = ROPE_THETA ** (-jnp.arange(0, ROPE_AXIS_DIM, 2, dtype=F32) / ROPE_AXIS_DIM)
    ang = jnp.concatenate([rows[:, None] * inv, cols[:, None] * inv], axis=-1)
    return jnp.cos(ang), jnp.sin(ang)


def _apply_rope(x, cos, sin):
    xf = x.astype(F32).reshape(x.shape[:-1] + (HEAD_DIM // 2, 2))
    x1, x2 = xf[..., 0], xf[..., 1]
    c = cos[None, :, None, :]
    s = sin[None, :, None, :]
    out = jnp.stack([x1 * c - x2 * s, x1 * s + x2 * c], axis=-1).reshape(x.shape)
    return out.astype(x.dtype)


def _block_attention(q, k, v):
    B, L, H, Dh = q.shape
    KV = k.shape[2]
    G = H // KV
    nb = L // Q_BLOCK
    qb = q.reshape(B, nb, Q_BLOCK, KV, G, Dh).transpose(1, 0, 2, 3, 4, 5)
    scale = Dh ** -0.5

    def one_block(qblk):
        s = jnp.einsum('bqkgd,bskd->bkgqs', qblk, k, preferred_element_type=F32) * scale
        p = jax.nn.softmax(s, axis=-1).astype(v.dtype)
        return jnp.einsum('bkgqs,bskd->bqkgd', p, v)

    o = lax.map(one_block, qb)
    return o.transpose(1, 0, 2, 3, 4, 5).reshape(B, L, H, Dh)


def _conformer_conv(h, w_in, b_in, w_dw, b_dw, ln_g, ln_b, w_out, b_out):
    a, gt = jnp.split(h @ w_in + b_in, 2, axis=-1)
    z = a * jax.nn.sigmoid(gt)
    z = _depthwise_conv(z, w_dw, b_dw)
    z = jax.nn.silu(_layer_norm(z, ln_g, ln_b))
    return z @ w_out + b_out


def _gqa(h, w_qkv, w_o, q_g, k_g, rope, ctx_k, ctx_v):
    B, L, _ = h.shape
    nq = N_HEADS * HEAD_DIM
    nk = N_KV_HEADS * HEAD_DIM
    qkv = h @ w_qkv
    q = _rms_norm(qkv[..., :nq].reshape(B, L, N_HEADS, HEAD_DIM), q_g)
    k = _rms_norm(qkv[..., nq:nq + nk].reshape(B, L, N_KV_HEADS, HEAD_DIM), k_g)
    v = qkv[..., nq + nk:].reshape(B, L, N_KV_HEADS, HEAD_DIM)
    if rope is None:
        k_all, v_all = k, v
    else:
        cos, sin = rope
        q = _apply_rope(q, cos, sin)
        k_all = jnp.concatenate([ctx_k.astype(k.dtype), _apply_rope(k, cos, sin)], axis=1)
        v_all = jnp.concatenate([ctx_v.astype(v.dtype), v], axis=1)
    o = _block_attention(q, k_all, v_all)
    return o.reshape(B, L, nq) @ w_o, k, v


def _hyena_filter(L, w1, b1, fr1, w2, b2, fr2, w3):
    D = w3.shape[-1] // 2
    t = jnp.arange(L, dtype=F32) / L
    bands = jnp.arange(1, HY_BANDS + 1, dtype=F32)
    ph = 2.0 * math.pi * t[:, None] * bands
    z = jnp.concatenate([t[:, None], jnp.sin(ph), jnp.cos(ph)], axis=-1)
    f = jnp.sin(fr1.astype(F32) * (z @ w1.astype(F32) + b1.astype(F32)))
    f = jnp.sin(fr2.astype(F32) * (f @ w2.astype(F32) + b2.astype(F32)))
    f = (f @ w3.astype(F32)).reshape(L, 2, D)
    rates = jnp.abs(jnp.linspace(math.log(HY_DECAY_TARGET) / HY_LONG_PCT,
                                 math.log(HY_DECAY_TARGET) / HY_SHORT_PCT, D, dtype=F32))
    f = f * jnp.exp(-t[:, None] * rates)[:, None, :]
    h_fwd, h_bwd = f[:, 0], f[:, 1]
    return jnp.concatenate([h_fwd, jnp.zeros((1, D), F32), h_bwd[:0:-1]], axis=0)


def _hyena(h, w_in, b_in, w_short, b_short, filt, skip, w_out, b_out):
    z = _depthwise_conv(h @ w_in + b_in, w_short, b_short)
    x0, x1, v = jnp.split(z, 3, axis=-1)
    u = (v * x1).astype(F32)
    L = u.shape[1]
    U = jnp.fft.rfft(u, n=2 * L, axis=1)
    K = jnp.fft.rfft(filt, axis=0)
    y = jnp.fft.irfft(U * K[None], n=2 * L, axis=1)[:, :L] + u * skip.astype(F32)
    y = (y * x0.astype(F32)).astype(h.dtype)
    return y @ w_out + b_out


def _ec_moe(h, w_router, w_gate, w_up, w_down):
    B, L, D = h.shape
    cap = EC_CAPACITY_FACTOR * L // N_EXPERTS
    logits = jnp.einsum('bld,de->ble', h.astype(F32), w_router.astype(F32))
    aff = jax.nn.softmax(logits, axis=-1)
    g, idx = lax.top_k(aff.transpose(0, 2, 1), cap)
    xe = jax.vmap(lambda hb, ib: hb[ib])(h, idx)
    a = jnp.einsum('becd,edf->becf', xe, w_gate)
    u = jnp.einsum('becd,edf->becf', xe, w_up)
    ye = jnp.einsum('becf,efd->becd', jax.nn.silu(a) * u, w_down) * g[..., None].astype(h.dtype)
    return jax.vmap(lambda yb, ib: jnp.zeros((L, D), yb.dtype).at[ib.reshape(-1)].add(yb.reshape(-1, D)))(ye, idx)


def _trunk(x, cvec, P, cache_k, cache_v):
    B, L, D = x.shape
    is_ctx = cache_k is None
    rope = None if is_ctx else _axial_rope(L)
    new_k, new_v = [], []
    for i in range(DEPTH):
        kind, j = i % N_MIXERS, i // N_MIXERS
        m = (jax.nn.silu(cvec) @ P['mod_w'][i] + P['mod_b'][i]).reshape(cvec.shape[0], 1, 6, D)
        h = _rms_norm(x, P['norm1_g'][i]) * (1 + m[:, :, 1]) + m[:, :, 0]
        if kind == 0:
            out = _conformer_conv(h, P['cv_w_in'][j], P['cv_b_in'][j], P['cv_w_dw'][j], P['cv_b_dw'][j],
                                  P['cv_ln_g'][j], P['cv_ln_b'][j], P['cv_w_out'][j], P['cv_b_out'][j])
        elif kind == 1:
            ck = None if is_ctx else cache_k[:, j]
            cv = None if is_ctx else cache_v[:, j]
            out, k, v = _gqa(h, P['at_w_qkv'][j], P['at_w_o'][j], P['at_q_norm'][j], P['at_k_norm'][j],
                             rope, ck, cv)
            if is_ctx:
                new_k.append(k)
                new_v.append(v)
        else:
            filt = _hyena_filter(L, P['hy_f_w1'][j], P['hy_f_b1'][j], P['hy_f_freq1'][j], P['hy_f_w2'][j],
                                 P['hy_f_b2'][j], P['hy_f_freq2'][j], P['hy_f_w3'][j])
            out = _hyena(h, P['hy_w_in'][j], P['hy_b_in'][j], P['hy_w_short'][j], P['hy_b_short'][j],
                         filt, P['hy_skip'][j], P['hy_w_out'][j], P['hy_b_out'][j])
        x = x + m[:, :, 2] * out
        h = _rms_norm(x, P['norm2_g'][i]) * (1 + m[:, :, 4]) + m[:, :, 3]
        x = x + m[:, :, 5] * _ec_moe(h, P['moe_router'][i], P['moe_w_gate'][i], P['moe_w_up'][i], P['moe_w_down'][i])
    return _rms_norm(x, P['final_g']), new_k, new_v


def setup_inputs(seed: int = 0) -> dict:
    key = jax.random.key(seed)
    keys = iter(jax.random.split(key, 64))

    def nrm(shape, scale=1.0):
        return jax.random.normal(next(keys), shape, jnp.float32) * scale

    D = D_MODEL
    F = EXPERT_FF
    E = N_EXPERTS
    FH = HY_FILTER_HIDDEN
    QKV = (N_HEADS + 2 * N_KV_HEADS) * HEAD_DIM
    return {
        'x_prompt': nrm((BATCH, SEQ, D)),
        'x_sample': nrm((DEC_BATCH, DEC_SEQ, D)),
        'cache_k': nrm((DEC_BATCH, N_ATTN_LAYERS, PAST_LEN, N_KV_HEADS, HEAD_DIM)),
        'cache_v': nrm((DEC_BATCH, N_ATTN_LAYERS, PAST_LEN, N_KV_HEADS, HEAD_DIM)),
        'c': nrm((DEC_BATCH, D)),
        'c_ctx': nrm((D,)),
        'mod_w': nrm((DEPTH, D, 6 * D), 0.5 * D ** -0.5),
        'mod_b': nrm((DEPTH, 6 * D), 0.02),
        'norm1_g': 1.0 + nrm((DEPTH, D), 0.05),
        'norm2_g': 1.0 + nrm((DEPTH, D), 0.05),
        'cv_w_in': nrm((N_CONV_LAYERS, D, 2 * D), D ** -0.5),
        'cv_b_in': nrm((N_CONV_LAYERS, 2 * D), 0.02),
        'cv_w_dw': nrm((N_CONV_LAYERS, CONV_WIDTH, D), CONV_WIDTH ** -0.5),
        'cv_b_dw': nrm((N_CONV_LAYERS, D), 0.02),
        'cv_ln_g': 1.0 + nrm((N_CONV_LAYERS, D), 0.05),
        'cv_ln_b': nrm((N_CONV_LAYERS, D), 0.02),
        'cv_w_out': nrm((N_CONV_LAYERS, D, D), D ** -0.5),
        'cv_b_out': nrm((N_CONV_LAYERS, D), 0.02),
        'at_w_qkv': nrm((N_ATTN_LAYERS, D, QKV), D ** -0.5),
        'at_w_o': nrm((N_ATTN_LAYERS, N_HEADS * HEAD_DIM, D), (N_HEADS * HEAD_DIM) ** -0.5),
        'at_q_norm': 1.0 + nrm((N_ATTN_LAYERS, HEAD_DIM), 0.05),
        'at_k_norm': 1.0 + nrm((N_ATTN_LAYERS, HEAD_DIM), 0.05),
        'hy_w_in': nrm((N_HYENA_LAYERS, D, 3 * D), D ** -0.5),
        'hy_b_in': nrm((N_HYENA_LAYERS, 3 * D), 0.02),
        'hy_w_short': nrm((N_HYENA_LAYERS, SHORT_WIDTH, 3 * D), SHORT_WIDTH ** -0.5),
        'hy_b_short': nrm((N_HYENA_LAYERS, 3 * D), 0.02),
        'hy_f_w1': nrm((N_HYENA_LAYERS, HY_EMB, FH), HY_EMB ** -0.5),
        'hy_f_b1': nrm((N_HYENA_LAYERS, FH), 0.1),
        'hy_f_freq1': 1.0 + nrm((N_HYENA_LAYERS, FH), 0.1),
        'hy_f_w2': nrm((N_HYENA_LAYERS, FH, FH), FH ** -0.5),
        'hy_f_b2': nrm((N_HYENA_LAYERS, FH), 0.1),
        'hy_f_freq2': 1.0 + nrm((N_HYENA_LAYERS, FH), 0.1),
        'hy_f_w3': nrm((N_HYENA_LAYERS, FH, 2 * D), 0.05 * FH ** -0.5),
        'hy_skip': nrm((N_HYENA_LAYERS, D), 0.5),
        'hy_w_out': nrm((N_HYENA_LAYERS, D, D), D ** -0.5),
        'hy_b_out': nrm((N_HYENA_LAYERS, D), 0.02),
        'moe_router': nrm((DEPTH, D, E), D ** -0.5),
        'moe_w_gate': nrm((DEPTH, E, D, F), D ** -0.5),
        'moe_w_up': nrm((DEPTH, E, D, F), D ** -0.5),
        'moe_w_down': nrm((DEPTH, E, F, D), F ** -0.5),
        'final_g': 1.0 + nrm((D,), 0.05),
    }


def reference(x_prompt, x_sample, cache_k, cache_v, c, c_ctx, mod_w, mod_b, norm1_g, norm2_g,
              cv_w_in, cv_b_in, cv_w_dw, cv_b_dw, cv_ln_g, cv_ln_b, cv_w_out, cv_b_out,
              at_w_qkv, at_w_o, at_q_norm, at_k_norm,
              hy_w_in, hy_b_in, hy_w_short, hy_b_short, hy_f_w1, hy_f_b1, hy_f_freq1, hy_f_w2,
              hy_f_b2, hy_f_freq2, hy_f_w3, hy_skip, hy_w_out, hy_b_out,
              moe_router, moe_w_gate, moe_w_up, moe_w_down, final_g):
    P = {
        'mod_w': mod_w, 'mod_b': mod_b, 'norm1_g': norm1_g, 'norm2_g': norm2_g,
        'cv_w_in': cv_w_in, 'cv_b_in': cv_b_in, 'cv_w_dw': cv_w_dw, 'cv_b_dw': cv_b_dw,
        'cv_ln_g': cv_ln_g, 'cv_ln_b': cv_ln_b, 'cv_w_out': cv_w_out, 'cv_b_out': cv_b_out,
        'at_w_qkv': at_w_qkv, 'at_w_o': at_w_o, 'at_q_norm': at_q_norm, 'at_k_norm': at_k_norm,
        'hy_w_in': hy_w_in, 'hy_b_in': hy_b_in, 'hy_w_short': hy_w_short, 'hy_b_short': hy_b_short,
        'hy_f_w1': hy_f_w1, 'hy_f_b1': hy_f_b1, 'hy_f_freq1': hy_f_freq1, 'hy_f_w2': hy_f_w2,
        'hy_f_b2': hy_f_b2, 'hy_f_freq2': hy_f_freq2, 'hy_f_w3': hy_f_w3, 'hy_skip': hy_skip,
        'hy_w_out': hy_w_out, 'hy_b_out': hy_b_out,
        'moe_router': moe_router, 'moe_w_gate': moe_w_gate, 'moe_w_up': moe_w_up, 'moe_w_down': moe_w_down,
        'final_g': final_g,
    }
    y_prompt, ks, vs = _trunk(x_prompt, c_ctx[None, :], P, None, None)
    new_cache_k = jnp.stack(ks, axis=1)
    new_cache_v = jnp.stack(vs, axis=1)
    y_sample, _, _ = _trunk(x_sample, c, P, cache_k, cache_v)
    return (y_prompt, y_sample, new_cache_k, new_cache_v)
```

```python
import functools
import math

import jax
import jax.numpy as jnp
from jax import lax
from jax.experimental import pallas as pl
from jax.experimental.pallas import tpu as pltpu

F32 = jnp.float32
BF16 = jnp.bfloat16
HIGHEST = lax.Precision.HIGHEST

D = 1024
DEPTH = 4
GRID_W = 64
N_MIXERS = 3
HEAD_DIM = 128
N_HEADS = 8
N_KV_HEADS = 2
KV_GROUP = N_HEADS // N_KV_HEADS
NQ = N_HEADS * HEAD_DIM
NK = N_KV_HEADS * HEAD_DIM
ROPE_AXIS_DIM = HEAD_DIM // 2
ROPE_THETA = 10000.0
CONV_WIDTH = 31
CONV_PAD = CONV_WIDTH // 2
HY_BANDS = 16
HY_DECAY_TARGET = 1e-2
HY_SHORT_PCT = 0.3
HY_LONG_PCT = 1.5
N_EXPERTS = 16
EXPERT_FF = 1024
EC_CAPACITY_FACTOR = 2
NORM_EPS = 1e-6

LANES = 128
SUBLANES = 8
VMEM_LIMIT_BYTES = 56 * 1024 * 1024

TOKEN_TILE = 256
CONV_HALO = 16
CONV_ROW_CHUNK = 32
MOD_ROWS = 8
GATHER_ROWS = 512
FFN_F_TILE = 512
CUMSUM_CHUNK = 256


def _cparams(*sem):
    return pltpu.CompilerParams(dimension_semantics=sem, vmem_limit_bytes=VMEM_LIMIT_BYTES)


def _norm_mod(x, g, sh, sc):
    y = x * lax.rsqrt(jnp.mean(x * x, axis=-1, keepdims=True) + NORM_EPS)
    return (y * g) * (1.0 + sc) + sh


def _mod_spec(rowfn, j):
    return pl.BlockSpec((1, 1, D), lambda *idx: (rowfn(*idx) * 6 + j, 0, 0))


def _row_spec(n):
    return pl.BlockSpec((1, n), lambda *idx: (0, 0))


def _full_spec(shape):
    nd = len(shape)
    return pl.BlockSpec(shape, lambda *idx: (0,) * nd)


def _mod_kernel(c_ref, w_ref, b_ref, o_ref):
    cv = c_ref[...]
    s = cv * jax.nn.sigmoid(cv)
    o_ref[0] = jnp.dot(s, w_ref[0], preferred_element_type=F32, precision=HIGHEST) + b_ref[0]


def _mod_all(cond, mod_w, mod_b):
    tn = 1536
    n = 6 * D
    return pl.pallas_call(
        _mod_kernel,
        grid=(DEPTH, n // tn),
        in_specs=[
            pl.BlockSpec((MOD_ROWS, D), lambda i, j: (0, 0)),
            pl.BlockSpec((1, D, tn), lambda i, j: (i, 0, j)),
            pl.BlockSpec((1, 1, tn), lambda i, j: (i, 0, j)),
        ],
        out_specs=pl.BlockSpec((1, MOD_ROWS, tn), lambda i, j: (i, 0, j)),
        out_shape=jax.ShapeDtypeStruct((DEPTH, MOD_ROWS, n), F32),
        compiler_params=_cparams("parallel", "parallel"),
        name="mod_all",
    )(cond, mod_w, mod_b.reshape(DEPTH, 1, n))


def _proj_glu_kernel(x_ref, g_ref, sh_ref, sc_ref, w_ref, b_ref, o_ref):
    h = _norm_mod(x_ref[...], g_ref[...], sh_ref[0], sc_ref[0])
    y = jnp.dot(h.astype(BF16), w_ref[...], preferred_element_type=F32) + b_ref[...]
    o_ref[...] = y[:, :D] * jax.nn.sigmoid(y[:, D:])


def _proj_plain_kernel(x_ref, g_ref, sh_ref, sc_ref, w_ref, b_ref, o_ref):
    h = _norm_mod(x_ref[...], g_ref[...], sh_ref[0], sc_ref[0])
    o_ref[...] = jnp.dot(h.astype(BF16), w_ref[...], preferred_element_type=F32) + b_ref[...]


def _head_norm(seg, g):
    return seg * lax.rsqrt(jnp.mean(seg * seg, axis=-1, keepdims=True) + NORM_EPS) * g


def _proj_qkv_kernel(x_ref, g_ref, sh_ref, sc_ref, w_ref, qg_ref, kg_ref, *rest, rope):
    if rope:
        cos_ref, sin_ref, q_ref, k_ref, v_ref = rest
    else:
        q_ref, k_ref, v_ref = rest
    h = _norm_mod(x_ref[...], g_ref[...], sh_ref[0], sc_ref[0])
    y = jnp.dot(h.astype(BF16), w_ref[...], preferred_element_type=F32)
    if rope:
        cos = cos_ref[...]
        sin = sin_ref[...]
        lane = lax.broadcasted_iota(jnp.int32, (y.shape[0], HEAD_DIM), 1)
        even = (lane & 1) == 0
    for hd in range(N_HEADS + N_KV_HEADS):
        seg = y[:, hd * HEAD_DIM:(hd + 1) * HEAD_DIM]
        nrm = _head_norm(seg, qg_ref[...] if hd < N_HEADS else kg_ref[...])
        if rope:
            partner = jnp.where(even, pltpu.roll(nrm, HEAD_DIM - 1, 1), pltpu.roll(nrm, 1, 1))
            nrm = nrm * cos + partner * sin
        if hd < N_HEADS:
            q_ref[:, hd * HEAD_DIM:(hd + 1) * HEAD_DIM] = nrm.astype(BF16)
        else:
            k_ref[:, (hd - N_HEADS) * HEAD_DIM:(hd - N_HEADS + 1) * HEAD_DIM] = nrm
    v_ref[...] = y[:, NQ + NK:]


def _proj(kind, x, g, mi, w, b, mode, extra=None):
    rows = x.shape[0]
    tm = TOKEN_TILE
    nb = kind["L"] // tm
    rowfn = kind["rowfn"](nb)
    n = w.shape[1]
    x_spec = pl.BlockSpec((tm, D), lambda i: (i, 0))
    base_specs = [x_spec, _row_spec(D), _mod_spec(rowfn, 0), _mod_spec(rowfn, 1), _full_spec((D, n))]
    if mode in ("glu", "plain"):
        n_out = D if mode == "glu" else n
        return pl.pallas_call(
            _proj_glu_kernel if mode == "glu" else _proj_plain_kernel,
            grid=(rows // tm,),
            in_specs=base_specs + [_row_spec(n)],
            out_specs=pl.BlockSpec((tm, n_out), lambda i: (i, 0)),
            out_shape=jax.ShapeDtypeStruct((rows, n_out), F32),
            compiler_params=_cparams("parallel"),
            name="proj_" + mode,
        )(x, g, mi, mi, w, b)
    qg, kg, cos, sin = extra
    rope = cos is not None
    specs = base_specs + [_row_spec(HEAD_DIM), _row_spec(HEAD_DIM)]
    args = [x, g, mi, mi, w, qg, kg]
    if rope:
        specs += [pl.BlockSpec((tm, HEAD_DIM), lambda i: (i % nb, 0))] * 2
        args += [cos, sin]
    return pl.pallas_call(
        functools.partial(_proj_qkv_kernel, rope=rope),
        grid=(rows // tm,),
        in_specs=specs,
        out_specs=[pl.BlockSpec((tm, NQ), lambda i: (i, 0)),
                   pl.BlockSpec((tm, NK), lambda i: (i, 0)),
                   pl.BlockSpec((tm, NK), lambda i: (i, 0))],
        out_shape=[jax.ShapeDtypeStruct((rows, NQ), BF16),
                   jax.ShapeDtypeStruct((rows, NK), F32),
                   jax.ShapeDtypeStruct((rows, NK), F32)],
        compiler_params=_cparams("parallel"),
        name="proj_qkv",
    )(*args)


def _out_proj_kernel(a_ref, w_ref, b_ref, x_ref, gate_ref, o_ref):
    y = jnp.dot(a_ref[...].astype(BF16), w_ref[...], preferred_element_type=F32) + b_ref[...]
    o_ref[...] = x_ref[...] + gate_ref[0] * y


def _out_proj(kind, a, w, b, x, mi):
    rows = x.shape[0]
    tm = TOKEN_TILE
    rowfn = kind["rowfn"](kind["L"] // tm)
    t_spec = pl.BlockSpec((tm, D), lambda i: (i, 0))
    return pl.pallas_call(
        _out_proj_kernel,
        grid=(rows // tm,),
        in_specs=[t_spec, _full_spec((D, D)), _row_spec(D), t_spec, _mod_spec(rowfn, 2)],
        out_specs=t_spec,
        out_shape=jax.ShapeDtypeStruct((rows, D), F32),
        compiler_params=_cparams("parallel"),
        name="out_proj",
    )(a, w, b, x, mi)


def _conv_kernel(zp_ref, zc_ref, zn_ref, wdw_ref, bdw_ref, lg_ref, lb_ref, wo_ref, bo_ref,
                 x_ref, gate_ref, o_ref, zb_ref, act_ref, *, nb, tm):
    j = pl.program_id(0) % nb
    zb_ref[0:CONV_HALO, :] = jnp.where(j > 0, zp_ref[...], 0.0)
    zb_ref[CONV_HALO:CONV_HALO + tm, :] = zc_ref[...]
    zb_ref[CONV_HALO + tm:2 * CONV_HALO + tm, :] = jnp.where(j < nb - 1, zn_ref[...], 0.0)
    first = CONV_HALO - CONV_PAD
    for r0 in range(0, tm, CONV_ROW_CHUNK):
        acc = jnp.broadcast_to(bdw_ref[...], (CONV_ROW_CHUNK, D))
        for k in range(CONV_WIDTH):
            lo = r0 + first + k
            acc = acc + zb_ref[lo:lo + CONV_ROW_CHUNK, :] * wdw_ref[k:k + 1, :]
        xc = acc - jnp.mean(acc, axis=-1, keepdims=True)
        y = xc * lax.rsqrt(jnp.mean(xc * xc, axis=-1, keepdims=True) + NORM_EPS)
        y = y * lg_ref[...] + lb_ref[...]
        act_ref[r0:r0 + CONV_ROW_CHUNK, :] = (y * jax.nn.sigmoid(y)).astype(BF16)
    out = jnp.dot(act_ref[...], wo_ref[...], preferred_element_type=F32) + bo_ref[...]
    o_ref[...] = x_ref[...] + gate_ref[0] * out


def _conv_tail(kind, z, w_dw, b_dw, ln_g, ln_b, w_out, b_out, x, mi):
    rows = x.shape[0]
    tm = TOKEN_TILE
    nb = kind["L"] // tm
    rowfn = kind["rowfn"](nb)
    hb = tm // CONV_HALO
    n_halo = rows // CONV_HALO
    t_spec = pl.BlockSpec((tm, D), lambda i: (i, 0))
    return pl.pallas_call(
        functools.partial(_conv_kernel, nb=nb, tm=tm),
        grid=(rows // tm,),
        in_specs=[
            pl.BlockSpec((CONV_HALO, D), lambda i: (jnp.maximum(i * hb - 1, 0), 0)),
            t_spec,
            pl.BlockSpec((CONV_HALO, D), lambda i: (jnp.minimum((i + 1) * hb, n_halo - 1), 0)),
            _full_spec((CONV_WIDTH, D)), _row_spec(D), _row_spec(D), _row_spec(D),
            _full_spec((D, D)), _row_spec(D), t_spec, _mod_spec(rowfn, 2),
        ],
        out_specs=t_spec,
        out_shape=jax.ShapeDtypeStruct((rows, D), F32),
        scratch_shapes=[pltpu.VMEM((tm + 2 * CONV_HALO, D), F32), pltpu.VMEM((tm, D), BF16)],
        compiler_params=_cparams("parallel"),
        name="conv_tail",
    )(z, z, z, w_dw, b_dw, ln_g, ln_b, w_out, b_out, x, mi)


def _attn_kernel(q_ref, k_ref, v_ref, wo_ref, x_ref, gate_ref, o_ref):
    scale = HEAD_DIM ** -0.5
    outs = []
    for hd in range(N_HEADS):
        kv = hd // KV_GROUP
        qh = q_ref[:, hd * HEAD_DIM:(hd + 1) * HEAD_DIM]
        kh = k_ref[0, :, kv * HEAD_DIM:(kv + 1) * HEAD_DIM]
        vh = v_ref[0, :, kv * HEAD_DIM:(kv + 1) * HEAD_DIM]
        s = lax.dot_general(qh, kh, (((1,), (1,)), ((), ())), preferred_element_type=F32) * scale
        p = jnp.exp(s - jnp.max(s, axis=-1, keepdims=True))
        l = jnp.sum(p, axis=-1, keepdims=True)
        o = jnp.dot(p.astype(BF16), vh, preferred_element_type=F32)
        outs.append(o / l)
    o = jnp.concatenate(outs, axis=1).astype(BF16)
    y = jnp.dot(o, wo_ref[...], preferred_element_type=F32)
    o_ref[...] = x_ref[...] + gate_ref[0] * y


def _attention(kind, q, k_all, v_all, w_o, x, mi):
    rows = x.shape[0]
    tq = TOKEN_TILE
    nb = kind["L"] // tq
    rowfn = kind["rowfn"](nb)
    s_len = k_all.shape[1]
    t_spec = pl.BlockSpec((tq, D), lambda i: (i, 0))
    kv_spec = pl.BlockSpec((1, s_len, NK), lambda i: (i // nb, 0, 0))
    return pl.pallas_call(
        _attn_kernel,
        grid=(rows // tq,),
        in_specs=[t_spec, kv_spec, kv_spec, _full_spec((D, D)), t_spec, _mod_spec(rowfn, 2)],
        out_specs=t_spec,
        out_shape=jax.ShapeDtypeStruct((rows, D), F32),
        compiler_params=_cparams("parallel"),
        name="attention",
    )(q, k_all, v_all, w_o, x, mi)


def _hy_pre_kernel(zp_ref, zc_ref, zn_ref, w_ref, b_ref, u_ref, x0_ref, zb_ref, *, nb, tm):
    j = pl.program_id(0) % nb
    zb_ref[0:SUBLANES, :] = jnp.where(j > 0, zp_ref[...], 0.0)
    zb_ref[SUBLANES:SUBLANES + tm, :] = zc_ref[...]
    zb_ref[SUBLANES + tm:2 * SUBLANES + tm, :] = jnp.where(j < nb - 1, zn_ref[...], 0.0)
    rc = CONV_ROW_CHUNK
    for r0 in range(0, tm, rc):
        parts = []
        for part in range(3):
            cs = slice(part * D, (part + 1) * D)
            lo = r0 + SUBLANES - 1
            z = (zb_ref[lo:lo + rc, cs] * w_ref[0:1, cs] + zb_ref[lo + 1:lo + 1 + rc, cs] * w_ref[1:2, cs]
                 + zb_ref[lo + 2:lo + 2 + rc, cs] * w_ref[2:3, cs] + b_ref[:, cs])
            parts.append(z)
        x0_ref[r0:r0 + rc, :] = parts[0]
        u_ref[r0:r0 + rc, :] = parts[2] * parts[1]


def _hy_pre(kind, zin, w_short, b_short):
    rows = zin.shape[0]
    tm = TOKEN_TILE
    nb = kind["L"] // tm
    hb = tm // SUBLANES
    n_halo = rows // SUBLANES
    t_spec = pl.BlockSpec((tm, D), lambda i: (i, 0))
    return pl.pallas_call(
        functools.partial(_hy_pre_kernel, nb=nb, tm=tm),
        grid=(rows // tm,),
        in_specs=[
            pl.BlockSpec((SUBLANES, 3 * D), lambda i: (jnp.maximum(i * hb - 1, 0), 0)),
            pl.BlockSpec((tm, 3 * D), lambda i: (i, 0)),
            pl.BlockSpec((SUBLANES, 3 * D), lambda i: (jnp.minimum((i + 1) * hb, n_halo - 1), 0)),
            _full_spec((3, 3 * D)), _row_spec(3 * D),
        ],
        out_specs=[t_spec, t_spec],
        out_shape=[jax.ShapeDtypeStruct((rows, D), F32), jax.ShapeDtypeStruct((rows, D), F32)],
        scratch_shapes=[pltpu.VMEM((tm + 2 * SUBLANES, 3 * D), F32)],
        compiler_params=_cparams("parallel"),
        name="hy_pre",
    )(zin, zin, zin, w_short, b_short)


def _hy_filter_kernel(z_ref, w1_ref, b1_ref, f1_ref, w2_ref, b2_ref, f2_ref, w3f_ref, w3b_ref,
                      dec_ref, ha_ref, hb_ref, nyq_ref):
    f = jnp.sin(f1_ref[...] * (jnp.dot(z_ref[...], w1_ref[...], preferred_element_type=F32,
                                       precision=HIGHEST) + b1_ref[...]))
    f = jnp.sin(f2_ref[...] * (jnp.dot(f, w2_ref[...], preferred_element_type=F32,
                                       precision=HIGHEST) + b2_ref[...]))
    dec = dec_ref[...]
    hf = jnp.dot(f, w3f_ref[...], preferred_element_type=F32, precision=HIGHEST) * dec
    hb = jnp.dot(f, w3b_ref[...], preferred_element_type=F32, precision=HIGHEST) * dec
    row = lax.broadcasted_iota(jnp.int32, hf.shape, 0)
    hb = jnp.where(row == 0, 0.0, hb)
    ha = hf + hb
    ha_ref[...] = ha
    hb_ref[...] = hf - hb
    sign = jnp.where((row & 1) == 0, 1.0, -1.0)
    nyq_ref[...] = jnp.sum(ha * sign, axis=0, keepdims=True)


def _hy_filter(L, zfeat, w1, b1, fr1, w2, b2, fr2, w3, decay):
    tc = 256
    nct = D // tc
    hp = w1.shape[1]
    c_spec = pl.BlockSpec((L, tc), lambda c: (0, c))
    return pl.pallas_call(
        _hy_filter_kernel,
        grid=(nct,),
        in_specs=[
            _full_spec(zfeat.shape), _full_spec(w1.shape), _row_spec(hp), _row_spec(hp),
            _full_spec(w2.shape), _row_spec(hp), _row_spec(hp),
            pl.BlockSpec((hp, tc), lambda c: (0, c)),
            pl.BlockSpec((hp, tc), lambda c: (0, nct + c)),
            c_spec,
        ],
        out_specs=[c_spec, c_spec, pl.BlockSpec((1, tc), lambda c: (0, c))],
        out_shape=[jax.ShapeDtypeStruct((L, D), F32), jax.ShapeDtypeStruct((L, D), F32),
                   jax.ShapeDtypeStruct((1, D), F32)],
        compiler_params=_cparams("parallel"),
        name="hy_filter",
    )(zfeat, w1, b1, fr1, w2, b2, fr2, w3, w3, decay)


def _hy_spectrum_kernel(fc_ref, fs_ref, ha_ref, hb_ref, nyq_ref, kra_ref, krb_ref, ki_ref, *, L, ft):
    kr = jnp.dot(fc_ref[...], ha_ref[...], preferred_element_type=F32, precision=HIGHEST)
    ki = jnp.dot(fs_ref[...], hb_ref[...], preferred_element_type=F32, precision=HIGHEST)
    row = lax.broadcasted_iota(jnp.int32, kr.shape, 0) + pl.program_id(0) * ft
    scale = jnp.where(row == 0, 0.5 / L, 1.0 / L)
    kr = kr * scale
    kra_ref[...] = kr
    krb_ref[...] = jnp.where(row == 0, nyq_ref[...] * (0.5 / L), kr)
    ki_ref[...] = ki * scale


def _hy_spectrum(L, fc, fs, ha, hb, nyq):
    ft = min(L, 256)
    tc = 512
    mat_spec = pl.BlockSpec((ft, L), lambda k, c: (k, 0))
    h_spec = pl.BlockSpec((L, tc), lambda k, c: (0, c))
    o_spec = pl.BlockSpec((ft, tc), lambda k, c: (k, c))
    return pl.pallas_call(
        functools.partial(_hy_spectrum_kernel, L=L, ft=ft),
        grid=(L // ft, D // tc),
        in_specs=[mat_spec, mat_spec, h_spec, h_spec, pl.BlockSpec((1, tc), lambda k, c: (0, c))],
        out_specs=[o_spec, o_spec, o_spec],
        out_shape=[jax.ShapeDtypeStruct((L, D), F32)] * 3,
        compiler_params=_cparams("parallel", "parallel"),
        name="hy_spectrum",
    )(fc, fs, ha, hb, nyq)


def _hy_longconv_kernel(u_ref, fc_ref, fs_ref, gc_ref, gs_ref, kra_ref, krb_ref, ki_ref,
                        x0_ref, skip_ref, o_ref, acc_ref):
    k = pl.program_id(2)

    @pl.when(k == 0)
    def _():
        acc_ref[...] = jnp.zeros_like(acc_ref)

    ub = u_ref[...].astype(BF16)
    ur = jnp.dot(fc_ref[...], ub, preferred_element_type=F32)
    ui = jnp.dot(fs_ref[...], ub, preferred_element_type=F32)
    kra = kra_ref[...]
    krb = krb_ref[...]
    ki = ki_ref[...]
    yr = (ur * kra - ui * ki).astype(BF16)
    yi = (ur * ki + ui * krb).astype(BF16)
    acc_ref[...] += (jnp.dot(gc_ref[...], yr, preferred_element_type=F32)
                     + jnp.dot(gs_ref[...], yi, preferred_element_type=F32))

    @pl.when(k == pl.num_programs(2) - 1)
    def _():
        o_ref[...] = (acc_ref[...] + u_ref[...] * skip_ref[...]) * x0_ref[...]


def _hy_longconv(kind, u, x0, mats, spec, skip):
    B, L = kind["B"], kind["L"]
    fc, fs, fst = mats
    kra, krb, ki = spec
    ft = min(L, 512)
    tc = 1024 if L <= 256 else 256
    d_spec = pl.BlockSpec((L, tc), lambda b, c, k: (b, c))
    fwd_spec = pl.BlockSpec((ft, L), lambda b, c, k: (k, 0))
    inv_spec = pl.BlockSpec((L, ft), lambda b, c, k: (0, k))
    k_spec = pl.BlockSpec((ft, tc), lambda b, c, k: (k, c))
    return pl.pallas_call(
        _hy_longconv_kernel,
        grid=(B, D // tc, L // ft),
        in_specs=[d_spec, fwd_spec, fwd_spec, inv_spec, inv_spec, k_spec, k_spec, k_spec, d_spec,
                  pl.BlockSpec((1, tc), lambda b, c, k: (0, c))],
        out_specs=d_spec,
        out_shape=jax.ShapeDtypeStruct((B * L, D), F32),
        scratch_shapes=[pltpu.VMEM((L, tc), F32)],
        compiler_params=_cparams("parallel", "parallel", "arbitrary"),
        name="hy_longconv",
    )(u, fc, fs, fc, fst, kra, krb, ki, x0, skip)


def _router_kernel(x_ref, g_ref, sh_ref, sc_ref, wr_ref, h_ref, aff_ref):
    h = _norm_mod(x_ref[...], g_ref[...], sh_ref[0], sc_ref[0])
    h_ref[...] = h.astype(BF16)
    logits = lax.dot_general(wr_ref[...], h, (((1,), (1,)), ((), ())),
                             preferred_element_type=F32, precision=HIGHEST)
    e = jnp.exp(logits - jnp.max(logits, axis=0, keepdims=True))
    aff_ref[0] = e / jnp.sum(e, axis=0, keepdims=True)


def _router(kind, x, g, mi, w_router_t):
    rows = x.shape[0]
    tm = TOKEN_TILE
    nb = kind["L"] // tm
    rowfn = kind["rowfn"](nb)
    return pl.pallas_call(
        _router_kernel,
        grid=(rows // tm,),
        in_specs=[pl.BlockSpec((tm, D), lambda i: (i, 0)), _row_spec(D), _mod_spec(rowfn, 3),
                  _mod_spec(rowfn, 4), _full_spec((N_EXPERTS, D))],
        out_specs=[pl.BlockSpec((tm, D), lambda i: (i, 0)),
                   pl.BlockSpec((1, N_EXPERTS, tm), lambda i: (i // nb, 0, i % nb))],
        out_shape=[jax.ShapeDtypeStruct((rows, D), BF16),
                   jax.ShapeDtypeStruct((kind["B"], N_EXPERTS, kind["L"]), F32)],
        compiler_params=_cparams("parallel"),
        name="router",
    )(x, g, mi, mi, w_router_t)


def _lane_cumsum(mask_f32, tri):
    rows, n = mask_f32.shape
    run = jnp.zeros((rows, 1), F32)
    pieces = []
    for c0 in range(0, n, CUMSUM_CHUNK):
        chunk = mask_f32[:, c0:c0 + CUMSUM_CHUNK]
        pieces.append(jnp.dot(chunk.astype(BF16), tri, preferred_element_type=F32) + run)
        run = run + jnp.sum(chunk, axis=1, keepdims=True)
    return jnp.concatenate(pieces, axis=1) if len(pieces) > 1 else pieces[0]


def _route_kernel(aff_ref, slot_ref, *, cap):
    bits = pltpu.bitcast(aff_ref[...], jnp.int32)
    rows = bits.shape[0]
    thr = jnp.zeros((rows, 1), jnp.int32)
    capf = float(cap)
    for bit in range(30, -1, -1):
        cand = thr | (1 << bit)
        cnt = jnp.sum(jnp.where(bits >= cand, 1.0, 0.0), axis=1, keepdims=True)
        thr = jnp.where(cnt >= capf, cand, thr)
    r_i = lax.broadcasted_iota(jnp.int32, (CUMSUM_CHUNK, CUMSUM_CHUNK), 0)
    c_i = lax.broadcasted_iota(jnp.int32, (CUMSUM_CHUNK, CUMSUM_CHUNK), 1)
    tri = jnp.where(r_i <= c_i, 1.0, 0.0).astype(BF16)
    gt = jnp.where(bits > thr, 1.0, 0.0)
    eq = jnp.where(bits == thr, 1.0, 0.0)
    need = capf - jnp.sum(gt, axis=1, keepdims=True)
    eq_rank = _lane_cumsum(eq, tri)
    sel = gt + eq * jnp.where(eq_rank <= need, 1.0, 0.0)
    pos = _lane_cumsum(sel, tri)
    slot_ref[...] = jnp.where(sel > 0.5, pos - 1.0, -1.0).astype(jnp.int32)


def _route(aff2d, cap):
    return pl.pallas_call(
        functools.partial(_route_kernel, cap=cap),
        out_shape=jax.ShapeDtypeStruct(aff2d.shape, jnp.int32),
        compiler_params=pltpu.CompilerParams(vmem_limit_bytes=VMEM_LIMIT_BYTES),
        name="route",
    )(aff2d)


def _dispatch_mask(slot, cap):
    eg, n = slot.shape
    iota = lax.broadcasted_iota(jnp.int32, (cap, n), 0)
    masks = [iota == slot[e:e + 1, :] for e in range(eg)]
    return masks


def _gather_kernel(slot_ref, aff_ref, h_ref, xe_ref, g_ref, *, cap, eg):
    slot = slot_ref[0]
    aff = aff_ref[0]
    masks = _dispatch_mask(slot, cap)
    onehot = jnp.concatenate([jnp.where(m, 1.0, 0.0) for m in masks], axis=0).astype(BF16)
    xe = jnp.dot(onehot, h_ref[...], preferred_element_type=F32)
    xe_ref[...] = xe.reshape(eg, cap, D).astype(BF16)
    for e in range(eg):
        gsel = jnp.sum(jnp.where(masks[e], aff[e:e + 1, :], 0.0), axis=1, keepdims=True)
        g_ref[e] = jnp.broadcast_to(gsel, (cap, LANES))


def _gather(kind, slot3, aff3, h2):
    B, L, cap, eg = kind["B"], kind["L"], kind["cap"], kind["eg"]
    ng = N_EXPERTS // eg
    r_spec = pl.BlockSpec((1, eg, L), lambda b, e: (b * ng + e, 0, 0))
    return pl.pallas_call(
        functools.partial(_gather_kernel, cap=cap, eg=eg),
        grid=(B, ng),
        in_specs=[r_spec, r_spec, pl.BlockSpec((L, D), lambda b, e: (b, 0))],
        out_specs=[pl.BlockSpec((eg, cap, D), lambda b, e: (e, b, 0)),
                   pl.BlockSpec((eg, cap, LANES), lambda b, e: (e, b, 0))],
        out_shape=[jax.ShapeDtypeStruct((N_EXPERTS, B * cap, D), BF16),
                   jax.ShapeDtypeStruct((N_EXPERTS, B * cap, LANES), F32)],
        compiler_params=_cparams("parallel", "parallel"),
        name="moe_gather",
    )(slot3, aff3, h2)


def _ffn_kernel(xa_ref, xb_ref, ga_ref, gb_ref, wg_ref, wu_ref, wd_ref, ya_ref, yb_ref, acc_ref):
    f = pl.program_id(1)

    @pl.when(f == 0)
    def _():
        acc_ref[...] = jnp.zeros_like(acc_ref)

    wg = wg_ref[0].astype(BF16)
    wu = wu_ref[0].astype(BF16)
    wd = wd_ref[0].astype(BF16)
    na = xa_ref.shape[1]
    for x_ref, r0 in ((xa_ref, 0), (xb_ref, na)):
        x = x_ref[0]
        a = jnp.dot(x, wg, preferred_element_type=F32)
        u = jnp.dot(x, wu, preferred_element_type=F32)
        mid = (a * jax.nn.sigmoid(a) * u).astype(BF16)
        acc_ref[r0:r0 + x.shape[0], :] += jnp.dot(mid, wd, preferred_element_type=F32)

    @pl.when(f == pl.num_programs(1) - 1)
    def _():
        ya_ref[0] = (acc_ref[0:na, :] * ga_ref[0][:, 0:1]).astype(BF16)
        yb_ref[0] = (acc_ref[na:, :] * gb_ref[0][:, 0:1]).astype(BF16)


def _ffn(xe_a, xe_b, g_a, g_b, w_gate, w_up, w_down):
    na, nb_ = xe_a.shape[1], xe_b.shape[1]
    tf = FFN_F_TILE
    return pl.pallas_call(
        _ffn_kernel,
        grid=(N_EXPERTS, EXPERT_FF // tf),
        in_specs=[
            pl.BlockSpec((1, na, D), lambda e, f: (e, 0, 0)),
            pl.BlockSpec((1, nb_, D), lambda e, f: (e, 0, 0)),
            pl.BlockSpec((1, na, LANES), lambda e, f: (e, 0, 0)),
            pl.BlockSpec((1, nb_, LANES), lambda e, f: (e, 0, 0)),
            pl.BlockSpec((1, D, tf), lambda e, f: (e, 0, f)),
            pl.BlockSpec((1, D, tf), lambda e, f: (e, 0, f)),
            pl.BlockSpec((1, tf, D), lambda e, f: (e, f, 0)),
        ],
        out_specs=[pl.BlockSpec((1, na, D), lambda e, f: (e, 0, 0)),
                   pl.BlockSpec((1, nb_, D), lambda e, f: (e, 0, 0))],
        out_shape=[jax.ShapeDtypeStruct((N_EXPERTS, na, D), BF16),
                   jax.ShapeDtypeStruct((N_EXPERTS, nb_, D), BF16)],
        scratch_shapes=[pltpu.VMEM((na + nb_, D), F32)],
        compiler_params=_cparams("parallel", "arbitrary"),
        name="moe_ffn",
    )(xe_a, xe_b, g_a, g_b, w_gate, w_up, w_down)


def _scatter_kernel(slot_ref, ye_ref, x_ref, gate_ref, fg_ref, o_ref, acc_ref, *, cap, eg, final):
    e = pl.program_id(2)

    @pl.when(e == 0)
    def _():
        acc_ref[...] = jnp.zeros_like(acc_ref)

    masks = _dispatch_mask(slot_ref[0], cap)
    onehot = jnp.concatenate([jnp.where(m, 1.0, 0.0) for m in masks], axis=0).astype(BF16)
    ye = ye_ref[...].reshape(eg * cap, D)
    acc_ref[...] += lax.dot_general(onehot, ye, (((0,), (0,)), ((), ())), preferred_element_type=F32)

    @pl.when(e == pl.num_programs(2) - 1)
    def _():
        x = x_ref[...] + gate_ref[0] * acc_ref[...]
        if final:
            x = x * lax.rsqrt(jnp.mean(x * x, axis=-1, keepdims=True) + NORM_EPS) * fg_ref[...]
        o_ref[...] = x


def _scatter(kind, slot3, ye, x, mi, final_g, final):
    B, L, cap, eg = kind["B"], kind["L"], kind["cap"], kind["eg"]
    ng = N_EXPERTS // eg
    lt = min(L, 1024)
    nl = L // lt
    rowfn = lambda b, l, e: kind["rowfn"](1)(b)
    x_spec = pl.BlockSpec((lt, D), lambda b, l, e: (b * nl + l, 0))
    return pl.pallas_call(
        functools.partial(_scatter_kernel, cap=cap, eg=eg, final=final),
        grid=(B, nl, ng),
        in_specs=[pl.BlockSpec((1, eg, lt), lambda b, l, e: (b * ng + e, 0, l)),
                  pl.BlockSpec((eg, cap, D), lambda b, l, e: (e, b, 0)),
                  x_spec, _mod_spec(rowfn, 5), _row_spec(D)],
        out_specs=x_spec,
        out_shape=jax.ShapeDtypeStruct((B * L, D), F32),
        scratch_shapes=[pltpu.VMEM((lt, D), F32)],
        compiler_params=_cparams("parallel", "parallel", "arbitrary"),
        name="moe_scatter",
    )(slot3, ye, x, mi, final_g)


def _rope_tables(L):
    n_rows = L // GRID_W
    rows = jnp.repeat(jnp.arange(n_rows), GRID_W).astype(F32)
    cols = jnp.tile(jnp.arange(GRID_W), n_rows).astype(F32)
    inv = ROPE_THETA ** (-jnp.arange(0, ROPE_AXIS_DIM, 2, dtype=F32) / ROPE_AXIS_DIM)
    ang = jnp.concatenate([rows[:, None] * inv, cols[:, None] * inv], axis=-1)
    cos = jnp.repeat(jnp.cos(ang), 2, axis=-1)
    sin = jnp.repeat(jnp.sin(ang), 2, axis=-1)
    sign = jnp.tile(jnp.array([-1.0, 1.0], F32), HEAD_DIM // 2)
    return cos, sin * sign


def _dft_tables(L):
    f = jnp.arange(L, dtype=jnp.int32)
    ang = ((f[:, None] * f[None, :]) % (2 * L)).astype(F32) * (math.pi / L)
    fc = jnp.cos(ang)
    fs = -jnp.sin(ang)
    sign = jnp.where(f % 2 == 0, 1.0, -1.0).astype(F32)
    return fc, fs, fs.at[0, :].set(sign), fs.at[:, 0].set(sign)


def _filter_features(L):
    t = jnp.arange(L, dtype=F32) / L
    bands = jnp.arange(1, HY_BANDS + 1, dtype=F32)
    ph = 2.0 * math.pi * t[:, None] * bands
    z = jnp.concatenate([t[:, None], jnp.sin(ph), jnp.cos(ph)], axis=-1)
    z = jnp.pad(z, ((0, 0), (0, LANES - z.shape[1])))
    rates = jnp.abs(jnp.linspace(math.log(HY_DECAY_TARGET) / HY_LONG_PCT,
                                 math.log(HY_DECAY_TARGET) / HY_SHORT_PCT, D, dtype=F32))
    return z, jnp.exp(-t[:, None] * rates)


def _pad_to(a, shape):
    return jnp.pad(a, [(0, s - d) for d, s in zip(a.shape, shape)])


def kernel(x_prompt, x_sample, cache_k, cache_v, c, c_ctx, mod_w, mod_b, norm1_g, norm2_g, cv_w_in, cv_b_in, cv_w_dw, cv_b_dw, cv_ln_g, cv_ln_b, cv_w_out, cv_b_out, at_w_qkv, at_w_o, at_q_norm, at_k_norm, hy_w_in, hy_b_in, hy_w_short, hy_b_short, hy_f_w1, hy_f_b1, hy_f_freq1, hy_f_w2, hy_f_b2, hy_f_freq2, hy_f_w3, hy_skip, hy_w_out, hy_b_out, moe_router, moe_w_gate, moe_w_up, moe_w_down, final_g):
    b_ctx, l_ctx, _ = x_prompt.shape
    b_lat, l_lat, _ = x_sample.shape
    kinds = [
        dict(B=b_ctx, L=l_ctx, lat=False, rowfn=lambda nb: (lambda i: 0)),
        dict(B=b_lat, L=l_lat, lat=True, rowfn=lambda nb: (lambda i: 1 + i // nb)),
    ]
    for kd in kinds:
        kd["cap"] = EC_CAPACITY_FACTOR * kd["L"] // N_EXPERTS
        kd["eg"] = min(N_EXPERTS, GATHER_ROWS // kd["cap"])
    xs = [x_prompt.reshape(b_ctx * l_ctx, D), x_sample.reshape(b_lat * l_lat, D)]

    cond = jnp.concatenate([c_ctx[None, :], c, jnp.zeros((MOD_ROWS - 1 - b_lat, D), F32)], axis=0)
    mod = _mod_all(cond, mod_w, mod_b)

    row = lambda v: v.reshape(1, -1)
    final_row = row(final_g)
    new_k = new_v = None

    for i in range(DEPTH):
        mixer, j = i % N_MIXERS, i // N_MIXERS
        mi = mod[i].reshape(MOD_ROWS * 6, 1, D)
        g1 = row(norm1_g[i])
        if mixer == 0:
            w_in = cv_w_in[j].astype(BF16)
            w_out = cv_w_out[j].astype(BF16)
            for n, kd in enumerate(kinds):
                z = _proj(kd, xs[n], g1, mi, w_in, row(cv_b_in[j]), "glu")
                xs[n] = _conv_tail(kd, z, cv_w_dw[j], row(cv_b_dw[j]), row(cv_ln_g[j]), row(cv_ln_b[j]),
                                   w_out, row(cv_b_out[j]), xs[n], mi)
        elif mixer == 1:
            w_qkv = at_w_qkv[j].astype(BF16)
            w_o = at_w_o[j].astype(BF16)
            qg, kg = row(at_q_norm[j]), row(at_k_norm[j])
            for n, kd in enumerate(kinds):
                if kd["lat"]:
                    cos, sin = _rope_tables(kd["L"])
                    q, k, v = _proj(kd, xs[n], g1, mi, w_qkv, None, "qkv", (qg, kg, cos, sin))
                    ck = cache_k[:, j].reshape(kd["B"], -1, NK)
                    cv = cache_v[:, j].reshape(kd["B"], -1, NK)
                    k_all = jnp.concatenate([ck, k.reshape(kd["B"], kd["L"], NK)], axis=1).astype(BF16)
                    v_all = jnp.concatenate([cv, v.reshape(kd["B"], kd["L"], NK)], axis=1).astype(BF16)
                else:
                    q, k, v = _proj(kd, xs[n], g1, mi, w_qkv, None, "qkv", (qg, kg, None, None))
                    new_k = k.reshape(kd["B"], 1, kd["L"], N_KV_HEADS, HEAD_DIM)
                    new_v = v.reshape(kd["B"], 1, kd["L"], N_KV_HEADS, HEAD_DIM)
                    k_all = k.reshape(kd["B"], kd["L"], NK).astype(BF16)
                    v_all = v.reshape(kd["B"], kd["L"], NK).astype(BF16)
                xs[n] = _attention(kd, q, k_all, v_all, w_o, xs[n], mi)
        else:
            w_in = hy_w_in[j].astype(BF16)
            w_out = hy_w_out[j].astype(BF16)
            hp = LANES
            w1 = _pad_to(hy_f_w1[j], (LANES, hp))
            w2 = _pad_to(hy_f_w2[j], (hp, hp))
            w3 = _pad_to(hy_f_w3[j], (hp, 2 * D))
            b1, fr1 = _pad_to(row(hy_f_b1[j]), (1, hp)), _pad_to(row(hy_f_freq1[j]), (1, hp))
            b2, fr2 = _pad_to(row(hy_f_b2[j]), (1, hp)), _pad_to(row(hy_f_freq2[j]), (1, hp))
            for n, kd in enumerate(kinds):
                L = kd["L"]
                zfeat, decay = _filter_features(L)
                fc, fs, fs_fwd, fs_inv = _dft_tables(L)
                ha, hb, nyq = _hy_filter(L, zfeat, w1, b1, fr1, w2, b2, fr2, w3, decay)
                spec = _hy_spectrum(L, fc, fs, ha, hb, nyq)
                zin = _proj(kd, xs[n], g1, mi, w_in, row(hy_b_in[j]), "plain")
                u, x0 = _hy_pre(kd, zin, hy_w_short[j], row(hy_b_short[j]))
                mats = (fc.astype(BF16), fs_fwd.astype(BF16), fs_inv.astype(BF16))
                y = _hy_longconv(kd, u, x0, mats, spec, row(hy_skip[j]))
                xs[n] = _out_proj(kd, y, w_out, row(hy_b_out[j]), xs[n], mi)

        g2 = row(norm2_g[i])
        w_router_t = moe_router[i].T
        slots, xes, gs = [], [], []
        for n, kd in enumerate(kinds):
            h2, aff = _router(kd, xs[n], g2, mi, w_router_t)
            slot = _route(aff.reshape(kd["B"] * N_EXPERTS, kd["L"]), kd["cap"])
            ng = N_EXPERTS // kd["eg"]
            slot3 = slot.reshape(kd["B"] * ng, kd["eg"], kd["L"])
            aff3 = aff.reshape(kd["B"] * ng, kd["eg"], kd["L"])
            xe, gsel = _gather(kd, slot3, aff3, h2)
            slots.append(slot3)
            xes.append(xe)
            gs.append(gsel)
        yes = _ffn(xes[0], xes[1], gs[0], gs[1], moe_w_gate[i], moe_w_up[i], moe_w_down[i])
        for n, kd in enumerate(kinds):
            xs[n] = _scatter(kd, slots[n], yes[n], xs[n], mi, final_row, final=(i == DEPTH - 1))

    y_prompt = xs[0].reshape(b_ctx, l_ctx, D)
    y_sample = xs[1].reshape(b_lat, l_lat, D)
    return (y_prompt, y_sample, new_k, new_v)
```

```python
import functools
import math

import jax
import jax.numpy as jnp
from jax import lax
from jax.experimental import pallas as pl
from jax.experimental.pallas import tpu as pltpu

F32 = jnp.float32
BF16 = jnp.bfloat16
HIGHEST = lax.Precision.HIGHEST

D = 1024
DEPTH = 4
GRID_W = 64
N_MIXERS = 3
HEAD_DIM = 128
N_HEADS = 8
N_KV_HEADS = 2
KV_GROUP = N_HEADS // N_KV_HEADS
NQ = N_HEADS * HEAD_DIM
NK = N_KV_HEADS * HEAD_DIM
ROPE_AXIS_DIM = HEAD_DIM // 2
ROPE_THETA = 10000.0
CONV_WIDTH = 31
CONV_PAD = CONV_WIDTH // 2
HY_BANDS = 16
HY_DECAY_TARGET = 1e-2
HY_SHORT_PCT = 0.3
HY_LONG_PCT = 1.5
N_EXPERTS = 16
EXPERT_FF = 1024
EC_CAPACITY_FACTOR = 2
NORM_EPS = 1e-6

LANES = 128
SUBLANES = 8
VMEM_LIMIT_BYTES = 56 * 1024 * 1024

TOKEN_TILE = 256
CONV_HALO = 16
CONV_ROW_CHUNK = 32
MOD_ROWS = 8
GATHER_ROWS = 512
FFN_F_TILE = 512
CUMSUM_CHUNK = 256


def _cparams(*sem):
    return pltpu.CompilerParams(dimension_semantics=sem, vmem_limit_bytes=VMEM_LIMIT_BYTES)


def _norm_mod(x, g, sh, sc):
    y = x * lax.rsqrt(jnp.mean(x * x, axis=-1, keepdims=True) + NORM_EPS)
    return (y * g) * (1.0 + sc) + sh


def _split_bf16(a):
    hi = a.astype(BF16)
    return hi, (a - hi.astype(F32)).astype(BF16)


def _dot3(a_hi, a_lo, b_hi, b_lo):
    return (jnp.dot(a_hi, b_hi, preferred_element_type=F32) + jnp.dot(a_lo, b_hi, preferred_element_type=F32)
            + jnp.dot(a_hi, b_lo, preferred_element_type=F32))


def _mod_spec(rowfn, j):
    return pl.BlockSpec((1, 1, D), lambda *idx: (rowfn(*idx) * 6 + j, 0, 0))


def _row_spec(n):
    return pl.BlockSpec((1, n), lambda *idx: (0, 0))


def _full_spec(shape):
    nd = len(shape)
    return pl.BlockSpec(shape, lambda *idx: (0,) * nd)


def _mod_kernel(c_ref, w_ref, b_ref, o_ref):
    cv = c_ref[...]
    s_hi, s_lo = _split_bf16(cv * jax.nn.sigmoid(cv))
    w_hi, w_lo = _split_bf16(w_ref[0])
    o_ref[0] = _dot3(s_hi, s_lo, w_hi, w_lo) + b_ref[0]


def _mod_all(cond, mod_w, mod_b):
    tn = 1536
    n = 6 * D
    return pl.pallas_call(
        _mod_kernel,
        grid=(DEPTH, n // tn),
        in_specs=[
            pl.BlockSpec((MOD_ROWS, D), lambda i, j: (0, 0)),
            pl.BlockSpec((1, D, tn), lambda i, j: (i, 0, j)),
            pl.BlockSpec((1, 1, tn), lambda i, j: (i, 0, j)),
        ],
        out_specs=pl.BlockSpec((1, MOD_ROWS, tn), lambda i, j: (i, 0, j)),
        out_shape=jax.ShapeDtypeStruct((DEPTH, MOD_ROWS, n), F32),
        compiler_params=_cparams("parallel", "parallel"),
        name="mod_all",
    )(cond, mod_w, mod_b.reshape(DEPTH, 1, n))


def _proj_glu_kernel(x_ref, g_ref, sh_ref, sc_ref, w_ref, b_ref, o_ref):
    h = _norm_mod(x_ref[...], g_ref[...], sh_ref[0], sc_ref[0])
    y = jnp.dot(h.astype(BF16), w_ref[...], preferred_element_type=F32) + b_ref[...]
    o_ref[...] = y[:, :D] * jax.nn.sigmoid(y[:, D:])


def _proj_plain_kernel(x_ref, g_ref, sh_ref, sc_ref, w_ref, b_ref, o_ref):
    h = _norm_mod(x_ref[...], g_ref[...], sh_ref[0], sc_ref[0])
    o_ref[...] = jnp.dot(h.astype(BF16), w_ref[...], preferred_element_type=F32) + b_ref[...]


def _head_norm(seg, g):
    return seg * lax.rsqrt(jnp.mean(seg * seg, axis=-1, keepdims=True) + NORM_EPS) * g


def _proj_qkv_kernel(x_ref, g_ref, sh_ref, sc_ref, w_ref, qg_ref, kg_ref, *rest, rope):
    if rope:
        cos_ref, sin_ref, q_ref, k_ref, v_ref = rest
    else:
        q_ref, k_ref, v_ref = rest
    h = _norm_mod(x_ref[...], g_ref[...], sh_ref[0], sc_ref[0])
    y = jnp.dot(h.astype(BF16), w_ref[...], preferred_element_type=F32)
    if rope:
        cos = cos_ref[...]
        sin = sin_ref[...]
        lane = lax.broadcasted_iota(jnp.int32, (y.shape[0], HEAD_DIM), 1)
        even = (lane & 1) == 0
    for hd in range(N_HEADS + N_KV_HEADS):
        seg = y[:, hd * HEAD_DIM:(hd + 1) * HEAD_DIM]
        nrm = _head_norm(seg, qg_ref[...] if hd < N_HEADS else kg_ref[...])
        if rope:
            partner = jnp.where(even, pltpu.roll(nrm, HEAD_DIM - 1, 1), pltpu.roll(nrm, 1, 1))
            nrm = nrm * cos + partner * sin
        if hd < N_HEADS:
            q_ref[:, hd * HEAD_DIM:(hd + 1) * HEAD_DIM] = (nrm * HEAD_DIM ** -0.5).astype(BF16)
        else:
            k_ref[:, (hd - N_HEADS) * HEAD_DIM:(hd - N_HEADS + 1) * HEAD_DIM] = nrm
    v_ref[...] = y[:, NQ + NK:]


def _proj(kind, x, g, mi, w, b, mode, extra=None):
    rows = x.shape[0]
    tm = TOKEN_TILE
    nb = kind["L"] // tm
    rowfn = kind["rowfn"](nb)
    n = w.shape[1]
    x_spec = pl.BlockSpec((tm, D), lambda i: (i, 0))
    base_specs = [x_spec, _row_spec(D), _mod_spec(rowfn, 0), _mod_spec(rowfn, 1), _full_spec((D, n))]
    if mode in ("glu", "plain"):
        n_out = D if mode == "glu" else n
        return pl.pallas_call(
            _proj_glu_kernel if mode == "glu" else _proj_plain_kernel,
            grid=(rows // tm,),
            in_specs=base_specs + [_row_spec(n)],
            out_specs=pl.BlockSpec((tm, n_out), lambda i: (i, 0)),
            out_shape=jax.ShapeDtypeStruct((rows, n_out), F32),
            compiler_params=_cparams("parallel"),
            name="proj_" + mode,
        )(x, g, mi, mi, w, b)
    qg, kg, cos, sin = extra
    rope = cos is not None
    specs = base_specs + [_row_spec(HEAD_DIM), _row_spec(HEAD_DIM)]
    args = [x, g, mi, mi, w, qg, kg]
    if rope:
        specs += [pl.BlockSpec((tm, HEAD_DIM), lambda i: (i % nb, 0))] * 2
        args += [cos, sin]
    return pl.pallas_call(
        functools.partial(_proj_qkv_kernel, rope=rope),
        grid=(rows // tm,),
        in_specs=specs,
        out_specs=[pl.BlockSpec((tm, NQ), lambda i: (i, 0)),
                   pl.BlockSpec((tm, NK), lambda i: (i, 0)),
                   pl.BlockSpec((tm, NK), lambda i: (i, 0))],
        out_shape=[jax.ShapeDtypeStruct((rows, NQ), BF16),
                   jax.ShapeDtypeStruct((rows, NK), F32),
                   jax.ShapeDtypeStruct((rows, NK), F32)],
        compiler_params=_cparams("parallel"),
        name="proj_qkv",
    )(*args)


def _out_proj_kernel(a_ref, w_ref, b_ref, x_ref, gate_ref, o_ref):
    y = jnp.dot(a_ref[...].astype(BF16), w_ref[...], preferred_element_type=F32) + b_ref[...]
    o_ref[...] = x_ref[...] + gate_ref[0] * y


def _out_proj(kind, a, w, b, x, mi):
    rows = x.shape[0]
    tm = TOKEN_TILE
    rowfn = kind["rowfn"](kind["L"] // tm)
    t_spec = pl.BlockSpec((tm, D), lambda i: (i, 0))
    return pl.pallas_call(
        _out_proj_kernel,
        grid=(rows // tm,),
        in_specs=[t_spec, _full_spec((D, D)), _row_spec(D), t_spec, _mod_spec(rowfn, 2)],
        out_specs=t_spec,
        out_shape=jax.ShapeDtypeStruct((rows, D), F32),
        compiler_params=_cparams("parallel"),
        name="out_proj",
    )(a, w, b, x, mi)


def _conv_kernel(zp_ref, zc_ref, zn_ref, wdw_ref, bdw_ref, lg_ref, lb_ref, wo_ref, bo_ref,
                 x_ref, gate_ref, o_ref, zb_ref, ph_ref, act_ref, *, nb, tm):
    j = pl.program_id(0) % nb
    zb_ref[0:CONV_HALO, :] = jnp.where(j > 0, zp_ref[...], 0.0)
    zb_ref[CONV_HALO:CONV_HALO + tm, :] = zc_ref[...]
    zb_ref[CONV_HALO + tm:2 * CONV_HALO + tm, :] = jnp.where(j < nb - 1, zn_ref[...], 0.0)
    span = ph_ref.shape[1]
    for p in range(SUBLANES):
        ph_ref[p] = zb_ref[p:p + span, :]
    first = CONV_HALO - CONV_PAD
    for r0 in range(0, tm, CONV_ROW_CHUNK):
        acc = jnp.broadcast_to(bdw_ref[...], (CONV_ROW_CHUNK, D))
        for k in range(CONV_WIDTH):
            a, p = divmod(first + k, SUBLANES)
            lo = r0 + a * SUBLANES
            acc = acc + ph_ref[p, lo:lo + CONV_ROW_CHUNK, :] * wdw_ref[k:k + 1, :]
        xc = acc - jnp.mean(acc, axis=-1, keepdims=True)
        y = xc * lax.rsqrt(jnp.mean(xc * xc, axis=-1, keepdims=True) + NORM_EPS)
        y = y * lg_ref[...] + lb_ref[...]
        act_ref[r0:r0 + CONV_ROW_CHUNK, :] = (y * jax.nn.sigmoid(y)).astype(BF16)
    out = jnp.dot(act_ref[...], wo_ref[...], preferred_element_type=F32) + bo_ref[...]
    o_ref[...] = x_ref[...] + gate_ref[0] * out


def _conv_tail(kind, z, w_dw, b_dw, ln_g, ln_b, w_out, b_out, x, mi):
    rows = x.shape[0]
    tm = TOKEN_TILE
    nb = kind["L"] // tm
    rowfn = kind["rowfn"](nb)
    hb = tm // CONV_HALO
    n_halo = rows // CONV_HALO
    t_spec = pl.BlockSpec((tm, D), lambda i: (i, 0))
    return pl.pallas_call(
        functools.partial(_conv_kernel, nb=nb, tm=tm),
        grid=(rows // tm,),
        in_specs=[
            pl.BlockSpec((CONV_HALO, D), lambda i: (jnp.maximum(i * hb - 1, 0), 0)),
            t_spec,
            pl.BlockSpec((CONV_HALO, D), lambda i: (jnp.minimum((i + 1) * hb, n_halo - 1), 0)),
            _full_spec((CONV_WIDTH, D)), _row_spec(D), _row_spec(D), _row_spec(D),
            _full_spec((D, D)), _row_spec(D), t_spec, _mod_spec(rowfn, 2),
        ],
        out_specs=t_spec,
        out_shape=jax.ShapeDtypeStruct((rows, D), F32),
        scratch_shapes=[pltpu.VMEM((tm + 2 * CONV_HALO, D), F32),
                        pltpu.VMEM((SUBLANES, tm + 2 * CONV_HALO - SUBLANES, D), F32),
                        pltpu.VMEM((tm, D), BF16)],
        compiler_params=_cparams("parallel"),
        name="conv_tail",
    )(z, z, z, w_dw, b_dw, ln_g, ln_b, w_out, b_out, x, mi)


def _attn_kernel(q_ref, k_ref, v_ref, wo_ref, x_ref, gate_ref, o_ref):
    outs = []
    for hd in range(N_HEADS):
        kv = hd // KV_GROUP
        qh = q_ref[:, hd * HEAD_DIM:(hd + 1) * HEAD_DIM]
        kh = k_ref[0, :, kv * HEAD_DIM:(kv + 1) * HEAD_DIM]
        vh = v_ref[0, :, kv * HEAD_DIM:(kv + 1) * HEAD_DIM]
        s = lax.dot_general(qh, kh, (((1,), (1,)), ((), ())), preferred_element_type=F32)
        p = jnp.exp(s - jnp.max(s, axis=-1, keepdims=True))
        l = jnp.sum(p, axis=-1, keepdims=True)
        o = jnp.dot(p.astype(BF16), vh, preferred_element_type=F32)
        outs.append(o / l)
    o = jnp.concatenate(outs, axis=1).astype(BF16)
    y = jnp.dot(o, wo_ref[...], preferred_element_type=F32)
    o_ref[...] = x_ref[...] + gate_ref[0] * y


def _attention(kind, q, k_all, v_all, w_o, x, mi):
    rows = x.shape[0]
    tq = TOKEN_TILE
    nb = kind["L"] // tq
    rowfn = kind["rowfn"](nb)
    s_len = k_all.shape[1]
    t_spec = pl.BlockSpec((tq, D), lambda i: (i, 0))
    kv_spec = pl.BlockSpec((1, s_len, NK), lambda i: (i // nb, 0, 0))
    return pl.pallas_call(
        _attn_kernel,
        grid=(rows // tq,),
        in_specs=[t_spec, kv_spec, kv_spec, _full_spec((D, D)), t_spec, _mod_spec(rowfn, 2)],
        out_specs=t_spec,
        out_shape=jax.ShapeDtypeStruct((rows, D), F32),
        compiler_params=_cparams("parallel"),
        name="attention",
    )(q, k_all, v_all, w_o, x, mi)


def _hy_pre_kernel(zp_ref, zc_ref, zn_ref, w_ref, b_ref, u_ref, x0_ref, zb_ref, *, nb, tm):
    j = pl.program_id(0) % nb
    zb_ref[0:SUBLANES, :] = jnp.where(j > 0, zp_ref[...], 0.0)
    zb_ref[SUBLANES:SUBLANES + tm, :] = zc_ref[...]
    zb_ref[SUBLANES + tm:2 * SUBLANES + tm, :] = jnp.where(j < nb - 1, zn_ref[...], 0.0)
    rc = CONV_ROW_CHUNK
    for r0 in range(0, tm, rc):
        parts = []
        for part in range(3):
            cs = slice(part * D, (part + 1) * D)
            lo = r0 + SUBLANES - 1
            z = (zb_ref[lo:lo + rc, cs] * w_ref[0:1, cs] + zb_ref[lo + 1:lo + 1 + rc, cs] * w_ref[1:2, cs]
                 + zb_ref[lo + 2:lo + 2 + rc, cs] * w_ref[2:3, cs] + b_ref[:, cs])
            parts.append(z)
        x0_ref[r0:r0 + rc, :] = parts[0]
        u_ref[r0:r0 + rc, :] = parts[2] * parts[1]


def _hy_pre(kind, zin, w_short, b_short):
    rows = zin.shape[0]
    tm = TOKEN_TILE
    nb = kind["L"] // tm
    hb = tm // SUBLANES
    n_halo = rows // SUBLANES
    t_spec = pl.BlockSpec((tm, D), lambda i: (i, 0))
    return pl.pallas_call(
        functools.partial(_hy_pre_kernel, nb=nb, tm=tm),
        grid=(rows // tm,),
        in_specs=[
            pl.BlockSpec((SUBLANES, 3 * D), lambda i: (jnp.maximum(i * hb - 1, 0), 0)),
            pl.BlockSpec((tm, 3 * D), lambda i: (i, 0)),
            pl.BlockSpec((SUBLANES, 3 * D), lambda i: (jnp.minimum((i + 1) * hb, n_halo - 1), 0)),
            _full_spec((3, 3 * D)), _row_spec(3 * D),
        ],
        out_specs=[t_spec, t_spec],
        out_shape=[jax.ShapeDtypeStruct((rows, D), F32), jax.ShapeDtypeStruct((rows, D), F32)],
        scratch_shapes=[pltpu.VMEM((tm + 2 * SUBLANES, 3 * D), F32)],
        compiler_params=_cparams("parallel"),
        name="hy_pre",
    )(zin, zin, zin, w_short, b_short)


def _hy_filter_kernel(z_ref, w1_ref, b1_ref, f1_ref, w2_ref, b2_ref, f2_ref, w3f_ref, w3b_ref,
                      dec_ref, ha_hi_ref, ha_lo_ref, hb_hi_ref, hb_lo_ref, nyq_ref, hid_ref):
    @pl.when(pl.program_id(0) == 0)
    def _():
        h1 = jnp.sin(f1_ref[...] * (jnp.dot(z_ref[...], w1_ref[...], preferred_element_type=F32,
                                            precision=HIGHEST) + b1_ref[...]))
        hid_ref[...] = jnp.sin(f2_ref[...] * (jnp.dot(h1, w2_ref[...], preferred_element_type=F32,
                                                      precision=HIGHEST) + b2_ref[...]))

    f = hid_ref[...]
    dec = dec_ref[...]
    hf = jnp.dot(f, w3f_ref[...], preferred_element_type=F32, precision=HIGHEST) * dec
    hb = jnp.dot(f, w3b_ref[...], preferred_element_type=F32, precision=HIGHEST) * dec
    row = lax.broadcasted_iota(jnp.int32, hf.shape, 0)
    hb = jnp.where(row == 0, 0.0, hb)
    ha = hf + hb
    ha_hi_ref[...], ha_lo_ref[...] = _split_bf16(ha)
    hb_hi_ref[...], hb_lo_ref[...] = _split_bf16(hf - hb)
    sign = jnp.where((row & 1) == 0, 1.0, -1.0)
    nyq_ref[...] = jnp.sum(ha * sign, axis=0, keepdims=True)


def _hy_filter(L, zfeat, w1, b1, fr1, w2, b2, fr2, w3, decay):
    tc = 256
    nct = D // tc
    hp = w1.shape[1]
    c_spec = pl.BlockSpec((L, tc), lambda c: (0, c))
    return pl.pallas_call(
        _hy_filter_kernel,
        grid=(nct,),
        in_specs=[
            _full_spec(zfeat.shape), _full_spec(w1.shape), _row_spec(hp), _row_spec(hp),
            _full_spec(w2.shape), _row_spec(hp), _row_spec(hp),
            pl.BlockSpec((hp, tc), lambda c: (0, c)),
            pl.BlockSpec((hp, tc), lambda c: (0, nct + c)),
            c_spec,
        ],
        out_specs=[c_spec] * 4 + [pl.BlockSpec((1, tc), lambda c: (0, c))],
        out_shape=[jax.ShapeDtypeStruct((L, D), BF16)] * 4 + [jax.ShapeDtypeStruct((1, D), F32)],
        scratch_shapes=[pltpu.VMEM((L, hp), F32)],
        compiler_params=_cparams("arbitrary"),
        name="hy_filter",
    )(zfeat, w1, b1, fr1, w2, b2, fr2, w3, w3, decay)


def _hy_spectrum_kernel(fc_hi_ref, fc_lo_ref, fs_hi_ref, fs_lo_ref, ha_hi_ref, ha_lo_ref, hb_hi_ref, hb_lo_ref,
                        nyq_ref, kra_ref, krb_ref, ki_ref, *, L, ft):
    kr = _dot3(fc_hi_ref[...], fc_lo_ref[...], ha_hi_ref[...], ha_lo_ref[...])
    ki = _dot3(fs_hi_ref[...], fs_lo_ref[...], hb_hi_ref[...], hb_lo_ref[...])
    row = lax.broadcasted_iota(jnp.int32, kr.shape, 0) + pl.program_id(1) * ft
    scale = jnp.where(row == 0, 0.5 / L, 1.0 / L)
    kr = kr * scale
    kra_ref[...] = kr
    krb_ref[...] = jnp.where(row == 0, nyq_ref[...] * (0.5 / L), kr)
    ki_ref[...] = ki * scale


def _hy_spectrum(L, fc_split, fs_split, taps, nyq):
    ft = min(L, 256)
    tc = 512
    mat_spec = pl.BlockSpec((ft, L), lambda c, k: (k, 0))
    h_spec = pl.BlockSpec((L, tc), lambda c, k: (0, c))
    o_spec = pl.BlockSpec((ft, tc), lambda c, k: (k, c))
    return pl.pallas_call(
        functools.partial(_hy_spectrum_kernel, L=L, ft=ft),
        grid=(D // tc, L // ft),
        in_specs=[mat_spec] * 4 + [h_spec] * 4 + [pl.BlockSpec((1, tc), lambda c, k: (0, c))],
        out_specs=[o_spec, o_spec, o_spec],
        out_shape=[jax.ShapeDtypeStruct((L, D), F32)] * 3,
        compiler_params=_cparams("parallel", "parallel"),
        name="hy_spectrum",
    )(*fc_split, *fs_split, *taps, nyq)


def _hy_longconv_kernel(u_ref, fc_ref, fs_ref, gc_ref, gs_ref, kra_ref, krb_ref, ki_ref,
                        x0_ref, skip_ref, o_ref, acc_ref, ub_ref):
    k = pl.program_id(2)

    @pl.when(k == 0)
    def _():
        acc_ref[...] = jnp.zeros_like(acc_ref)
        ub_ref[...] = u_ref[...].astype(BF16)

    ub = ub_ref[...]
    ur = jnp.dot(fc_ref[...], ub, preferred_element_type=F32)
    ui = jnp.dot(fs_ref[...], ub, preferred_element_type=F32)
    kra = kra_ref[...]
    krb = krb_ref[...]
    ki = ki_ref[...]
    yr = (ur * kra - ui * ki).astype(BF16)
    yi = (ur * ki + ui * krb).astype(BF16)
    acc_ref[...] += (jnp.dot(gc_ref[...], yr, preferred_element_type=F32)
                     + jnp.dot(gs_ref[...], yi, preferred_element_type=F32))

    @pl.when(k == pl.num_programs(2) - 1)
    def _():
        o_ref[...] = (acc_ref[...] + u_ref[...] * skip_ref[...]) * x0_ref[...]


def _hy_longconv(kind, u, x0, mats, spec, skip):
    B, L = kind["B"], kind["L"]
    fc, fs, fst = mats
    kra, krb, ki = spec
    ft = min(L, 256)
    tc = 1024 if L <= 256 else 512
    d_spec = pl.BlockSpec((L, tc), lambda b, c, k: (b, c))
    fwd_spec = pl.BlockSpec((ft, L), lambda b, c, k: (k, 0))
    inv_spec = pl.BlockSpec((L, ft), lambda b, c, k: (0, k))
    k_spec = pl.BlockSpec((ft, tc), lambda b, c, k: (k, c))
    return pl.pallas_call(
        _hy_longconv_kernel,
        grid=(B, D // tc, L // ft),
        in_specs=[d_spec, fwd_spec, fwd_spec, inv_spec, inv_spec, k_spec, k_spec, k_spec, d_spec,
                  pl.BlockSpec((1, tc), lambda b, c, k: (0, c))],
        out_specs=d_spec,
        out_shape=jax.ShapeDtypeStruct((B * L, D), F32),
        scratch_shapes=[pltpu.VMEM((L, tc), F32), pltpu.VMEM((L, tc), BF16)],
        compiler_params=_cparams("parallel", "parallel", "arbitrary"),
        name="hy_longconv",
    )(u, fc, fs, fc, fst, kra, krb, ki, x0, skip)


def _router_kernel(x_ref, g_ref, sh_ref, sc_ref, wr_ref, h_ref, aff_ref):
    h = _norm_mod(x_ref[...], g_ref[...], sh_ref[0], sc_ref[0])
    h_hi, h_lo = _split_bf16(h)
    h_ref[...] = h_hi
    nt = (((1,), (1,)), ((), ()))
    by_hi = lax.dot_general(wr_ref[...], h_hi, nt, preferred_element_type=F32)
    by_lo = lax.dot_general(wr_ref[0:N_EXPERTS, :], h_lo, nt, preferred_element_type=F32)
    logits = by_hi[0:N_EXPERTS] + by_hi[N_EXPERTS:] + by_lo
    e = jnp.exp(logits - jnp.max(logits, axis=0, keepdims=True))
    aff_ref[0] = e / jnp.sum(e, axis=0, keepdims=True)


def _router(kind, x, g, mi, w_router_t):
    rows = x.shape[0]
    tm = TOKEN_TILE
    nb = kind["L"] // tm
    rowfn = kind["rowfn"](nb)
    return pl.pallas_call(
        _router_kernel,
        grid=(rows // tm,),
        in_specs=[pl.BlockSpec((tm, D), lambda i: (i, 0)), _row_spec(D), _mod_spec(rowfn, 3),
                  _mod_spec(rowfn, 4), _full_spec((2 * N_EXPERTS, D))],
        out_specs=[pl.BlockSpec((tm, D), lambda i: (i, 0)),
                   pl.BlockSpec((1, N_EXPERTS, tm), lambda i: (i // nb, 0, i % nb))],
        out_shape=[jax.ShapeDtypeStruct((rows, D), BF16),
                   jax.ShapeDtypeStruct((kind["B"], N_EXPERTS, kind["L"]), F32)],
        compiler_params=_cparams("parallel"),
        name="router",
    )(x, g, mi, mi, w_router_t)


def _lane_cumsum(mask_f32, tri):
    rows, n = mask_f32.shape
    run = jnp.zeros((rows, 1), F32)
    pieces = []
    for c0 in range(0, n, CUMSUM_CHUNK):
        chunk = mask_f32[:, c0:c0 + CUMSUM_CHUNK]
        pieces.append(jnp.dot(chunk.astype(BF16), tri, preferred_element_type=F32) + run)
        run = run + jnp.sum(chunk, axis=1, keepdims=True)
    return jnp.concatenate(pieces, axis=1) if len(pieces) > 1 else pieces[0]


def _route_kernel(aff_ref, slot_ref, *, cap):
    bits = pltpu.bitcast(aff_ref[...], jnp.int32)
    rows = bits.shape[0]
    thr = jnp.zeros((rows, 1), jnp.int32)
    capf = float(cap)
    for bit in range(30, -1, -1):
        cand = thr | (1 << bit)
        cnt = jnp.sum(jnp.where(bits >= cand, 1.0, 0.0), axis=1, keepdims=True)
        thr = jnp.where(cnt >= capf, cand, thr)
    r_i = lax.broadcasted_iota(jnp.int32, (CUMSUM_CHUNK, CUMSUM_CHUNK), 0)
    c_i = lax.broadcasted_iota(jnp.int32, (CUMSUM_CHUNK, CUMSUM_CHUNK), 1)
    tri = jnp.where(r_i <= c_i, 1.0, 0.0).astype(BF16)
    gt = jnp.where(bits > thr, 1.0, 0.0)
    eq = jnp.where(bits == thr, 1.0, 0.0)
    need = capf - jnp.sum(gt, axis=1, keepdims=True)
    eq_rank = _lane_cumsum(eq, tri)
    sel = gt + eq * jnp.where(eq_rank <= need, 1.0, 0.0)
    pos = _lane_cumsum(sel, tri)
    slot_ref[...] = jnp.where(sel > 0.5, pos - 1.0, -1.0).astype(jnp.int32)


def _route(aff2d, cap):
    return pl.pallas_call(
        functools.partial(_route_kernel, cap=cap),
        out_shape=jax.ShapeDtypeStruct(aff2d.shape, jnp.int32),
        compiler_params=pltpu.CompilerParams(vmem_limit_bytes=VMEM_LIMIT_BYTES),
        name="route",
    )(aff2d)


def _dispatch_mask(slot, cap):
    eg, n = slot.shape
    iota = lax.broadcasted_iota(jnp.int32, (cap, n), 0)
    masks = [iota == slot[e:e + 1, :] for e in range(eg)]
    return masks


def _gather_kernel(slot_ref, aff_ref, h_ref, xe_ref, g_ref, *, cap, eg):
    slot = slot_ref[0]
    aff = aff_ref[0]
    masks = _dispatch_mask(slot, cap)
    onehot = jnp.concatenate([jnp.where(m, 1.0, 0.0) for m in masks], axis=0).astype(BF16)
    xe = jnp.dot(onehot, h_ref[...], preferred_element_type=F32)
    xe_ref[...] = xe.reshape(eg, cap, D).astype(BF16)
    for e in range(eg):
        gsel = jnp.sum(jnp.where(masks[e], aff[e:e + 1, :], 0.0), axis=1, keepdims=True)
        g_ref[e] = jnp.broadcast_to(gsel, (cap, LANES))


def _gather(kind, slot3, aff3, h2):
    B, L, cap, eg = kind["B"], kind["L"], kind["cap"], kind["eg"]
    ng = N_EXPERTS // eg
    r_spec = pl.BlockSpec((1, eg, L), lambda b, e: (b * ng + e, 0, 0))
    return pl.pallas_call(
        functools.partial(_gather_kernel, cap=cap, eg=eg),
        grid=(B, ng),
        in_specs=[r_spec, r_spec, pl.BlockSpec((L, D), lambda b, e: (b, 0))],
        out_specs=[pl.BlockSpec((eg, cap, D), lambda b, e: (e, b, 0)),
                   pl.BlockSpec((eg, cap, LANES), lambda b, e: (e, b, 0))],
        out_shape=[jax.ShapeDtypeStruct((N_EXPERTS, B * cap, D), BF16),
                   jax.ShapeDtypeStruct((N_EXPERTS, B * cap, LANES), F32)],
        compiler_params=_cparams("parallel", "parallel"),
        name="moe_gather",
    )(slot3, aff3, h2)


def _ffn_kernel(xa_ref, xb_ref, ga_ref, gb_ref, wg_ref, wu_ref, wd_ref, ya_ref, yb_ref, acc_ref):
    f = pl.program_id(1)

    @pl.when(f == 0)
    def _():
        acc_ref[...] = jnp.zeros_like(acc_ref)

    wg = wg_ref[0, 0].astype(BF16)
    wu = wu_ref[0, 0].astype(BF16)
    wd = wd_ref[0, 0].astype(BF16)
    na = xa_ref.shape[1]
    for x_ref, r0 in ((xa_ref, 0), (xb_ref, na)):
        x = x_ref[0]
        a = jnp.dot(x, wg, preferred_element_type=F32)
        u = jnp.dot(x, wu, preferred_element_type=F32)
        mid = (a * jax.nn.sigmoid(a) * u).astype(BF16)
        acc_ref[r0:r0 + x.shape[0], :] += jnp.dot(mid, wd, preferred_element_type=F32)

    @pl.when(f == pl.num_programs(1) - 1)
    def _():
        ya_ref[0] = (acc_ref[0:na, :] * ga_ref[0][:, 0:1]).astype(BF16)
        yb_ref[0] = (acc_ref[na:, :] * gb_ref[0][:, 0:1]).astype(BF16)


def _ffn(xe_a, xe_b, g_a, g_b, w_gate, w_up, w_down, layer):
    na, nb_ = xe_a.shape[1], xe_b.shape[1]
    tf = FFN_F_TILE
    return pl.pallas_call(
        _ffn_kernel,
        grid=(N_EXPERTS, EXPERT_FF // tf),
        in_specs=[
            pl.BlockSpec((1, na, D), lambda e, f: (e, 0, 0)),
            pl.BlockSpec((1, nb_, D), lambda e, f: (e, 0, 0)),
            pl.BlockSpec((1, na, LANES), lambda e, f: (e, 0, 0)),
            pl.BlockSpec((1, nb_, LANES), lambda e, f: (e, 0, 0)),
            pl.BlockSpec((1, 1, D, tf), lambda e, f: (layer, e, 0, f)),
            pl.BlockSpec((1, 1, D, tf), lambda e, f: (layer, e, 0, f)),
            pl.BlockSpec((1, 1, tf, D), lambda e, f: (layer, e, f, 0)),
        ],
        out_specs=[pl.BlockSpec((1, na, D), lambda e, f: (e, 0, 0)),
                   pl.BlockSpec((1, nb_, D), lambda e, f: (e, 0, 0))],
        out_shape=[jax.ShapeDtypeStruct((N_EXPERTS, na, D), BF16),
                   jax.ShapeDtypeStruct((N_EXPERTS, nb_, D), BF16)],
        scratch_shapes=[pltpu.VMEM((na + nb_, D), F32)],
        compiler_params=_cparams("parallel", "arbitrary"),
        name="moe_ffn",
    )(xe_a, xe_b, g_a, g_b, w_gate, w_up, w_down)


def _scatter_kernel(slot_ref, ye_ref, x_ref, gate_ref, fg_ref, o_ref, acc_ref, *, cap, eg, final):
    e = pl.program_id(2)

    @pl.when(e == 0)
    def _():
        acc_ref[...] = jnp.zeros_like(acc_ref)

    masks = _dispatch_mask(slot_ref[0], cap)
    onehot = jnp.concatenate([jnp.where(m, 1.0, 0.0) for m in masks], axis=0).astype(BF16)
    ye = ye_ref[...].reshape(eg * cap, D)
    acc_ref[...] += lax.dot_general(onehot, ye, (((0,), (0,)), ((), ())), preferred_element_type=F32)

    @pl.when(e == pl.num_programs(2) - 1)
    def _():
        x = x_ref[...] + gate_ref[0] * acc_ref[...]
        if final:
            x = x * lax.rsqrt(jnp.mean(x * x, axis=-1, keepdims=True) + NORM_EPS) * fg_ref[...]
        o_ref[...] = x


def _scatter(kind, slot3, ye, x, mi, final_g, final):
    B, L, cap, eg = kind["B"], kind["L"], kind["cap"], kind["eg"]
    ng = N_EXPERTS // eg
    lt = min(L, 1024)
    nl = L // lt
    rowfn = lambda b, l, e: kind["rowfn"](1)(b)
    x_spec = pl.BlockSpec((lt, D), lambda b, l, e: (b * nl + l, 0))
    return pl.pallas_call(
        functools.partial(_scatter_kernel, cap=cap, eg=eg, final=final),
        grid=(B, nl, ng),
        in_specs=[pl.BlockSpec((1, eg, lt), lambda b, l, e: (b * ng + e, 0, l)),
                  pl.BlockSpec((eg, cap, D), lambda b, l, e: (e, b, 0)),
                  x_spec, _mod_spec(rowfn, 5), _row_spec(D)],
        out_specs=x_spec,
        out_shape=jax.ShapeDtypeStruct((B * L, D), F32),
        scratch_shapes=[pltpu.VMEM((lt, D), F32)],
        compiler_params=_cparams("parallel", "parallel", "arbitrary"),
        name="moe_scatter",
    )(slot3, ye, x, mi, final_g)


def _rope_tables(L):
    n_rows = L // GRID_W
    rows = jnp.repeat(jnp.arange(n_rows), GRID_W).astype(F32)
    cols = jnp.tile(jnp.arange(GRID_W), n_rows).astype(F32)
    inv = ROPE_THETA ** (-jnp.arange(0, ROPE_AXIS_DIM, 2, dtype=F32) / ROPE_AXIS_DIM)
    ang = jnp.concatenate([rows[:, None] * inv, cols[:, None] * inv], axis=-1)
    cos = jnp.repeat(jnp.cos(ang), 2, axis=-1)
    sin = jnp.repeat(jnp.sin(ang), 2, axis=-1)
    sign = jnp.tile(jnp.array([-1.0, 1.0], F32), HEAD_DIM // 2)
    return cos, sin * sign


def _dft_tables(L):
    r = min(L, 64)
    s = jnp.arange(L, dtype=jnp.int32)

    def small(f):
        ang = ((f[:, None] * s[None, :]) % (2 * L)).astype(F32) * (math.pi / L)
        return jnp.cos(ang), jnp.sin(ang)

    c0, s0 = small(jnp.arange(r, dtype=jnp.int32))
    c1, s1 = small(jnp.arange(L // r, dtype=jnp.int32) * r)
    fc = (c1[:, None, :] * c0[None] - s1[:, None, :] * s0[None]).reshape(L, L)
    fs = -(s1[:, None, :] * c0[None] + c1[:, None, :] * s0[None]).reshape(L, L)
    sign = jnp.where(s % 2 == 0, 1.0, -1.0).astype(F32)
    return fc, fs, fs.at[0, :].set(sign), fs.at[:, 0].set(sign)


def _filter_features(L):
    t = jnp.arange(L, dtype=F32) / L
    bands = jnp.arange(1, HY_BANDS + 1, dtype=F32)
    ph = 2.0 * math.pi * t[:, None] * bands
    z = jnp.concatenate([t[:, None], jnp.sin(ph), jnp.cos(ph)], axis=-1)
    z = jnp.pad(z, ((0, 0), (0, LANES - z.shape[1])))
    rates = jnp.abs(jnp.linspace(math.log(HY_DECAY_TARGET) / HY_LONG_PCT,
                                 math.log(HY_DECAY_TARGET) / HY_SHORT_PCT, D, dtype=F32))
    return z, jnp.exp(-t[:, None] * rates)


def _pad_to(a, shape):
    return jnp.pad(a, [(0, s - d) for d, s in zip(a.shape, shape)])


def kernel(x_prompt, x_sample, cache_k, cache_v, c, c_ctx, mod_w, mod_b, norm1_g, norm2_g, cv_w_in, cv_b_in, cv_w_dw, cv_b_dw, cv_ln_g, cv_ln_b, cv_w_out, cv_b_out, at_w_qkv, at_w_o, at_q_norm, at_k_norm, hy_w_in, hy_b_in, hy_w_short, hy_b_short, hy_f_w1, hy_f_b1, hy_f_freq1, hy_f_w2, hy_f_b2, hy_f_freq2, hy_f_w3, hy_skip, hy_w_out, hy_b_out, moe_router, moe_w_gate, moe_w_up, moe_w_down, final_g):
    b_ctx, l_ctx, _ = x_prompt.shape
    b_lat, l_lat, _ = x_sample.shape
    kinds = [
        dict(B=b_ctx, L=l_ctx, lat=False, rowfn=lambda nb: (lambda i: 0)),
        dict(B=b_lat, L=l_lat, lat=True, rowfn=lambda nb: (lambda i: 1 + i // nb)),
    ]
    for kd in kinds:
        kd["cap"] = EC_CAPACITY_FACTOR * kd["L"] // N_EXPERTS
        kd["eg"] = min(N_EXPERTS, GATHER_ROWS // kd["cap"])
    xs = [x_prompt.reshape(b_ctx * l_ctx, D), x_sample.reshape(b_lat * l_lat, D)]

    cond = jnp.concatenate([c_ctx[None, :], c, jnp.zeros((MOD_ROWS - 1 - b_lat, D), F32)], axis=0)
    mod = _mod_all(cond, mod_w, mod_b)

    row = lambda v: v.reshape(1, -1)
    final_row = row(final_g)
    new_k = new_v = None

    for i in range(DEPTH):
        mixer, j = i % N_MIXERS, i // N_MIXERS
        mi = mod[i].reshape(MOD_ROWS * 6, 1, D)
        g1 = row(norm1_g[i])
        if mixer == 0:
            w_in = cv_w_in[j].astype(BF16)
            w_out = cv_w_out[j].astype(BF16)
            for n, kd in enumerate(kinds):
                z = _proj(kd, xs[n], g1, mi, w_in, row(cv_b_in[j]), "glu")
                xs[n] = _conv_tail(kd, z, cv_w_dw[j], row(cv_b_dw[j]), row(cv_ln_g[j]), row(cv_ln_b[j]),
                                   w_out, row(cv_b_out[j]), xs[n], mi)
        elif mixer == 1:
            w_qkv = at_w_qkv[j].astype(BF16)
            w_o = at_w_o[j].astype(BF16)
            qg, kg = row(at_q_norm[j]), row(at_k_norm[j])
            for n, kd in enumerate(kinds):
                if kd["lat"]:
                    cos, sin = _rope_tables(kd["L"])
                    q, k, v = _proj(kd, xs[n], g1, mi, w_qkv, None, "qkv", (qg, kg, cos, sin))
                    ck = cache_k[:, j].reshape(kd["B"], -1, NK)
                    cv = cache_v[:, j].reshape(kd["B"], -1, NK)
                    k_all = jnp.concatenate([ck, k.reshape(kd["B"], kd["L"], NK)], axis=1).astype(BF16)
                    v_all = jnp.concatenate([cv, v.reshape(kd["B"], kd["L"], NK)], axis=1).astype(BF16)
                else:
                    q, k, v = _proj(kd, xs[n], g1, mi, w_qkv, None, "qkv", (qg, kg, None, None))
                    new_k = k.reshape(kd["B"], 1, kd["L"], N_KV_HEADS, HEAD_DIM)
                    new_v = v.reshape(kd["B"], 1, kd["L"], N_KV_HEADS, HEAD_DIM)
                    k_all = k.reshape(kd["B"], kd["L"], NK).astype(BF16)
                    v_all = v.reshape(kd["B"], kd["L"], NK).astype(BF16)
                xs[n] = _attention(kd, q, k_all, v_all, w_o, xs[n], mi)
        else:
            w_in = hy_w_in[j].astype(BF16)
            w_out = hy_w_out[j].astype(BF16)
            hp = LANES
            w1 = _pad_to(hy_f_w1[j], (LANES, hp))
            w2 = _pad_to(hy_f_w2[j], (hp, hp))
            w3 = _pad_to(hy_f_w3[j], (hp, 2 * D))
            b1, fr1 = _pad_to(row(hy_f_b1[j]), (1, hp)), _pad_to(row(hy_f_freq1[j]), (1, hp))
            b2, fr2 = _pad_to(row(hy_f_b2[j]), (1, hp)), _pad_to(row(hy_f_freq2[j]), (1, hp))
            for n, kd in enumerate(kinds):
                L = kd["L"]
                zfeat, decay = _filter_features(L)
                fc, fs, fs_fwd, fs_inv = _dft_tables(L)
                *taps, nyq = _hy_filter(L, zfeat, w1, b1, fr1, w2, b2, fr2, w3, decay)
                spec = _hy_spectrum(L, _split_bf16(fc), _split_bf16(fs), taps, nyq)
                zin = _proj(kd, xs[n], g1, mi, w_in, row(hy_b_in[j]), "plain")
                u, x0 = _hy_pre(kd, zin, hy_w_short[j], row(hy_b_short[j]))
                mats = (fc.astype(BF16), fs_fwd.astype(BF16), fs_inv.astype(BF16))
                y = _hy_longconv(kd, u, x0, mats, spec, row(hy_skip[j]))
                xs[n] = _out_proj(kd, y, w_out, row(hy_b_out[j]), xs[n], mi)

        g2 = row(norm2_g[i])
        w_router_t = jnp.concatenate(_split_bf16(moe_router[i].T), axis=0)
        slots, xes, gs = [], [], []
        for n, kd in enumerate(kinds):
            h2, aff = _router(kd, xs[n], g2, mi, w_router_t)
            slot = _route(aff.reshape(kd["B"] * N_EXPERTS, kd["L"]), kd["cap"])
            ng = N_EXPERTS // kd["eg"]
            slot3 = slot.reshape(kd["B"] * ng, kd["eg"], kd["L"])
            aff3 = aff.reshape(kd["B"] * ng, kd["eg"], kd["L"])
            xe, gsel = _gather(kd, slot3, aff3, h2)
            slots.append(slot3)
            xes.append(xe)
            gs.append(gsel)
        yes = _ffn(xes[0], xes[1], gs[0], gs[1], moe_w_gate, moe_w_up, moe_w_down, i)
        for n, kd in enumerate(kinds):
            xs[n] = _scatter(kd, slots[n], yes[n], xs[n], mi, final_row, final=(i == DEPTH - 1))

    y_prompt = xs[0].reshape(b_ctx, l_ctx, D)
    y_sample = xs[1].reshape(b_lat, l_lat, D)
    return (y_prompt, y_sample, new_k, new_v)
```

```python
import functools
import math

import jax
import jax.numpy as jnp
from jax import lax
from jax.experimental import pallas as pl
from jax.experimental.pallas import tpu as pltpu

F32 = jnp.float32
BF16 = jnp.bfloat16
HIGHEST = lax.Precision.HIGHEST

D = 1024
DEPTH = 4
GRID_W = 64
N_MIXERS = 3
HEAD_DIM = 128
N_HEADS = 8
N_KV_HEADS = 2
KV_GROUP = N_HEADS // N_KV_HEADS
NQ = N_HEADS * HEAD_DIM
NK = N_KV_HEADS * HEAD_DIM
ROPE_AXIS_DIM = HEAD_DIM // 2
ROPE_THETA = 10000.0
CONV_WIDTH = 31
CONV_PAD = CONV_WIDTH // 2
SHORT_WIDTH = 3
HY_BANDS = 16
HY_DECAY_TARGET = 1e-2
HY_SHORT_PCT = 0.3
HY_LONG_PCT = 1.5
N_EXPERTS = 16
EXPERT_FF = 1024
EC_CAPACITY_FACTOR = 2
NORM_EPS = 1e-6

LANES = 128
SUBLANES = 8
VMEM_LIMIT_BYTES = 56 * 1024 * 1024

TOKEN_TILE = 256
HALO = 16
CONV_ROW_CHUNK = 32
MOD_ROWS = 8
GATHER_ROWS = 512
DISPATCH_TOKENS = 1024
FFN_F_TILE = 512
CUMSUM_CHUNK = 256


def _cparams(*sem):
    return pltpu.CompilerParams(dimension_semantics=sem, vmem_limit_bytes=VMEM_LIMIT_BYTES)


def _norm_mod(x, g, sh, sc):
    y = x * lax.rsqrt(jnp.mean(x * x, axis=-1, keepdims=True) + NORM_EPS)
    return (y * g) * (1.0 + sc) + sh


def _split_bf16(a):
    hi = a.astype(BF16)
    return hi, (a - hi.astype(F32)).astype(BF16)


def _dot3(a_hi, a_lo, b_hi, b_lo):
    return (jnp.dot(a_hi, b_hi, preferred_element_type=F32) + jnp.dot(a_lo, b_hi, preferred_element_type=F32)
            + jnp.dot(a_hi, b_lo, preferred_element_type=F32))


def _mod_spec(rowfn, j):
    return pl.BlockSpec((1, 1, D), lambda *idx: (rowfn(*idx) * 6 + j, 0, 0))


def _row_spec(n):
    return pl.BlockSpec((1, n), lambda *idx: (0, 0))


def _full_spec(shape):
    nd = len(shape)
    return pl.BlockSpec(shape, lambda *idx: (0,) * nd)


def _tile_spec(tm, n=D):
    return pl.BlockSpec((tm, n), lambda i: (i, 0))


def _halo_specs(rows, tm, n=D):
    hb = tm // HALO
    last = rows // HALO - 1
    return (pl.BlockSpec((HALO, n), lambda i: (jnp.maximum(i * hb - 1, 0), 0)),
            pl.BlockSpec((HALO, n), lambda i: (jnp.minimum((i + 1) * hb, last), 0)))


def _mod_kernel(c_ref, w_ref, b_ref, o_ref):
    cv = c_ref[...]
    s_hi, s_lo = _split_bf16(cv * jax.nn.sigmoid(cv))
    w_hi, w_lo = _split_bf16(w_ref[0])
    o_ref[0] = _dot3(s_hi, s_lo, w_hi, w_lo) + b_ref[0]


def _mod_all(cond, mod_w, mod_b):
    tn = 1536
    n = 6 * D
    return pl.pallas_call(
        _mod_kernel,
        grid=(DEPTH, n // tn),
        in_specs=[
            pl.BlockSpec((MOD_ROWS, D), lambda i, j: (0, 0)),
            pl.BlockSpec((1, D, tn), lambda i, j: (i, 0, j)),
            pl.BlockSpec((1, 1, tn), lambda i, j: (i, 0, j)),
        ],
        out_specs=pl.BlockSpec((1, MOD_ROWS, tn), lambda i, j: (i, 0, j)),
        out_shape=jax.ShapeDtypeStruct((DEPTH, MOD_ROWS, n), F32),
        compiler_params=_cparams("parallel", "parallel"),
        name="mod_all",
    )(cond, mod_w, mod_b.reshape(DEPTH, 1, n))


def _router_tail(x_new, g2_ref, sh2_ref, sc2_ref, wr_ref, h_ref, aff_ref):
    h = _norm_mod(x_new, g2_ref[...], sh2_ref[0], sc2_ref[0])
    h_hi, h_lo = _split_bf16(h)
    h_ref[...] = h_hi
    nt = (((1,), (1,)), ((), ()))
    by_hi = lax.dot_general(wr_ref[...], h_hi, nt, preferred_element_type=F32)
    by_lo = lax.dot_general(wr_ref[0:N_EXPERTS, :], h_lo, nt, preferred_element_type=F32)
    logits = by_hi[0:N_EXPERTS] + by_hi[N_EXPERTS:] + by_lo
    e = jnp.exp(logits - jnp.max(logits, axis=0, keepdims=True))
    aff_ref[0] = e / jnp.sum(e, axis=0, keepdims=True)


def _router_in_specs(rowfn):
    return [_row_spec(D), _mod_spec(rowfn, 3), _mod_spec(rowfn, 4), _full_spec((2 * N_EXPERTS, D))]


def _router_out(kind, tm):
    nb = kind["L"] // tm
    rows = kind["B"] * kind["L"]
    specs = [_tile_spec(tm), pl.BlockSpec((1, N_EXPERTS, tm), lambda i: (i // nb, 0, i % nb))]
    shapes = [jax.ShapeDtypeStruct((rows, D), BF16),
              jax.ShapeDtypeStruct((kind["B"], N_EXPERTS, kind["L"]), F32)]
    return specs, shapes


def _project_with_halo(xp_ref, xc_ref, xn_ref, g_ref, sh_ref, sc_ref, w_ref, b_ref, hs_ref, tm):
    g, sh, sc = g_ref[...], sh_ref[0], sc_ref[0]
    hs_ref[0:HALO, :] = _norm_mod(xp_ref[...], g, sh, sc).astype(BF16)
    hs_ref[HALO:HALO + tm, :] = _norm_mod(xc_ref[...], g, sh, sc).astype(BF16)
    hs_ref[HALO + tm:2 * HALO + tm, :] = _norm_mod(xn_ref[...], g, sh, sc).astype(BF16)
    return jnp.dot(hs_ref[...], w_ref[...], preferred_element_type=F32) + b_ref[...]


def _zero_outside_sequence(zb_ref, j, nb, tm):
    @pl.when(j == 0)
    def _():
        zb_ref[0:HALO, :] = jnp.zeros((HALO, zb_ref.shape[1]), F32)

    @pl.when(j == nb - 1)
    def _():
        zb_ref[HALO + tm:2 * HALO + tm, :] = jnp.zeros((HALO, zb_ref.shape[1]), F32)


def _conv_kernel(xp_ref, xc_ref, xn_ref, g1_ref, sh1_ref, sc1_ref, win_ref, bin_ref, wdw_ref, bdw_ref,
                 lg_ref, lb_ref, wo_ref, bo_ref, gate_ref, g2_ref, sh2_ref, sc2_ref, wr_ref,
                 o_ref, h_ref, aff_ref, hs_ref, zb_ref, ph_ref, act_ref, *, nb, tm):
    j = pl.program_id(0) % nb
    y = _project_with_halo(xp_ref, xc_ref, xn_ref, g1_ref, sh1_ref, sc1_ref, win_ref, bin_ref, hs_ref, tm)
    zb_ref[...] = y[:, :D] * jax.nn.sigmoid(y[:, D:])
    _zero_outside_sequence(zb_ref, j, nb, tm)
    span = ph_ref.shape[1]
    for p in range(SUBLANES):
        ph_ref[p] = zb_ref[p:p + span, :]
    first = HALO - CONV_PAD
    for r0 in range(0, tm, CONV_ROW_CHUNK):
        acc = jnp.broadcast_to(bdw_ref[...], (CONV_ROW_CHUNK, D))
        for k in range(CONV_WIDTH):
            a, p = divmod(first + k, SUBLANES)
            lo = r0 + a * SUBLANES
            acc = acc + ph_ref[p, lo:lo + CONV_ROW_CHUNK, :] * wdw_ref[k:k + 1, :]
        xc = acc - jnp.mean(acc, axis=-1, keepdims=True)
        yn = xc * lax.rsqrt(jnp.mean(xc * xc, axis=-1, keepdims=True) + NORM_EPS)
        yn = yn * lg_ref[...] + lb_ref[...]
        act_ref[r0:r0 + CONV_ROW_CHUNK, :] = (yn * jax.nn.sigmoid(yn)).astype(BF16)
    out = jnp.dot(act_ref[...], wo_ref[...], preferred_element_type=F32) + bo_ref[...]
    x_new = xc_ref[...] + gate_ref[0] * out
    o_ref[...] = x_new
    _router_tail(x_new, g2_ref, sh2_ref, sc2_ref, wr_ref, h_ref, aff_ref)


def _conv_mixer(kind, x, g1, mi, w_in, b_in, w_dw, b_dw, ln_g, ln_b, w_out, b_out, g2, wr):
    rows = x.shape[0]
    tm = TOKEN_TILE
    nb = kind["L"] // tm
    rowfn = kind["rowfn"](nb)
    prev_spec, next_spec = _halo_specs(rows, tm)
    r_specs, r_shapes = _router_out(kind, tm)
    return pl.pallas_call(
        functools.partial(_conv_kernel, nb=nb, tm=tm),
        grid=(rows // tm,),
        in_specs=[prev_spec, _tile_spec(tm), next_spec,
                  _row_spec(D), _mod_spec(rowfn, 0), _mod_spec(rowfn, 1),
                  _full_spec((D, 2 * D)), _row_spec(2 * D),
                  _full_spec((CONV_WIDTH, D)), _row_spec(D), _row_spec(D), _row_spec(D),
                  _full_spec((D, D)), _row_spec(D), _mod_spec(rowfn, 2)] + _router_in_specs(rowfn),
        out_specs=[_tile_spec(tm)] + r_specs,
        out_shape=[jax.ShapeDtypeStruct((rows, D), F32)] + r_shapes,
        scratch_shapes=[pltpu.VMEM((tm + 2 * HALO, D), BF16),
                        pltpu.VMEM((tm + 2 * HALO, D), F32),
                        pltpu.VMEM((SUBLANES, tm + 2 * HALO - SUBLANES, D), F32),
                        pltpu.VMEM((tm, D), BF16)],
        compiler_params=_cparams("parallel"),
        name="conv_mixer",
    )(x, x, x, g1, mi, mi, w_in, b_in, w_dw, b_dw, ln_g, ln_b, w_out, b_out, mi, g2, mi, mi, wr)


def _head_norm(seg, g):
    return seg * lax.rsqrt(jnp.mean(seg * seg, axis=-1, keepdims=True) + NORM_EPS) * g


def _proj_qkv_kernel(x_ref, g_ref, sh_ref, sc_ref, w_ref, qg_ref, kg_ref, *rest, rope):
    if rope:
        cos_ref, sin_ref, q_ref, k_ref, v_ref = rest
    else:
        q_ref, k_ref, v_ref = rest
    h = _norm_mod(x_ref[...], g_ref[...], sh_ref[0], sc_ref[0])
    y = jnp.dot(h.astype(BF16), w_ref[...], preferred_element_type=F32)
    if rope:
        cos = cos_ref[...]
        sin = sin_ref[...]
        lane = lax.broadcasted_iota(jnp.int32, (y.shape[0], HEAD_DIM), 1)
        even = (lane & 1) == 0
    for hd in range(N_HEADS + N_KV_HEADS):
        seg = y[:, hd * HEAD_DIM:(hd + 1) * HEAD_DIM]
        nrm = _head_norm(seg, qg_ref[...] if hd < N_HEADS else kg_ref[...])
        if rope:
            partner = jnp.where(even, pltpu.roll(nrm, HEAD_DIM - 1, 1), pltpu.roll(nrm, 1, 1))
            nrm = nrm * cos + partner * sin
        if hd < N_HEADS:
            q_ref[:, hd * HEAD_DIM:(hd + 1) * HEAD_DIM] = (nrm * HEAD_DIM ** -0.5).astype(BF16)
        else:
            k_ref[:, (hd - N_HEADS) * HEAD_DIM:(hd - N_HEADS + 1) * HEAD_DIM] = nrm
    v_ref[...] = y[:, NQ + NK:]


def _proj_qkv(kind, x, g, mi, w, qg, kg, cos, sin):
    rows = x.shape[0]
    tm = TOKEN_TILE
    nb = kind["L"] // tm
    rowfn = kind["rowfn"](nb)
    rope = cos is not None
    specs = [_tile_spec(tm), _row_spec(D), _mod_spec(rowfn, 0), _mod_spec(rowfn, 1),
             _full_spec((D, NQ + 2 * NK)), _row_spec(HEAD_DIM), _row_spec(HEAD_DIM)]
    args = [x, g, mi, mi, w, qg, kg]
    if rope:
        specs += [pl.BlockSpec((tm, HEAD_DIM), lambda i: (i % nb, 0))] * 2
        args += [cos, sin]
    return pl.pallas_call(
        functools.partial(_proj_qkv_kernel, rope=rope),
        grid=(rows // tm,),
        in_specs=specs,
        out_specs=[_tile_spec(tm, NQ), _tile_spec(tm, NK), _tile_spec(tm, NK)],
        out_shape=[jax.ShapeDtypeStruct((rows, NQ), BF16),
                   jax.ShapeDtypeStruct((rows, NK), F32),
                   jax.ShapeDtypeStruct((rows, NK), F32)],
        compiler_params=_cparams("parallel"),
        name="proj_qkv",
    )(*args)


def _attn_kernel(q_ref, k_ref, v_ref, wo_ref, x_ref, gate_ref, g2_ref, sh2_ref, sc2_ref, wr_ref,
                 o_ref, h_ref, aff_ref):
    outs = []
    for hd in range(N_HEADS):
        kv = hd // KV_GROUP
        qh = q_ref[:, hd * HEAD_DIM:(hd + 1) * HEAD_DIM]
        kh = k_ref[0, :, kv * HEAD_DIM:(kv + 1) * HEAD_DIM]
        vh = v_ref[0, :, kv * HEAD_DIM:(kv + 1) * HEAD_DIM]
        s = lax.dot_general(qh, kh, (((1,), (1,)), ((), ())), preferred_element_type=F32)
        p = jnp.exp(s - jnp.max(s, axis=-1, keepdims=True))
        l = jnp.sum(p, axis=-1, keepdims=True)
        o = jnp.dot(p.astype(BF16), vh, preferred_element_type=F32)
        outs.append(o / l)
    o = jnp.concatenate(outs, axis=1).astype(BF16)
    y = jnp.dot(o, wo_ref[...], preferred_element_type=F32)
    x_new = x_ref[...] + gate_ref[0] * y
    o_ref[...] = x_new
    _router_tail(x_new, g2_ref, sh2_ref, sc2_ref, wr_ref, h_ref, aff_ref)


def _attention(kind, q, k_all, v_all, w_o, x, mi, g2, wr):
    rows = x.shape[0]
    tq = TOKEN_TILE
    nb = kind["L"] // tq
    rowfn = kind["rowfn"](nb)
    s_len = k_all.shape[1]
    kv_spec = pl.BlockSpec((1, s_len, NK), lambda i: (i // nb, 0, 0))
    r_specs, r_shapes = _router_out(kind, tq)
    return pl.pallas_call(
        _attn_kernel,
        grid=(rows // tq,),
        in_specs=[_tile_spec(tq), kv_spec, kv_spec, _full_spec((D, D)), _tile_spec(tq),
                  _mod_spec(rowfn, 2)] + _router_in_specs(rowfn),
        out_specs=[_tile_spec(tq)] + r_specs,
        out_shape=[jax.ShapeDtypeStruct((rows, D), F32)] + r_shapes,
        compiler_params=_cparams("parallel"),
        name="attention",
    )(q, k_all, v_all, w_o, x, mi, g2, mi, mi, wr)


def _hy_pre_kernel(xp_ref, xc_ref, xn_ref, g1_ref, sh1_ref, sc1_ref, win_ref, bin_ref, w_ref, b_ref,
                   u_ref, ub_ref, x0_ref, hs_ref, zb_ref, *, nb, tm):
    j = pl.program_id(0) % nb
    zb_ref[...] = _project_with_halo(xp_ref, xc_ref, xn_ref, g1_ref, sh1_ref, sc1_ref, win_ref, bin_ref,
                                     hs_ref, tm)
    _zero_outside_sequence(zb_ref, j, nb, tm)
    rc = CONV_ROW_CHUNK
    for r0 in range(0, tm, rc):
        parts = []
        for part in range(3):
            cs = slice(part * D, (part + 1) * D)
            lo = r0 + HALO - SHORT_WIDTH // 2
            z = b_ref[:, cs]
            for k in range(SHORT_WIDTH):
                z = z + zb_ref[lo + k:lo + k + rc, cs] * w_ref[k:k + 1, cs]
            parts.append(z)
        x0_ref[r0:r0 + rc, :] = parts[0]
        u = parts[2] * parts[1]
        u_ref[r0:r0 + rc, :] = u
        ub_ref[r0:r0 + rc, :] = u.astype(BF16)


def _hy_pre(kind, x, g1, mi, w_in, b_in, w_short, b_short):
    rows = x.shape[0]
    tm = TOKEN_TILE
    nb = kind["L"] // tm
    rowfn = kind["rowfn"](nb)
    prev_spec, next_spec = _halo_specs(rows, tm)
    return pl.pallas_call(
        functools.partial(_hy_pre_kernel, nb=nb, tm=tm),
        grid=(rows // tm,),
        in_specs=[prev_spec, _tile_spec(tm), next_spec,
                  _row_spec(D), _mod_spec(rowfn, 0), _mod_spec(rowfn, 1),
                  _full_spec((D, 3 * D)), _row_spec(3 * D),
                  _full_spec((SHORT_WIDTH, 3 * D)), _row_spec(3 * D)],
        out_specs=[_tile_spec(tm)] * 3,
        out_shape=[jax.ShapeDtypeStruct((rows, D), F32), jax.ShapeDtypeStruct((rows, D), BF16),
                   jax.ShapeDtypeStruct((rows, D), F32)],
        scratch_shapes=[pltpu.VMEM((tm + 2 * HALO, D), BF16), pltpu.VMEM((tm + 2 * HALO, 3 * D), F32)],
        compiler_params=_cparams("parallel"),
        name="hy_pre",
    )(x, x, x, g1, mi, mi, w_in, b_in, w_short, b_short)


def _hy_filter_kernel(z_ref, w1_ref, b1_ref, f1_ref, w2_ref, b2_ref, f2_ref, w3f_ref, w3b_ref,
                      dec_ref, ha_hi_ref, ha_lo_ref, hb_hi_ref, hb_lo_ref, nyq_ref, hid_ref):
    @pl.when(pl.program_id(0) == 0)
    def _():
        h1 = jnp.sin(f1_ref[...] * (jnp.dot(z_ref[...], w1_ref[...], preferred_element_type=F32,
                                            precision=HIGHEST) + b1_ref[...]))
        hid_ref[...] = jnp.sin(f2_ref[...] * (jnp.dot(h1, w2_ref[...], preferred_element_type=F32,
                                                      precision=HIGHEST) + b2_ref[...]))

    f = hid_ref[...]
    dec = dec_ref[...]
    hf = jnp.dot(f, w3f_ref[...], preferred_element_type=F32, precision=HIGHEST) * dec
    hb = jnp.dot(f, w3b_ref[...], preferred_element_type=F32, precision=HIGHEST) * dec
    row = lax.broadcasted_iota(jnp.int32, hf.shape, 0)
    hb = jnp.where(row == 0, 0.0, hb)
    ha = hf + hb
    ha_hi_ref[...], ha_lo_ref[...] = _split_bf16(ha)
    hb_hi_ref[...], hb_lo_ref[...] = _split_bf16(hf - hb)
    sign = jnp.where((row & 1) == 0, 1.0, -1.0)
    nyq_ref[...] = jnp.sum(ha * sign, axis=0, keepdims=True)


def _hy_filter(L, zfeat, w1, b1, fr1, w2, b2, fr2, w3, decay):
    tc = 256
    nct = D // tc
    hp = w1.shape[1]
    c_spec = pl.BlockSpec((L, tc), lambda c: (0, c))
    return pl.pallas_call(
        _hy_filter_kernel,
        grid=(nct,),
        in_specs=[
            _full_spec(zfeat.shape), _full_spec(w1.shape), _row_spec(hp), _row_spec(hp),
            _full_spec(w2.shape), _row_spec(hp), _row_spec(hp),
            pl.BlockSpec((hp, tc), lambda c: (0, c)),
            pl.BlockSpec((hp, tc), lambda c: (0, nct + c)),
            c_spec,
        ],
        out_specs=[c_spec] * 4 + [pl.BlockSpec((1, tc), lambda c: (0, c))],
        out_shape=[jax.ShapeDtypeStruct((L, D), BF16)] * 4 + [jax.ShapeDtypeStruct((1, D), F32)],
        scratch_shapes=[pltpu.VMEM((L, hp), F32)],
        compiler_params=_cparams("arbitrary"),
        name="hy_filter",
    )(zfeat, w1, b1, fr1, w2, b2, fr2, w3, w3, decay)


def _hy_spectrum_kernel(fc_hi_ref, fc_lo_ref, fs_hi_ref, fs_lo_ref, ha_hi_ref, ha_lo_ref, hb_hi_ref, hb_lo_ref,
                        nyq_ref, kra_ref, krb_ref, ki_ref, *, L, ft):
    kr = _dot3(fc_hi_ref[...], fc_lo_ref[...], ha_hi_ref[...], ha_lo_ref[...])
    ki = _dot3(fs_hi_ref[...], fs_lo_ref[...], hb_hi_ref[...], hb_lo_ref[...])
    row = lax.broadcasted_iota(jnp.int32, kr.shape, 0) + pl.program_id(1) * ft
    scale = jnp.where(row == 0, 0.5 / L, 1.0 / L)
    kr = kr * scale
    kra_ref[...] = kr
    krb_ref[...] = jnp.where(row == 0, nyq_ref[...] * (0.5 / L), kr)
    ki_ref[...] = ki * scale


def _hy_spectrum(L, fc_split, fs_split, taps, nyq):
    ft = min(L, 256)
    tc = 512
    mat_spec = pl.BlockSpec((ft, L), lambda c, k: (k, 0))
    h_spec = pl.BlockSpec((L, tc), lambda c, k: (0, c))
    o_spec = pl.BlockSpec((ft, tc), lambda c, k: (k, c))
    return pl.pallas_call(
        functools.partial(_hy_spectrum_kernel, L=L, ft=ft),
        grid=(D // tc, L // ft),
        in_specs=[mat_spec] * 4 + [h_spec] * 4 + [pl.BlockSpec((1, tc), lambda c, k: (0, c))],
        out_specs=[o_spec, o_spec, o_spec],
        out_shape=[jax.ShapeDtypeStruct((L, D), F32)] * 3,
        compiler_params=_cparams("parallel", "parallel"),
        name="hy_spectrum",
    )(*fc_split, *fs_split, *taps, nyq)


def _hy_longconv_kernel(ub_ref, fc_ref, fs_ref, gc_ref, gs_ref, kra_ref, krb_ref, ki_ref, o_ref, acc_ref):
    k = pl.program_id(2)

    @pl.when(k == 0)
    def _():
        acc_ref[...] = jnp.zeros_like(acc_ref)

    ub = ub_ref[...]
    ur = jnp.dot(fc_ref[...], ub, preferred_element_type=F32)
    ui = jnp.dot(fs_ref[...], ub, preferred_element_type=F32)
    kra = kra_ref[...]
    krb = krb_ref[...]
    ki = ki_ref[...]
    yr = (ur * kra - ui * ki).astype(BF16)
    yi = (ur * ki + ui * krb).astype(BF16)
    acc_ref[...] += jnp.dot(gc_ref[...], yr, preferred_element_type=F32)
    acc_ref[...] += jnp.dot(gs_ref[...], yi, preferred_element_type=F32)

    @pl.when(k == pl.num_programs(2) - 1)
    def _():
        o_ref[...] = acc_ref[...]


def _hy_longconv(kind, ub, mats, spec):
    B, L = kind["B"], kind["L"]
    fc, fs, fst = mats
    kra, krb, ki = spec
    ft = min(L, 512)
    tc = 1024 if L <= 256 else 512
    d_spec = pl.BlockSpec((L, tc), lambda b, c, k: (b, c))
    fwd_spec = pl.BlockSpec((ft, L), lambda b, c, k: (k, 0))
    inv_spec = pl.BlockSpec((L, ft), lambda b, c, k: (0, k))
    k_spec = pl.BlockSpec((ft, tc), lambda b, c, k: (k, c))
    return pl.pallas_call(
        _hy_longconv_kernel,
        grid=(B, D // tc, L // ft),
        in_specs=[d_spec, fwd_spec, fwd_spec, inv_spec, inv_spec, k_spec, k_spec, k_spec],
        out_specs=d_spec,
        out_shape=jax.ShapeDtypeStruct((B * L, D), F32),
        scratch_shapes=[pltpu.VMEM((L, tc), F32)],
        compiler_params=_cparams("parallel", "parallel", "arbitrary"),
        name="hy_longconv",
    )(ub, fc, fs, fc, fst, kra, krb, ki)


def _hy_out_kernel(y_ref, u_ref, x0_ref, skip_ref, w_ref, b_ref, x_ref, gate_ref,
                   g2_ref, sh2_ref, sc2_ref, wr_ref, o_ref, h_ref, aff_ref):
    a = ((y_ref[...] + u_ref[...] * skip_ref[...]) * x0_ref[...]).astype(BF16)
    y = jnp.dot(a, w_ref[...], preferred_element_type=F32) + b_ref[...]
    x_new = x_ref[...] + gate_ref[0] * y
    o_ref[...] = x_new
    _router_tail(x_new, g2_ref, sh2_ref, sc2_ref, wr_ref, h_ref, aff_ref)


def _hy_out(kind, y, u, x0, skip, w, b, x, mi, g2, wr):
    rows = x.shape[0]
    tm = TOKEN_TILE
    rowfn = kind["rowfn"](kind["L"] // tm)
    t_spec = _tile_spec(tm)
    r_specs, r_shapes = _router_out(kind, tm)
    return pl.pallas_call(
        _hy_out_kernel,
        grid=(rows // tm,),
        in_specs=[t_spec, t_spec, t_spec, _row_spec(D), _full_spec((D, D)), _row_spec(D), t_spec,
                  _mod_spec(rowfn, 2)] + _router_in_specs(rowfn),
        out_specs=[t_spec] + r_specs,
        out_shape=[jax.ShapeDtypeStruct((rows, D), F32)] + r_shapes,
        compiler_params=_cparams("parallel"),
        name="hy_out",
    )(y, u, x0, skip, w, b, x, mi, g2, mi, mi, wr)


def _lane_cumsum(mask_f32, tri):
    rows, n = mask_f32.shape
    run = jnp.zeros((rows, 1), F32)
    pieces = []
    for c0 in range(0, n, CUMSUM_CHUNK):
        chunk = mask_f32[:, c0:c0 + CUMSUM_CHUNK]
        pieces.append(jnp.dot(chunk.astype(BF16), tri, preferred_element_type=F32) + run)
        run = run + jnp.sum(chunk, axis=1, keepdims=True)
    return jnp.concatenate(pieces, axis=1) if len(pieces) > 1 else pieces[0]


def _route_kernel(aff_ref, slot_ref, *, cap):
    bits = pltpu.bitcast(aff_ref[...], jnp.int32)
    rows = bits.shape[0]
    thr = jnp.zeros((rows, 1), jnp.int32)
    capf = float(cap)
    for bit in range(30, -1, -1):
        cand = thr | (1 << bit)
        cnt = jnp.sum(jnp.where(bits >= cand, 1.0, 0.0), axis=1, keepdims=True)
        thr = jnp.where(cnt >= capf, cand, thr)
    r_i = lax.broadcasted_iota(jnp.int32, (CUMSUM_CHUNK, CUMSUM_CHUNK), 0)
    c_i = lax.broadcasted_iota(jnp.int32, (CUMSUM_CHUNK, CUMSUM_CHUNK), 1)
    tri = jnp.where(r_i <= c_i, 1.0, 0.0).astype(BF16)
    gt = jnp.where(bits > thr, 1.0, 0.0)
    eq = jnp.where(bits == thr, 1.0, 0.0)
    need = capf - jnp.sum(gt, axis=1, keepdims=True)
    eq_rank = _lane_cumsum(eq, tri)
    sel = gt + eq * jnp.where(eq_rank <= need, 1.0, 0.0)
    pos = _lane_cumsum(sel, tri)
    slot_ref[...] = jnp.where(sel > 0.5, pos - 1.0, -1.0).astype(jnp.int32)


def _route(aff2d, cap):
    return pl.pallas_call(
        functools.partial(_route_kernel, cap=cap),
        out_shape=jax.ShapeDtypeStruct(aff2d.shape, jnp.int32),
        compiler_params=pltpu.CompilerParams(vmem_limit_bytes=VMEM_LIMIT_BYTES),
        name="route",
    )(aff2d)


def _dispatch_masks(slot, cap):
    eg, n = slot.shape
    iota = lax.broadcasted_iota(jnp.int32, (cap, n), 0)
    return [iota == slot[e:e + 1, :] for e in range(eg)]


def _onehot(masks):
    return jnp.concatenate([jnp.where(m, 1.0, 0.0) for m in masks], axis=0).astype(BF16)


def _gather_kernel(slot_ref, aff_ref, h_ref, xe_ref, g_ref, *, cap, eg, bt, seq):
    for b in range(bt):
        masks = _dispatch_masks(slot_ref[b], cap)
        aff = aff_ref[b]
        xe = jnp.dot(_onehot(masks), h_ref[b * seq:(b + 1) * seq, :], preferred_element_type=F32)
        xe_ref[:, b * cap:(b + 1) * cap, :] = xe.reshape(eg, cap, D).astype(BF16)
        for e in range(eg):
            gsel = jnp.sum(jnp.where(masks[e], aff[e:e + 1, :], 0.0), axis=1, keepdims=True)
            g_ref[e, b * cap:(b + 1) * cap, :] = jnp.broadcast_to(gsel, (cap, LANES))


def _gather(kind, slot3, aff3, h2):
    B, L, cap, eg, bt = kind["B"], kind["L"], kind["cap"], kind["eg"], kind["bt"]
    ng = N_EXPERTS // eg
    assert bt == 1 or ng == 1
    r_spec = pl.BlockSpec((bt, eg, L), lambda b, e: (b * ng + e, 0, 0))
    return pl.pallas_call(
        functools.partial(_gather_kernel, cap=cap, eg=eg, bt=bt, seq=L),
        grid=(B // bt, ng),
        in_specs=[r_spec, r_spec, pl.BlockSpec((bt * L, D), lambda b, e: (b, 0))],
        out_specs=[pl.BlockSpec((eg, bt * cap, D), lambda b, e: (e, b, 0)),
                   pl.BlockSpec((eg, bt * cap, LANES), lambda b, e: (e, b, 0))],
        out_shape=[jax.ShapeDtypeStruct((N_EXPERTS, B * cap, D), BF16),
                   jax.ShapeDtypeStruct((N_EXPERTS, B * cap, LANES), F32)],
        compiler_params=_cparams("parallel", "parallel"),
        name="moe_gather",
    )(slot3, aff3, h2)


def _ffn_kernel(xa_ref, xb_ref, ga_ref, gb_ref, wg_ref, wu_ref, wd_ref, ya_ref, yb_ref, acc_ref):
    f = pl.program_id(1)

    @pl.when(f == 0)
    def _():
        acc_ref[...] = jnp.zeros_like(acc_ref)

    wg = wg_ref[0, 0].astype(BF16)
    wu = wu_ref[0, 0].astype(BF16)
    wd = wd_ref[0, 0].astype(BF16)
    na = xa_ref.shape[1]
    for x_ref, r0 in ((xa_ref, 0), (xb_ref, na)):
        x = x_ref[0]
        a = jnp.dot(x, wg, preferred_element_type=F32)
        u = jnp.dot(x, wu, preferred_element_type=F32)
        mid = (a * jax.nn.sigmoid(a) * u).astype(BF16)
        acc_ref[r0:r0 + x.shape[0], :] += jnp.dot(mid, wd, preferred_element_type=F32)

    @pl.when(f == pl.num_programs(1) - 1)
    def _():
        ya_ref[0] = (acc_ref[0:na, :] * ga_ref[0][:, 0:1]).astype(BF16)
        yb_ref[0] = (acc_ref[na:, :] * gb_ref[0][:, 0:1]).astype(BF16)


def _ffn(xe_a, xe_b, g_a, g_b, w_gate, w_up, w_down, layer):
    na, nb_ = xe_a.shape[1], xe_b.shape[1]
    tf = FFN_F_TILE
    return pl.pallas_call(
        _ffn_kernel,
        grid=(N_EXPERTS, EXPERT_FF // tf),
        in_specs=[
            pl.BlockSpec((1, na, D), lambda e, f: (e, 0, 0)),
            pl.BlockSpec((1, nb_, D), lambda e, f: (e, 0, 0)),
            pl.BlockSpec((1, na, LANES), lambda e, f: (e, 0, 0)),
            pl.BlockSpec((1, nb_, LANES), lambda e, f: (e, 0, 0)),
            pl.BlockSpec((1, 1, D, tf), lambda e, f: (layer, e, 0, f)),
            pl.BlockSpec((1, 1, D, tf), lambda e, f: (layer, e, 0, f)),
            pl.BlockSpec((1, 1, tf, D), lambda e, f: (layer, e, f, 0)),
        ],
        out_specs=[pl.BlockSpec((1, na, D), lambda e, f: (e, 0, 0)),
                   pl.BlockSpec((1, nb_, D), lambda e, f: (e, 0, 0))],
        out_shape=[jax.ShapeDtypeStruct((N_EXPERTS, na, D), BF16),
                   jax.ShapeDtypeStruct((N_EXPERTS, nb_, D), BF16)],
        scratch_shapes=[pltpu.VMEM((na + nb_, D), F32)],
        compiler_params=_cparams("parallel", "arbitrary"),
        name="moe_ffn",
    )(xe_a, xe_b, g_a, g_b, w_gate, w_up, w_down)


def _scatter_kernel(slot_ref, ye_ref, x_ref, gate_ref, fg_ref, o_ref, acc_ref, *, cap, eg, bt, lt, final):
    e = pl.program_id(2)

    @pl.when(e == 0)
    def _():
        acc_ref[...] = jnp.zeros_like(acc_ref)

    tn = (((0,), (0,)), ((), ()))
    for b in range(bt):
        onehot = _onehot(_dispatch_masks(slot_ref[b], cap))
        ye = ye_ref[:, b * cap:(b + 1) * cap, :].reshape(eg * cap, D)
        acc_ref[b * lt:(b + 1) * lt, :] += lax.dot_general(onehot, ye, tn, preferred_element_type=F32)

    @pl.when(e == pl.num_programs(2) - 1)
    def _():
        x = x_ref[...] + gate_ref[0] * acc_ref[...]
        if final:
            x = x * lax.rsqrt(jnp.mean(x * x, axis=-1, keepdims=True) + NORM_EPS) * fg_ref[...]
        o_ref[...] = x


def _scatter(kind, slot3, ye, x, mi, final_g, final):
    B, L, cap, eg, bt = kind["B"], kind["L"], kind["cap"], kind["eg"], kind["bt"]
    ng = N_EXPERTS // eg
    lt = min(L, DISPATCH_TOKENS)
    nl = L // lt
    assert bt == 1 or (ng == 1 and nl == 1)
    rowfn = lambda b, l, e: kind["rowfn"](1)(b)
    x_spec = pl.BlockSpec((bt * lt, D), lambda b, l, e: (b * nl + l, 0))
    return pl.pallas_call(
        functools.partial(_scatter_kernel, cap=cap, eg=eg, bt=bt, lt=lt, final=final),
        grid=(B // bt, nl, ng),
        in_specs=[pl.BlockSpec((bt, eg, lt), lambda b, l, e: (b * ng + e, 0, l)),
                  pl.BlockSpec((eg, bt * cap, D), lambda b, l, e: (e, b, 0)),
                  x_spec, _mod_spec(rowfn, 5), _row_spec(D)],
        out_specs=x_spec,
        out_shape=jax.ShapeDtypeStruct((B * L, D), F32),
        scratch_shapes=[pltpu.VMEM((bt * lt, D), F32)],
        compiler_params=_cparams("parallel", "parallel", "arbitrary"),
        name="moe_scatter",
    )(slot3, ye, x, mi, final_g)


def _rope_tables(L):
    n_rows = L // GRID_W
    rows = jnp.repeat(jnp.arange(n_rows), GRID_W).astype(F32)
    cols = jnp.tile(jnp.arange(GRID_W), n_rows).astype(F32)
    inv = ROPE_THETA ** (-jnp.arange(0, ROPE_AXIS_DIM, 2, dtype=F32) / ROPE_AXIS_DIM)
    ang = jnp.concatenate([rows[:, None] * inv, cols[:, None] * inv], axis=-1)
    cos = jnp.repeat(jnp.cos(ang), 2, axis=-1)
    sin = jnp.repeat(jnp.sin(ang), 2, axis=-1)
    sign = jnp.tile(jnp.array([-1.0, 1.0], F32), HEAD_DIM // 2)
    return cos, sin * sign


def _dft_tables(L):
    r = min(L, 64)
    s = jnp.arange(L, dtype=jnp.int32)

    def small(f):
        ang = ((f[:, None] * s[None, :]) % (2 * L)).astype(F32) * (math.pi / L)
        return jnp.cos(ang), jnp.sin(ang)

    c0, s0 = small(jnp.arange(r, dtype=jnp.int32))
    c1, s1 = small(jnp.arange(L // r, dtype=jnp.int32) * r)
    fc = (c1[:, None, :] * c0[None] - s1[:, None, :] * s0[None]).reshape(L, L)
    fs = -(s1[:, None, :] * c0[None] + c1[:, None, :] * s0[None]).reshape(L, L)
    sign = jnp.where(s % 2 == 0, 1.0, -1.0).astype(F32)
    return fc, fs, fs.at[0, :].set(sign), fs.at[:, 0].set(sign)


def _filter_features(L):
    t = jnp.arange(L, dtype=F32) / L
    bands = jnp.arange(1, HY_BANDS + 1, dtype=F32)
    ph = 2.0 * math.pi * t[:, None] * bands
    z = jnp.concatenate([t[:, None], jnp.sin(ph), jnp.cos(ph)], axis=-1)
    z = jnp.pad(z, ((0, 0), (0, LANES - z.shape[1])))
    rates = jnp.abs(jnp.linspace(math.log(HY_DECAY_TARGET) / HY_LONG_PCT,
                                 math.log(HY_DECAY_TARGET) / HY_SHORT_PCT, D, dtype=F32))
    return z, jnp.exp(-t[:, None] * rates)


def _pad_to(a, shape):
    return jnp.pad(a, [(0, s - d) for d, s in zip(a.shape, shape)])


def kernel(x_prompt, x_sample, cache_k, cache_v, c, c_ctx, mod_w, mod_b, norm1_g, norm2_g, cv_w_in, cv_b_in, cv_w_dw, cv_b_dw, cv_ln_g, cv_ln_b, cv_w_out, cv_b_out, at_w_qkv, at_w_o, at_q_norm, at_k_norm, hy_w_in, hy_b_in, hy_w_short, hy_b_short, hy_f_w1, hy_f_b1, hy_f_freq1, hy_f_w2, hy_f_b2, hy_f_freq2, hy_f_w3, hy_skip, hy_w_out, hy_b_out, moe_router, moe_w_gate, moe_w_up, moe_w_down, final_g):
    b_ctx, l_ctx, _ = x_prompt.shape
    b_lat, l_lat, _ = x_sample.shape
    kinds = [
        dict(B=b_ctx, L=l_ctx, lat=False, rowfn=lambda nb: (lambda i: 0)),
        dict(B=b_lat, L=l_lat, lat=True, rowfn=lambda nb: (lambda i: 1 + i // nb)),
    ]
    for kd in kinds:
        kd["cap"] = EC_CAPACITY_FACTOR * kd["L"] // N_EXPERTS
        kd["eg"] = min(N_EXPERTS, GATHER_ROWS // kd["cap"])
        kd["bt"] = 1 if kd["lat"] else max(1, DISPATCH_TOKENS // kd["L"])
    xs = [x_prompt.reshape(b_ctx * l_ctx, D), x_sample.reshape(b_lat * l_lat, D)]

    cond = jnp.concatenate([c_ctx[None, :], c, jnp.zeros((MOD_ROWS - 1 - b_lat, D), F32)], axis=0)
    mod = _mod_all(cond, mod_w, mod_b)

    row = lambda v: v.reshape(1, -1)
    final_row = row(final_g)
    new_k = new_v = None

    for i in range(DEPTH):
        mixer, j = i % N_MIXERS, i // N_MIXERS
        mi = mod[i].reshape(MOD_ROWS * 6, 1, D)
        g1 = row(norm1_g[i])
        g2 = row(norm2_g[i])
        wr = jnp.concatenate(_split_bf16(moe_router[i].T), axis=0)
        h2s, affs = [None, None], [None, None]
        if mixer == 0:
            w_in = cv_w_in[j].astype(BF16)
            w_out = cv_w_out[j].astype(BF16)
            for n, kd in enumerate(kinds):
                xs[n], h2s[n], affs[n] = _conv_mixer(
                    kd, xs[n], g1, mi, w_in, row(cv_b_in[j]), cv_w_dw[j], row(cv_b_dw[j]),
                    row(cv_ln_g[j]), row(cv_ln_b[j]), w_out, row(cv_b_out[j]), g2, wr)
        elif mixer == 1:
            w_qkv = at_w_qkv[j].astype(BF16)
            w_o = at_w_o[j].astype(BF16)
            qg, kg = row(at_q_norm[j]), row(at_k_norm[j])
            for n, kd in enumerate(kinds):
                if kd["lat"]:
                    cos, sin = _rope_tables(kd["L"])
                    q, k, v = _proj_qkv(kd, xs[n], g1, mi, w_qkv, qg, kg, cos, sin)
                    ck = cache_k[:, j].reshape(kd["B"], -1, NK)
                    cv = cache_v[:, j].reshape(kd["B"], -1, NK)
                    k_all = jnp.concatenate([ck, k.reshape(kd["B"], kd["L"], NK)], axis=1).astype(BF16)
                    v_all = jnp.concatenate([cv, v.reshape(kd["B"], kd["L"], NK)], axis=1).astype(BF16)
                else:
                    q, k, v = _proj_qkv(kd, xs[n], g1, mi, w_qkv, qg, kg, None, None)
                    new_k = k.reshape(kd["B"], 1, kd["L"], N_KV_HEADS, HEAD_DIM)
                    new_v = v.reshape(kd["B"], 1, kd["L"], N_KV_HEADS, HEAD_DIM)
                    k_all = k.reshape(kd["B"], kd["L"], NK).astype(BF16)
                    v_all = v.reshape(kd["B"], kd["L"], NK).astype(BF16)
                xs[n], h2s[n], affs[n] = _attention(kd, q, k_all, v_all, w_o, xs[n], mi, g2, wr)
        else:
            w_in = hy_w_in[j].astype(BF16)
            w_out = hy_w_out[j].astype(BF16)
            hp = LANES
            w1 = _pad_to(hy_f_w1[j], (LANES, hp))
            w2 = _pad_to(hy_f_w2[j], (hp, hp))
            w3 = _pad_to(hy_f_w3[j], (hp, 2 * D))
            b1, fr1 = _pad_to(row(hy_f_b1[j]), (1, hp)), _pad_to(row(hy_f_freq1[j]), (1, hp))
            b2, fr2 = _pad_to(row(hy_f_b2[j]), (1, hp)), _pad_to(row(hy_f_freq2[j]), (1, hp))
            for n, kd in enumerate(kinds):
                L = kd["L"]
                zfeat, decay = _filter_features(L)
                fc, fs, fs_fwd, fs_inv = _dft_tables(L)
                *taps, nyq = _hy_filter(L, zfeat, w1, b1, fr1, w2, b2, fr2, w3, decay)
                spec = _hy_spectrum(L, _split_bf16(fc), _split_bf16(fs), taps, nyq)
                u, ub, x0 = _hy_pre(kd, xs[n], g1, mi, w_in, row(hy_b_in[j]), hy_w_short[j], row(hy_b_short[j]))
                mats = (fc.astype(BF16), fs_fwd.astype(BF16), fs_inv.astype(BF16))
                y = _hy_longconv(kd, ub, mats, spec)
                xs[n], h2s[n], affs[n] = _hy_out(kd, y, u, x0, row(hy_skip[j]), w_out, row(hy_b_out[j]),
                                                  xs[n], mi, g2, wr)

        slots, xes, gs = [], [], []
        for n, kd in enumerate(kinds):
            slot = _route(affs[n].reshape(kd["B"] * N_EXPERTS, kd["L"]), kd["cap"])
            ng = N_EXPERTS // kd["eg"]
            slot3 = slot.reshape(kd["B"] * ng, kd["eg"], kd["L"])
            aff3 = affs[n].reshape(kd["B"] * ng, kd["eg"], kd["L"])
            xe, gsel = _gather(kd, slot3, aff3, h2s[n])
            slots.append(slot3)
            xes.append(xe)
            gs.append(gsel)
        yes = _ffn(xes[0], xes[1], gs[0], gs[1], moe_w_gate, moe_w_up, moe_w_down, i)
        for n, kd in enumerate(kinds):
            xs[n] = _scatter(kd, slots[n], yes[n], xs[n], mi, final_row, final=(i == DEPTH - 1))

    y_prompt = xs[0].reshape(b_ctx, l_ctx, D)
    y_sample = xs[1].reshape(b_lat, l_lat, D)
    return (y_prompt, y_sample, new_k, new_v)
```

```python
import functools
import math

import jax
import jax.numpy as jnp
from jax import lax
from jax.experimental import pallas as pl
from jax.experimental.pallas import tpu as pltpu

F32 = jnp.float32
BF16 = jnp.bfloat16
HIGHEST = lax.Precision.HIGHEST

D = 1024
DEPTH = 4
GRID_W = 64
N_MIXERS = 3
HEAD_DIM = 128
N_HEADS = 8
N_KV_HEADS = 2
KV_GROUP = N_HEADS // N_KV_HEADS
NQ = N_HEADS * HEAD_DIM
NK = N_KV_HEADS * HEAD_DIM
ROPE_AXIS_DIM = HEAD_DIM // 2
ROPE_THETA = 10000.0
CONV_WIDTH = 31
CONV_PAD = CONV_WIDTH // 2
SHORT_WIDTH = 3
HY_BANDS = 16
HY_DECAY_TARGET = 1e-2
HY_SHORT_PCT = 0.3
HY_LONG_PCT = 1.5
N_EXPERTS = 16
EXPERT_FF = 1024
EC_CAPACITY_FACTOR = 2
NORM_EPS = 1e-6

LANES = 128
SUBLANES = 8
VMEM_LIMIT_BYTES = 56 * 1024 * 1024

TOKEN_TILE = 256
HALO = 16
CONV_ROW_CHUNK = 32
MOD_ROWS = 8
GATHER_ROWS = 1024
DISPATCH_TOKENS = 1024
FFN_F_TILE = 512
CUMSUM_CHUNK = 256


def _cparams(*sem):
    return pltpu.CompilerParams(dimension_semantics=sem, vmem_limit_bytes=VMEM_LIMIT_BYTES)


def _norm_mod(x, g, sh, sc):
    y = x * lax.rsqrt(jnp.mean(x * x, axis=-1, keepdims=True) + NORM_EPS)
    return (y * g) * (1.0 + sc) + sh


def _split_bf16(a):
    hi = a.astype(BF16)
    return hi, (a - hi.astype(F32)).astype(BF16)


def _dot3(a_hi, a_lo, b_hi, b_lo):
    return (jnp.dot(a_hi, b_hi, preferred_element_type=F32) + jnp.dot(a_lo, b_hi, preferred_element_type=F32)
            + jnp.dot(a_hi, b_lo, preferred_element_type=F32))


def _mod_spec(rowfn, j):
    return pl.BlockSpec((1, 1, D), lambda *idx: (rowfn(*idx) * 6 + j, 0, 0))


def _row_spec(n):
    return pl.BlockSpec((1, n), lambda *idx: (0, 0))


def _full_spec(shape):
    nd = len(shape)
    return pl.BlockSpec(shape, lambda *idx: (0,) * nd)


def _tile_spec(tm, n=D):
    return pl.BlockSpec((tm, n), lambda i: (i, 0))


def _halo_specs(rows, tm, n=D):
    hb = tm // HALO
    last = rows // HALO - 1
    return (pl.BlockSpec((HALO, n), lambda i: (jnp.maximum(i * hb - 1, 0), 0)),
            pl.BlockSpec((HALO, n), lambda i: (jnp.minimum((i + 1) * hb, last), 0)))


def _mod_kernel(c_ref, w_ref, b_ref, o_ref):
    cv = c_ref[...]
    s_hi, s_lo = _split_bf16(cv * jax.nn.sigmoid(cv))
    w_hi, w_lo = _split_bf16(w_ref[0])
    o_ref[0] = _dot3(s_hi, s_lo, w_hi, w_lo) + b_ref[0]


def _mod_all(cond, mod_w, mod_b):
    tn = 1536
    n = 6 * D
    return pl.pallas_call(
        _mod_kernel,
        grid=(DEPTH, n // tn),
        in_specs=[
            pl.BlockSpec((MOD_ROWS, D), lambda i, j: (0, 0)),
            pl.BlockSpec((1, D, tn), lambda i, j: (i, 0, j)),
            pl.BlockSpec((1, 1, tn), lambda i, j: (i, 0, j)),
        ],
        out_specs=pl.BlockSpec((1, MOD_ROWS, tn), lambda i, j: (i, 0, j)),
        out_shape=jax.ShapeDtypeStruct((DEPTH, MOD_ROWS, n), F32),
        compiler_params=_cparams("parallel", "parallel"),
        name="mod_all",
    )(cond, mod_w, mod_b.reshape(DEPTH, 1, n))


def _router_tail(x_new, g2_ref, sh2_ref, sc2_ref, wr_ref, h_ref, aff_ref):
    h = _norm_mod(x_new, g2_ref[...], sh2_ref[0], sc2_ref[0])
    h_hi, h_lo = _split_bf16(h)
    h_ref[...] = h_hi
    nt = (((1,), (1,)), ((), ()))
    by_hi = lax.dot_general(wr_ref[...], h_hi, nt, preferred_element_type=F32)
    by_lo = lax.dot_general(wr_ref[0:N_EXPERTS, :], h_lo, nt, preferred_element_type=F32)
    logits = by_hi[0:N_EXPERTS] + by_hi[N_EXPERTS:] + by_lo
    e = jnp.exp(logits - jnp.max(logits, axis=0, keepdims=True))
    aff_ref[0] = e / jnp.sum(e, axis=0, keepdims=True)


def _router_in_specs(rowfn):
    return [_row_spec(D), _mod_spec(rowfn, 3), _mod_spec(rowfn, 4), _full_spec((2 * N_EXPERTS, D))]


def _router_out(kind, tm):
    nb = kind["L"] // tm
    rows = kind["B"] * kind["L"]
    specs = [_tile_spec(tm), pl.BlockSpec((1, N_EXPERTS, tm), lambda i: (i // nb, 0, i % nb))]
    shapes = [jax.ShapeDtypeStruct((rows, D), BF16),
              jax.ShapeDtypeStruct((kind["B"], N_EXPERTS, kind["L"]), F32)]
    return specs, shapes


def _project_with_halo(xp_ref, xc_ref, xn_ref, g_ref, sh_ref, sc_ref, w_ref, b_ref, hs_ref, tm):
    g, sh, sc = g_ref[...], sh_ref[0], sc_ref[0]
    hs_ref[0:HALO, :] = _norm_mod(xp_ref[...], g, sh, sc).astype(BF16)
    hs_ref[HALO:HALO + tm, :] = _norm_mod(xc_ref[...], g, sh, sc).astype(BF16)
    hs_ref[HALO + tm:2 * HALO + tm, :] = _norm_mod(xn_ref[...], g, sh, sc).astype(BF16)
    return jnp.dot(hs_ref[...], w_ref[...], preferred_element_type=F32) + b_ref[...]


def _zero_outside_sequence(zb_ref, j, nb, tm):
    @pl.when(j == 0)
    def _():
        zb_ref[0:HALO, :] = jnp.zeros((HALO, zb_ref.shape[1]), F32)

    @pl.when(j == nb - 1)
    def _():
        zb_ref[HALO + tm:2 * HALO + tm, :] = jnp.zeros((HALO, zb_ref.shape[1]), F32)


def _conv_kernel(xp_ref, xc_ref, xn_ref, g1_ref, sh1_ref, sc1_ref, win_ref, bin_ref,
                 fwd_hi_ref, fwd_lo_ref, inv_hi_ref, inv_lo_ref, kra_ref, krb_ref, ki_ref, bdw_ref,
                 lg_ref, lb_ref, wo_ref, bo_ref, gate_ref, g2_ref, sh2_ref, sc2_ref, wr_ref,
                 o_ref, h_ref, aff_ref, hs_ref, *, nb, tm):
    j = pl.program_id(0) % nb
    n = tm + 2 * HALO
    y = _project_with_halo(xp_ref, xc_ref, xn_ref, g1_ref, sh1_ref, sc1_ref, win_ref, bin_ref, hs_ref, tm)
    z = y[:, :D] * jax.nn.sigmoid(y[:, D:])
    row = lax.broadcasted_iota(jnp.int32, (n, 1), 0)
    first_valid = jnp.where(j > 0, 0, HALO)
    end_valid = jnp.where(j < nb - 1, n, HALO + tm)
    z = jnp.where((row >= first_valid) & (row < end_valid), z, 0.0)
    z_hi, z_lo = _split_bf16(z)
    u = _dot3(fwd_hi_ref[...], fwd_lo_ref[...], z_hi, z_lo)
    ur, ui = u[:n // 2], u[n // 2:]
    ki = ki_ref[...]
    yr = ur * kra_ref[...] - ui * ki
    yi = ur * ki + ui * krb_ref[...]
    y_hi, y_lo = _split_bf16(jnp.concatenate([yr, yi], axis=0))
    acc = _dot3(inv_hi_ref[...], inv_lo_ref[...], y_hi, y_lo) + bdw_ref[...]
    xc = acc - jnp.mean(acc, axis=-1, keepdims=True)
    yn = xc * lax.rsqrt(jnp.mean(xc * xc, axis=-1, keepdims=True) + NORM_EPS)
    yn = yn * lg_ref[...] + lb_ref[...]
    act = (yn * jax.nn.sigmoid(yn)).astype(BF16)
    out = jnp.dot(act, wo_ref[...], preferred_element_type=F32) + bo_ref[...]
    x_new = xc_ref[...] + gate_ref[0] * out
    o_ref[...] = x_new
    _router_tail(x_new, g2_ref, sh2_ref, sc2_ref, wr_ref, h_ref, aff_ref)


def _conv_spectrum_kernel(cw_ref, sw_ref, sgn_ref, w_ref, kra_ref, krb_ref, ki_ref, *, n):
    w = w_ref[...]
    kr = jnp.dot(cw_ref[...], w, preferred_element_type=F32, precision=HIGHEST)
    ki = jnp.dot(sw_ref[...], w, preferred_element_type=F32, precision=HIGHEST)
    nyq = jnp.sum(w * sgn_ref[...], axis=0, keepdims=True)
    row = lax.broadcasted_iota(jnp.int32, kr.shape, 0)
    scale = jnp.where(row == 0, 1.0 / n, 2.0 / n)
    kr = kr * scale
    kra_ref[...] = kr
    krb_ref[...] = jnp.where(row == 0, nyq * (1.0 / n), kr)
    ki_ref[...] = ki * scale


def _conv_spectrum(w_dw_padded, tables):
    cw, sw, sgn = tables
    n = 2 * cw.shape[0]
    out = jax.ShapeDtypeStruct((n // 2, D), F32)
    return pl.pallas_call(
        functools.partial(_conv_spectrum_kernel, n=n),
        out_shape=[out, out, out],
        compiler_params=pltpu.CompilerParams(vmem_limit_bytes=VMEM_LIMIT_BYTES),
        name="conv_spectrum",
    )(cw, sw, sgn, w_dw_padded)


def _conv_mixer(kind, x, g1, mi, w_in, b_in, dft, spec, b_dw, ln_g, ln_b, w_out, b_out, g2, wr):
    rows = x.shape[0]
    tm = TOKEN_TILE
    n = tm + 2 * HALO
    nb = kind["L"] // tm
    rowfn = kind["rowfn"](nb)
    prev_spec, next_spec = _halo_specs(rows, tm)
    r_specs, r_shapes = _router_out(kind, tm)
    return pl.pallas_call(
        functools.partial(_conv_kernel, nb=nb, tm=tm),
        grid=(rows // tm,),
        in_specs=[prev_spec, _tile_spec(tm), next_spec,
                  _row_spec(D), _mod_spec(rowfn, 0), _mod_spec(rowfn, 1),
                  _full_spec((D, 2 * D)), _row_spec(2 * D),
                  _full_spec((n, n)), _full_spec((n, n)), _full_spec((tm, n)), _full_spec((tm, n)),
                  _full_spec((n // 2, D)), _full_spec((n // 2, D)), _full_spec((n // 2, D)),
                  _row_spec(D), _row_spec(D), _row_spec(D),
                  _full_spec((D, D)), _row_spec(D), _mod_spec(rowfn, 2)] + _router_in_specs(rowfn),
        out_specs=[_tile_spec(tm)] + r_specs,
        out_shape=[jax.ShapeDtypeStruct((rows, D), F32)] + r_shapes,
        scratch_shapes=[pltpu.VMEM((n, D), BF16)],
        compiler_params=_cparams("parallel"),
        name="conv_mixer",
    )(x, x, x, g1, mi, mi, w_in, b_in, *dft, *spec, b_dw, ln_g, ln_b, w_out, b_out, mi, g2, mi, mi, wr)


def _head_norm(seg, g):
    return seg * lax.rsqrt(jnp.mean(seg * seg, axis=-1, keepdims=True) + NORM_EPS) * g


def _proj_qkv_kernel(x_ref, g_ref, sh_ref, sc_ref, w_ref, qg_ref, kg_ref, *rest, rope):
    if rope:
        cos_ref, sin_ref, q_ref, k_ref, v_ref = rest
    else:
        q_ref, k_ref, v_ref = rest
    h = _norm_mod(x_ref[...], g_ref[...], sh_ref[0], sc_ref[0])
    y = jnp.dot(h.astype(BF16), w_ref[...], preferred_element_type=F32)
    if rope:
        cos = cos_ref[...]
        sin = sin_ref[...]
        lane = lax.broadcasted_iota(jnp.int32, (y.shape[0], HEAD_DIM), 1)
        even = (lane & 1) == 0
    for hd in range(N_HEADS + N_KV_HEADS):
        seg = y[:, hd * HEAD_DIM:(hd + 1) * HEAD_DIM]
        nrm = _head_norm(seg, qg_ref[...] if hd < N_HEADS else kg_ref[...])
        if rope:
            partner = jnp.where(even, pltpu.roll(nrm, HEAD_DIM - 1, 1), pltpu.roll(nrm, 1, 1))
            nrm = nrm * cos + partner * sin
        if hd < N_HEADS:
            q_ref[:, hd * HEAD_DIM:(hd + 1) * HEAD_DIM] = (nrm * HEAD_DIM ** -0.5).astype(BF16)
        else:
            k_ref[:, (hd - N_HEADS) * HEAD_DIM:(hd - N_HEADS + 1) * HEAD_DIM] = nrm
    v_ref[...] = y[:, NQ + NK:]


def _proj_qkv(kind, x, g, mi, w, qg, kg, cos, sin):
    rows = x.shape[0]
    tm = TOKEN_TILE
    nb = kind["L"] // tm
    rowfn = kind["rowfn"](nb)
    rope = cos is not None
    specs = [_tile_spec(tm), _row_spec(D), _mod_spec(rowfn, 0), _mod_spec(rowfn, 1),
             _full_spec((D, NQ + 2 * NK)), _row_spec(HEAD_DIM), _row_spec(HEAD_DIM)]
    args = [x, g, mi, mi, w, qg, kg]
    if rope:
        specs += [pl.BlockSpec((tm, HEAD_DIM), lambda i: (i % nb, 0))] * 2
        args += [cos, sin]
    return pl.pallas_call(
        functools.partial(_proj_qkv_kernel, rope=rope),
        grid=(rows // tm,),
        in_specs=specs,
        out_specs=[_tile_spec(tm, NQ), _tile_spec(tm, NK), _tile_spec(tm, NK)],
        out_shape=[jax.ShapeDtypeStruct((rows, NQ), BF16),
                   jax.ShapeDtypeStruct((rows, NK), F32),
                   jax.ShapeDtypeStruct((rows, NK), F32)],
        compiler_params=_cparams("parallel"),
        name="proj_qkv",
    )(*args)


def _attn_kernel(q_ref, k_ref, v_ref, wo_ref, x_ref, gate_ref, g2_ref, sh2_ref, sc2_ref, wr_ref,
                 o_ref, h_ref, aff_ref):
    outs = []
    for hd in range(N_HEADS):
        kv = hd // KV_GROUP
        qh = q_ref[:, hd * HEAD_DIM:(hd + 1) * HEAD_DIM]
        kh = k_ref[0, :, kv * HEAD_DIM:(kv + 1) * HEAD_DIM]
        vh = v_ref[0, :, kv * HEAD_DIM:(kv + 1) * HEAD_DIM]
        s = lax.dot_general(qh, kh, (((1,), (1,)), ((), ())), preferred_element_type=F32)
        p = jnp.exp(s - jnp.max(s, axis=-1, keepdims=True))
        l = jnp.sum(p, axis=-1, keepdims=True)
        o = jnp.dot(p.astype(BF16), vh, preferred_element_type=F32)
        outs.append(o / l)
    o = jnp.concatenate(outs, axis=1).astype(BF16)
    y = jnp.dot(o, wo_ref[...], preferred_element_type=F32)
    x_new = x_ref[...] + gate_ref[0] * y
    o_ref[...] = x_new
    _router_tail(x_new, g2_ref, sh2_ref, sc2_ref, wr_ref, h_ref, aff_ref)


def _attention(kind, q, k_all, v_all, w_o, x, mi, g2, wr):
    rows = x.shape[0]
    tq = TOKEN_TILE
    nb = kind["L"] // tq
    rowfn = kind["rowfn"](nb)
    s_len = k_all.shape[1]
    kv_spec = pl.BlockSpec((1, s_len, NK), lambda i: (i // nb, 0, 0))
    r_specs, r_shapes = _router_out(kind, tq)
    return pl.pallas_call(
        _attn_kernel,
        grid=(rows // tq,),
        in_specs=[_tile_spec(tq), kv_spec, kv_spec, _full_spec((D, D)), _tile_spec(tq),
                  _mod_spec(rowfn, 2)] + _router_in_specs(rowfn),
        out_specs=[_tile_spec(tq)] + r_specs,
        out_shape=[jax.ShapeDtypeStruct((rows, D), F32)] + r_shapes,
        compiler_params=_cparams("parallel"),
        name="attention",
    )(q, k_all, v_all, w_o, x, mi, g2, mi, mi, wr)


def _hy_pre_kernel(xp_ref, xc_ref, xn_ref, g1_ref, sh1_ref, sc1_ref, win_ref, bin_ref, w_ref, b_ref,
                   u_ref, ub_ref, x0_ref, hs_ref, zb_ref, *, nb, tm):
    j = pl.program_id(0) % nb
    zb_ref[...] = _project_with_halo(xp_ref, xc_ref, xn_ref, g1_ref, sh1_ref, sc1_ref, win_ref, bin_ref,
                                     hs_ref, tm)
    _zero_outside_sequence(zb_ref, j, nb, tm)
    rc = CONV_ROW_CHUNK
    for r0 in range(0, tm, rc):
        parts = []
        for part in range(3):
            cs = slice(part * D, (part + 1) * D)
            lo = r0 + HALO - SHORT_WIDTH // 2
            z = b_ref[:, cs]
            for k in range(SHORT_WIDTH):
                z = z + zb_ref[lo + k:lo + k + rc, cs] * w_ref[k:k + 1, cs]
            parts.append(z)
        x0_ref[r0:r0 + rc, :] = parts[0]
        u = parts[2] * parts[1]
        u_ref[r0:r0 + rc, :] = u
        ub_ref[r0:r0 + rc, :] = u.astype(BF16)


def _hy_pre(kind, x, g1, mi, w_in, b_in, w_short, b_short):
    rows = x.shape[0]
    tm = TOKEN_TILE
    nb = kind["L"] // tm
    rowfn = kind["rowfn"](nb)
    prev_spec, next_spec = _halo_specs(rows, tm)
    return pl.pallas_call(
        functools.partial(_hy_pre_kernel, nb=nb, tm=tm),
        grid=(rows // tm,),
        in_specs=[prev_spec, _tile_spec(tm), next_spec,
                  _row_spec(D), _mod_spec(rowfn, 0), _mod_spec(rowfn, 1),
                  _full_spec((D, 3 * D)), _row_spec(3 * D),
                  _full_spec((SHORT_WIDTH, 3 * D)), _row_spec(3 * D)],
        out_specs=[_tile_spec(tm)] * 3,
        out_shape=[jax.ShapeDtypeStruct((rows, D), F32), jax.ShapeDtypeStruct((rows, D), BF16),
                   jax.ShapeDtypeStruct((rows, D), F32)],
        scratch_shapes=[pltpu.VMEM((tm + 2 * HALO, D), BF16), pltpu.VMEM((tm + 2 * HALO, 3 * D), F32)],
        compiler_params=_cparams("parallel"),
        name="hy_pre",
    )(x, x, x, g1, mi, mi, w_in, b_in, w_short, b_short)


def _hy_filter_kernel(z_ref, w1_ref, b1_ref, f1_ref, w2_ref, b2_ref, f2_ref, w3f_ref, w3b_ref,
                      dec_ref, ha_hi_ref, ha_lo_ref, hb_hi_ref, hb_lo_ref, nyq_ref, hid_ref):
    @pl.when(pl.program_id(0) == 0)
    def _():
        h1 = jnp.sin(f1_ref[...] * (jnp.dot(z_ref[...], w1_ref[...], preferred_element_type=F32,
                                            precision=HIGHEST) + b1_ref[...]))
        hid_ref[...] = jnp.sin(f2_ref[...] * (jnp.dot(h1, w2_ref[...], preferred_element_type=F32,
                                                      precision=HIGHEST) + b2_ref[...]))

    f = hid_ref[...]
    dec = dec_ref[...]
    hf = jnp.dot(f, w3f_ref[...], preferred_element_type=F32, precision=HIGHEST) * dec
    hb = jnp.dot(f, w3b_ref[...], preferred_element_type=F32, precision=HIGHEST) * dec
    row = lax.broadcasted_iota(jnp.int32, hf.shape, 0)
    hb = jnp.where(row == 0, 0.0, hb)
    ha = hf + hb
    ha_hi_ref[...], ha_lo_ref[...] = _split_bf16(ha)
    hb_hi_ref[...], hb_lo_ref[...] = _split_bf16(hf - hb)
    sign = jnp.where((row & 1) == 0, 1.0, -1.0)
    nyq_ref[...] = jnp.sum(ha * sign, axis=0, keepdims=True)


def _hy_filter(L, zfeat, w1, b1, fr1, w2, b2, fr2, w3, decay):
    tc = 256
    nct = D // tc
    hp = w1.shape[1]
    c_spec = pl.BlockSpec((L, tc), lambda c: (0, c))
    return pl.pallas_call(
        _hy_filter_kernel,
        grid=(nct,),
        in_specs=[
            _full_spec(zfeat.shape), _full_spec(w1.shape), _row_spec(hp), _row_spec(hp),
            _full_spec(w2.shape), _row_spec(hp), _row_spec(hp),
            pl.BlockSpec((hp, tc), lambda c: (0, c)),
            pl.BlockSpec((hp, tc), lambda c: (0, nct + c)),
            c_spec,
        ],
        out_specs=[c_spec] * 4 + [pl.BlockSpec((1, tc), lambda c: (0, c))],
        out_shape=[jax.ShapeDtypeStruct((L, D), BF16)] * 4 + [jax.ShapeDtypeStruct((1, D), F32)],
        scratch_shapes=[pltpu.VMEM((L, hp), F32)],
        compiler_params=_cparams("arbitrary"),
        name="hy_filter",
    )(zfeat, w1, b1, fr1, w2, b2, fr2, w3, w3, decay)


def _hy_spectrum_kernel(fc_hi_ref, fc_lo_ref, fs_hi_ref, fs_lo_ref, ha_hi_ref, ha_lo_ref, hb_hi_ref, hb_lo_ref,
                        nyq_ref, kra_ref, krb_ref, ki_ref, *, L, ft):
    kr = _dot3(fc_hi_ref[...], fc_lo_ref[...], ha_hi_ref[...], ha_lo_ref[...])
    ki = _dot3(fs_hi_ref[...], fs_lo_ref[...], hb_hi_ref[...], hb_lo_ref[...])
    row = lax.broadcasted_iota(jnp.int32, kr.shape, 0) + pl.program_id(1) * ft
    scale = jnp.where(row == 0, 0.5 / L, 1.0 / L)
    kr = kr * scale
    kra_ref[...] = kr
    krb_ref[...] = jnp.where(row == 0, nyq_ref[...] * (0.5 / L), kr)
    ki_ref[...] = ki * scale


def _hy_spectrum(L, fc_split, fs_split, taps, nyq):
    ft = min(L, 256)
    tc = 512
    mat_spec = pl.BlockSpec((ft, L), lambda c, k: (k, 0))
    h_spec = pl.BlockSpec((L, tc), lambda c, k: (0, c))
    o_spec = pl.BlockSpec((ft, tc), lambda c, k: (k, c))
    return pl.pallas_call(
        functools.partial(_hy_spectrum_kernel, L=L, ft=ft),
        grid=(D // tc, L // ft),
        in_specs=[mat_spec] * 4 + [h_spec] * 4 + [pl.BlockSpec((1, tc), lambda c, k: (0, c))],
        out_specs=[o_spec, o_spec, o_spec],
        out_shape=[jax.ShapeDtypeStruct((L, D), F32)] * 3,
        compiler_params=_cparams("parallel", "parallel"),
        name="hy_spectrum",
    )(*fc_split, *fs_split, *taps, nyq)


def _hy_longconv_kernel(ub_ref, fc_ref, fs_ref, gc_ref, gs_ref, kra_ref, krb_ref, ki_ref, o_ref, acc_ref):
    k = pl.program_id(2)

    @pl.when(k == 0)
    def _():
        acc_ref[...] = jnp.zeros_like(acc_ref)

    ub = ub_ref[...]
    ur = jnp.dot(fc_ref[...], ub, preferred_element_type=F32)
    ui = jnp.dot(fs_ref[...], ub, preferred_element_type=F32)
    kra = kra_ref[...]
    krb = krb_ref[...]
    ki = ki_ref[...]
    yr = (ur * kra - ui * ki).astype(BF16)
    yi = (ur * ki + ui * krb).astype(BF16)
    acc_ref[...] += jnp.dot(gc_ref[...], yr, preferred_element_type=F32)
    acc_ref[...] += jnp.dot(gs_ref[...], yi, preferred_element_type=F32)

    @pl.when(k == pl.num_programs(2) - 1)
    def _():
        o_ref[...] = acc_ref[...]


def _hy_longconv(kind, ub, mats, spec):
    B, L = kind["B"], kind["L"]
    fc, fs, fst = mats
    kra, krb, ki = spec
    ft = min(L, 512)
    tc = 1024 if L <= 256 else 512
    d_spec = pl.BlockSpec((L, tc), lambda b, c, k: (b, c))
    fwd_spec = pl.BlockSpec((ft, L), lambda b, c, k: (k, 0))
    inv_spec = pl.BlockSpec((L, ft), lambda b, c, k: (0, k))
    k_spec = pl.BlockSpec((ft, tc), lambda b, c, k: (k, c))
    return pl.pallas_call(
        _hy_longconv_kernel,
        grid=(B, D // tc, L // ft),
        in_specs=[d_spec, fwd_spec, fwd_spec, inv_spec, inv_spec, k_spec, k_spec, k_spec],
        out_specs=d_spec,
        out_shape=jax.ShapeDtypeStruct((B * L, D), F32),
        scratch_shapes=[pltpu.VMEM((L, tc), F32)],
        compiler_params=_cparams("parallel", "parallel", "arbitrary"),
        name="hy_longconv",
    )(ub, fc, fs, fc, fst, kra, krb, ki)


def _hy_out_kernel(y_ref, u_ref, x0_ref, skip_ref, w_ref, b_ref, x_ref, gate_ref,
                   g2_ref, sh2_ref, sc2_ref, wr_ref, o_ref, h_ref, aff_ref):
    a = ((y_ref[...] + u_ref[...] * skip_ref[...]) * x0_ref[...]).astype(BF16)
    y = jnp.dot(a, w_ref[...], preferred_element_type=F32) + b_ref[...]
    x_new = x_ref[...] + gate_ref[0] * y
    o_ref[...] = x_new
    _router_tail(x_new, g2_ref, sh2_ref, sc2_ref, wr_ref, h_ref, aff_ref)


def _hy_out(kind, y, u, x0, skip, w, b, x, mi, g2, wr):
    rows = x.shape[0]
    tm = TOKEN_TILE
    rowfn = kind["rowfn"](kind["L"] // tm)
    t_spec = _tile_spec(tm)
    r_specs, r_shapes = _router_out(kind, tm)
    return pl.pallas_call(
        _hy_out_kernel,
        grid=(rows // tm,),
        in_specs=[t_spec, t_spec, t_spec, _row_spec(D), _full_spec((D, D)), _row_spec(D), t_spec,
                  _mod_spec(rowfn, 2)] + _router_in_specs(rowfn),
        out_specs=[t_spec] + r_specs,
        out_shape=[jax.ShapeDtypeStruct((rows, D), F32)] + r_shapes,
        compiler_params=_cparams("parallel"),
        name="hy_out",
    )(y, u, x0, skip, w, b, x, mi, g2, mi, mi, wr)


def _lane_cumsum(mask_f32, tri):
    rows, n = mask_f32.shape
    run = jnp.zeros((rows, 1), F32)
    pieces = []
    for c0 in range(0, n, CUMSUM_CHUNK):
        chunk = mask_f32[:, c0:c0 + CUMSUM_CHUNK]
        pieces.append(jnp.dot(chunk.astype(BF16), tri, preferred_element_type=F32) + run)
        run = run + jnp.sum(chunk, axis=1, keepdims=True)
    return jnp.concatenate(pieces, axis=1) if len(pieces) > 1 else pieces[0]


def _route_kernel(aff_ref, slot_ref, *, cap):
    bits = pltpu.bitcast(aff_ref[...], jnp.int32)
    rows = bits.shape[0]
    thr = jnp.zeros((rows, 1), jnp.int32)
    capf = float(cap)
    for bit in range(30, -1, -1):
        cand = thr | (1 << bit)
        cnt = jnp.sum(jnp.where(bits >= cand, 1.0, 0.0), axis=1, keepdims=True)
        thr = jnp.where(cnt >= capf, cand, thr)
    r_i = lax.broadcasted_iota(jnp.int32, (CUMSUM_CHUNK, CUMSUM_CHUNK), 0)
    c_i = lax.broadcasted_iota(jnp.int32, (CUMSUM_CHUNK, CUMSUM_CHUNK), 1)
    tri = jnp.where(r_i <= c_i, 1.0, 0.0).astype(BF16)
    gt = jnp.where(bits > thr, 1.0, 0.0)
    eq = jnp.where(bits == thr, 1.0, 0.0)
    need = capf - jnp.sum(gt, axis=1, keepdims=True)
    eq_rank = _lane_cumsum(eq, tri)
    sel = gt + eq * jnp.where(eq_rank <= need, 1.0, 0.0)
    pos = _lane_cumsum(sel, tri)
    slot_ref[...] = jnp.where(sel > 0.5, pos - 1.0, -1.0).astype(jnp.int32)


def _route(aff2d, cap):
    return pl.pallas_call(
        functools.partial(_route_kernel, cap=cap),
        out_shape=jax.ShapeDtypeStruct(aff2d.shape, jnp.int32),
        compiler_params=pltpu.CompilerParams(vmem_limit_bytes=VMEM_LIMIT_BYTES),
        name="route",
    )(aff2d)


def _dispatch_masks(slot, cap):
    eg, n = slot.shape
    iota = lax.broadcasted_iota(jnp.int32, (cap, n), 0)
    return [iota == slot[e:e + 1, :] for e in range(eg)]


def _onehot(masks):
    return jnp.concatenate([jnp.where(m, 1.0, 0.0).astype(BF16) for m in masks], axis=0)


def _gather_kernel(slot_ref, aff_ref, h_ref, xe_ref, g_ref, *, cap, eg, bt, seq):
    for b in range(bt):
        masks = _dispatch_masks(slot_ref[b], cap)
        aff = aff_ref[b]
        xe = jnp.dot(_onehot(masks), h_ref[b * seq:(b + 1) * seq, :], preferred_element_type=F32)
        xe_ref[:, b * cap:(b + 1) * cap, :] = xe.reshape(eg, cap, D).astype(BF16)
        for e in range(eg):
            gsel = jnp.sum(jnp.where(masks[e], aff[e:e + 1, :], 0.0), axis=1, keepdims=True)
            g_ref[e, b * cap:(b + 1) * cap, :] = jnp.broadcast_to(gsel, (cap, LANES))


def _gather(kind, slot3, aff3, h2):
    B, L, cap, eg, bt = kind["B"], kind["L"], kind["cap"], kind["eg"], kind["bt"]
    ng = N_EXPERTS // eg
    assert bt == 1 or ng == 1
    r_spec = pl.BlockSpec((bt, eg, L), lambda b, e: (b * ng + e, 0, 0))
    return pl.pallas_call(
        functools.partial(_gather_kernel, cap=cap, eg=eg, bt=bt, seq=L),
        grid=(B // bt, ng),
        in_specs=[r_spec, r_spec, pl.BlockSpec((bt * L, D), lambda b, e: (b, 0))],
        out_specs=[pl.BlockSpec((eg, bt * cap, D), lambda b, e: (e, b, 0)),
                   pl.BlockSpec((eg, bt * cap, LANES), lambda b, e: (e, b, 0))],
        out_shape=[jax.ShapeDtypeStruct((N_EXPERTS, B * cap, D), BF16),
                   jax.ShapeDtypeStruct((N_EXPERTS, B * cap, LANES), F32)],
        compiler_params=_cparams("parallel", "parallel"),
        name="moe_gather",
    )(slot3, aff3, h2)


def _ffn_kernel(xa_ref, xb_ref, ga_ref, gb_ref, wg_ref, wu_ref, wd_ref, ya_ref, yb_ref, acc_ref):
    f = pl.program_id(1)

    @pl.when(f == 0)
    def _():
        acc_ref[...] = jnp.zeros_like(acc_ref)

    wg = wg_ref[0, 0].astype(BF16)
    wu = wu_ref[0, 0].astype(BF16)
    wd = wd_ref[0, 0].astype(BF16)
    na = xa_ref.shape[1]
    for x_ref, r0 in ((xa_ref, 0), (xb_ref, na)):
        x = x_ref[0]
        a = jnp.dot(x, wg, preferred_element_type=F32)
        u = jnp.dot(x, wu, preferred_element_type=F32)
        mid = (a * jax.nn.sigmoid(a) * u).astype(BF16)
        acc_ref[r0:r0 + x.shape[0], :] += jnp.dot(mid, wd, preferred_element_type=F32)

    @pl.when(f == pl.num_programs(1) - 1)
    def _():
        ya_ref[0] = (acc_ref[0:na, :] * ga_ref[0][:, 0:1]).astype(BF16)
        yb_ref[0] = (acc_ref[na:, :] * gb_ref[0][:, 0:1]).astype(BF16)


def _ffn(xe_a, xe_b, g_a, g_b, w_gate, w_up, w_down, layer):
    na, nb_ = xe_a.shape[1], xe_b.shape[1]
    tf = FFN_F_TILE
    return pl.pallas_call(
        _ffn_kernel,
        grid=(N_EXPERTS, EXPERT_FF // tf),
        in_specs=[
            pl.BlockSpec((1, na, D), lambda e, f: (e, 0, 0)),
            pl.BlockSpec((1, nb_, D), lambda e, f: (e, 0, 0)),
            pl.BlockSpec((1, na, LANES), lambda e, f: (e, 0, 0)),
            pl.BlockSpec((1, nb_, LANES), lambda e, f: (e, 0, 0)),
            pl.BlockSpec((1, 1, D, tf), lambda e, f: (layer, e, 0, f)),
            pl.BlockSpec((1, 1, D, tf), lambda e, f: (layer, e, 0, f)),
            pl.BlockSpec((1, 1, tf, D), lambda e, f: (layer, e, f, 0)),
        ],
        out_specs=[pl.BlockSpec((1, na, D), lambda e, f: (e, 0, 0)),
                   pl.BlockSpec((1, nb_, D), lambda e, f: (e, 0, 0))],
        out_shape=[jax.ShapeDtypeStruct((N_EXPERTS, na, D), BF16),
                   jax.ShapeDtypeStruct((N_EXPERTS, nb_, D), BF16)],
        scratch_shapes=[pltpu.VMEM((na + nb_, D), F32)],
        compiler_params=_cparams("parallel", "arbitrary"),
        name="moe_ffn",
    )(xe_a, xe_b, g_a, g_b, w_gate, w_up, w_down)


def _scatter_kernel(slot_ref, ye_ref, x_ref, gate_ref, fg_ref, o_ref, acc_ref, *, cap, eg, bt, lt, final):
    e = pl.program_id(2)

    @pl.when(e == 0)
    def _():
        acc_ref[...] = jnp.zeros_like(acc_ref)

    tn = (((0,), (0,)), ((), ()))
    for b in range(bt):
        onehot = _onehot(_dispatch_masks(slot_ref[b], cap))
        ye = ye_ref[:, b * cap:(b + 1) * cap, :].reshape(eg * cap, D)
        acc_ref[b * lt:(b + 1) * lt, :] += lax.dot_general(onehot, ye, tn, preferred_element_type=F32)

    @pl.when(e == pl.num_programs(2) - 1)
    def _():
        x = x_ref[...] + gate_ref[0] * acc_ref[...]
        if final:
            x = x * lax.rsqrt(jnp.mean(x * x, axis=-1, keepdims=True) + NORM_EPS) * fg_ref[...]
        o_ref[...] = x


def _scatter(kind, slot3, ye, x, mi, final_g, final):
    B, L, cap, eg, bt = kind["B"], kind["L"], kind["cap"], kind["eg"], kind["bt"]
    ng = N_EXPERTS // eg
    lt = min(L, DISPATCH_TOKENS)
    nl = L // lt
    assert bt == 1 or (ng == 1 and nl == 1)
    rowfn = lambda b, l, e: kind["rowfn"](1)(b)
    x_spec = pl.BlockSpec((bt * lt, D), lambda b, l, e: (b * nl + l, 0))
    return pl.pallas_call(
        functools.partial(_scatter_kernel, cap=cap, eg=eg, bt=bt, lt=lt, final=final),
        grid=(B // bt, nl, ng),
        in_specs=[pl.BlockSpec((bt, eg, lt), lambda b, l, e: (b * ng + e, 0, l)),
                  pl.BlockSpec((eg, bt * cap, D), lambda b, l, e: (e, b, 0)),
                  x_spec, _mod_spec(rowfn, 5), _row_spec(D)],
        out_specs=x_spec,
        out_shape=jax.ShapeDtypeStruct((B * L, D), F32),
        scratch_shapes=[pltpu.VMEM((bt * lt, D), F32)],
        compiler_params=_cparams("parallel", "parallel", "arbitrary"),
        name="moe_scatter",
    )(slot3, ye, x, mi, final_g)


def _rope_tables(L):
    n_rows = L // GRID_W
    rows = jnp.repeat(jnp.arange(n_rows), GRID_W).astype(F32)
    cols = jnp.tile(jnp.arange(GRID_W), n_rows).astype(F32)
    inv = ROPE_THETA ** (-jnp.arange(0, ROPE_AXIS_DIM, 2, dtype=F32) / ROPE_AXIS_DIM)
    ang = jnp.concatenate([rows[:, None] * inv, cols[:, None] * inv], axis=-1)
    cos = jnp.repeat(jnp.cos(ang), 2, axis=-1)
    sin = jnp.repeat(jnp.sin(ang), 2, axis=-1)
    sign = jnp.tile(jnp.array([-1.0, 1.0], F32), HEAD_DIM // 2)
    return cos, sin * sign


def _dft_tables(L):
    r = min(L, 64)
    s = jnp.arange(L, dtype=jnp.int32)

    def small(f):
        ang = ((f[:, None] * s[None, :]) % (2 * L)).astype(F32) * (math.pi / L)
        return jnp.cos(ang), jnp.sin(ang)

    c0, s0 = small(jnp.arange(r, dtype=jnp.int32))
    c1, s1 = small(jnp.arange(L // r, dtype=jnp.int32) * r)
    fc = (c1[:, None, :] * c0[None] - s1[:, None, :] * s0[None]).reshape(L, L)
    fs = -(s1[:, None, :] * c0[None] + c1[:, None, :] * s0[None]).reshape(L, L)
    sign = jnp.where(s % 2 == 0, 1.0, -1.0).astype(F32)
    return fc, fs, fs.at[0, :].set(sign), fs.at[:, 0].set(sign)


def _conv_dft_tables(tm):
    n = tm + 2 * HALO
    f = jnp.arange(n // 2, dtype=jnp.int32)
    r = jnp.arange(n, dtype=jnp.int32)

    def cs(pos):
        ang = ((f[:, None] * pos[None, :]) % n).astype(F32) * (2.0 * math.pi / n)
        return jnp.cos(ang), -jnp.sin(ang)

    fc, fs = cs(r)
    fs = fs.at[0, :].set(jnp.where(r % 2 == 0, 1.0, -1.0).astype(F32))
    fwd = jnp.concatenate([fc, fs], axis=0)
    inv = jnp.concatenate([fc[:, HALO:HALO + tm].T, fs[:, HALO:HALO + tm].T], axis=1)
    k = jnp.arange(2 * SUBLANES * ((CONV_WIDTH + 2 * SUBLANES - 1) // (2 * SUBLANES)), dtype=jnp.int32)
    lag = (CONV_PAD - k) % n
    cw, sw = cs(lag)
    live = (k < CONV_WIDTH)[None, :]
    cw, sw = jnp.where(live, cw, 0.0), jnp.where(live, sw, 0.0)
    sgn = jnp.where((CONV_PAD - k) % 2 == 0, 1.0, -1.0).astype(F32)[:, None]
    return _split_bf16(fwd) + _split_bf16(inv), (cw, sw, sgn)


def _filter_features(L):
    t = jnp.arange(L, dtype=F32) / L
    bands = jnp.arange(1, HY_BANDS + 1, dtype=F32)
    ph = 2.0 * math.pi * t[:, None] * bands
    z = jnp.concatenate([t[:, None], jnp.sin(ph), jnp.cos(ph)], axis=-1)
    z = jnp.pad(z, ((0, 0), (0, LANES - z.shape[1])))
    rates = jnp.abs(jnp.linspace(math.log(HY_DECAY_TARGET) / HY_LONG_PCT,
                                 math.log(HY_DECAY_TARGET) / HY_SHORT_PCT, D, dtype=F32))
    return z, jnp.exp(-t[:, None] * rates)


def _pad_to(a, shape):
    return jnp.pad(a, [(0, s - d) for d, s in zip(a.shape, shape)])


def kernel(x_prompt, x_sample, cache_k, cache_v, c, c_ctx, mod_w, mod_b, norm1_g, norm2_g, cv_w_in, cv_b_in, cv_w_dw, cv_b_dw, cv_ln_g, cv_ln_b, cv_w_out, cv_b_out, at_w_qkv, at_w_o, at_q_norm, at_k_norm, hy_w_in, hy_b_in, hy_w_short, hy_b_short, hy_f_w1, hy_f_b1, hy_f_freq1, hy_f_w2, hy_f_b2, hy_f_freq2, hy_f_w3, hy_skip, hy_w_out, hy_b_out, moe_router, moe_w_gate, moe_w_up, moe_w_down, final_g):
    b_ctx, l_ctx, _ = x_prompt.shape
    b_lat, l_lat, _ = x_sample.shape
    kinds = [
        dict(B=b_ctx, L=l_ctx, lat=False, rowfn=lambda nb: (lambda i: 0)),
        dict(B=b_lat, L=l_lat, lat=True, rowfn=lambda nb: (lambda i: 1 + i // nb)),
    ]
    for kd in kinds:
        kd["cap"] = EC_CAPACITY_FACTOR * kd["L"] // N_EXPERTS
        kd["eg"] = min(N_EXPERTS, GATHER_ROWS // kd["cap"])
        kd["bt"] = 1 if kd["lat"] else max(1, DISPATCH_TOKENS // kd["L"])
    xs = [x_prompt.reshape(b_ctx * l_ctx, D), x_sample.reshape(b_lat * l_lat, D)]

    cond = jnp.concatenate([c_ctx[None, :], c, jnp.zeros((MOD_ROWS - 1 - b_lat, D), F32)], axis=0)
    mod = _mod_all(cond, mod_w, mod_b)

    row = lambda v: v.reshape(1, -1)
    final_row = row(final_g)
    new_k = new_v = None

    for i in range(DEPTH):
        mixer, j = i % N_MIXERS, i // N_MIXERS
        mi = mod[i].reshape(MOD_ROWS * 6, 1, D)
        g1 = row(norm1_g[i])
        g2 = row(norm2_g[i])
        wr = jnp.concatenate(_split_bf16(moe_router[i].T), axis=0)
        h2s, affs = [None, None], [None, None]
        if mixer == 0:
            w_in = cv_w_in[j].astype(BF16)
            w_out = cv_w_out[j].astype(BF16)
            dft, tap_tables = _conv_dft_tables(TOKEN_TILE)
            spec = _conv_spectrum(_pad_to(cv_w_dw[j], (tap_tables[0].shape[1], D)), tap_tables)
            for n, kd in enumerate(kinds):
                xs[n], h2s[n], affs[n] = _conv_mixer(
                    kd, xs[n], g1, mi, w_in, row(cv_b_in[j]), dft, spec, row(cv_b_dw[j]),
                    row(cv_ln_g[j]), row(cv_ln_b[j]), w_out, row(cv_b_out[j]), g2, wr)
        elif mixer == 1:
            w_qkv = at_w_qkv[j].astype(BF16)
            w_o = at_w_o[j].astype(BF16)
            qg, kg = row(at_q_norm[j]), row(at_k_norm[j])
            for n, kd in enumerate(kinds):
                if kd["lat"]:
                    cos, sin = _rope_tables(kd["L"])
                    q, k, v = _proj_qkv(kd, xs[n], g1, mi, w_qkv, qg, kg, cos, sin)
                    ck = cache_k[:, j].reshape(kd["B"], -1, NK)
                    cv = cache_v[:, j].reshape(kd["B"], -1, NK)
                    k_all = jnp.concatenate([ck, k.reshape(kd["B"], kd["L"], NK)], axis=1).astype(BF16)
                    v_all = jnp.concatenate([cv, v.reshape(kd["B"], kd["L"], NK)], axis=1).astype(BF16)
                else:
                    q, k, v = _proj_qkv(kd, xs[n], g1, mi, w_qkv, qg, kg, None, None)
                    new_k = k.reshape(kd["B"], 1, kd["L"], N_KV_HEADS, HEAD_DIM)
                    new_v = v.reshape(kd["B"], 1, kd["L"], N_KV_HEADS, HEAD_DIM)
                    k_all = k.reshape(kd["B"], kd["L"], NK).astype(BF16)
                    v_all = v.reshape(kd["B"], kd["L"], NK).astype(BF16)
                xs[n], h2s[n], affs[n] = _attention(kd, q, k_all, v_all, w_o, xs[n], mi, g2, wr)
        else:
            w_in = hy_w_in[j].astype(BF16)
            w_out = hy_w_out[j].astype(BF16)
            hp = LANES
            w1 = _pad_to(hy_f_w1[j], (LANES, hp))
            w2 = _pad_to(hy_f_w2[j], (hp, hp))
            w3 = _pad_to(hy_f_w3[j], (hp, 2 * D))
            b1, fr1 = _pad_to(row(hy_f_b1[j]), (1, hp)), _pad_to(row(hy_f_freq1[j]), (1, hp))
            b2, fr2 = _pad_to(row(hy_f_b2[j]), (1, hp)), _pad_to(row(hy_f_freq2[j]), (1, hp))
            for n, kd in enumerate(kinds):
                L = kd["L"]
                zfeat, decay = _filter_features(L)
                fc, fs, fs_fwd, fs_inv = _dft_tables(L)
                *taps, nyq = _hy_filter(L, zfeat, w1, b1, fr1, w2, b2, fr2, w3, decay)
                spec = _hy_spectrum(L, _split_bf16(fc), _split_bf16(fs), taps, nyq)
                u, ub, x0 = _hy_pre(kd, xs[n], g1, mi, w_in, row(hy_b_in[j]), hy_w_short[j], row(hy_b_short[j]))
                mats = (fc.astype(BF16), fs_fwd.astype(BF16), fs_inv.astype(BF16))
                y = _hy_longconv(kd, ub, mats, spec)
                xs[n], h2s[n], affs[n] = _hy_out(kd, y, u, x0, row(hy_skip[j]), w_out, row(hy_b_out[j]),
                                                  xs[n], mi, g2, wr)

        slots, xes, gs = [], [], []
        for n, kd in enumerate(kinds):
            slot = _route(affs[n].reshape(kd["B"] * N_EXPERTS, kd["L"]), kd["cap"])
            ng = N_EXPERTS // kd["eg"]
            slot3 = slot.reshape(kd["B"] * ng, kd["eg"], kd["L"])
            aff3 = affs[n].reshape(kd["B"] * ng, kd["eg"], kd["L"])
            xe, gsel = _gather(kd, slot3, aff3, h2s[n])
            slots.append(slot3)
            xes.append(xe)
            gs.append(gsel)
        yes = _ffn(xes[0], xes[1], gs[0], gs[1], moe_w_gate, moe_w_up, moe_w_down, i)
        for n, kd in enumerate(kinds):
            xs[n] = _scatter(kd, slots[n], yes[n], xs[n], mi, final_row, final=(i == DEPTH - 1))

    y_prompt = xs[0].reshape(b_ctx, l_ctx, D)
    y_sample = xs[1].reshape(b_lat, l_lat, D)
    return (y_prompt, y_sample, new_k, new_v)
```

```python
import functools
import math

import jax
import jax.numpy as jnp
from jax import lax
from jax.experimental import pallas as pl
from jax.experimental.pallas import tpu as pltpu

F32 = jnp.float32
BF16 = jnp.bfloat16
HIGHEST = lax.Precision.HIGHEST

D = 1024
DEPTH = 4
GRID_W = 64
N_MIXERS = 3
HEAD_DIM = 128
N_HEADS = 8
N_KV_HEADS = 2
KV_GROUP = N_HEADS // N_KV_HEADS
NQ = N_HEADS * HEAD_DIM
NK = N_KV_HEADS * HEAD_DIM
ROPE_AXIS_DIM = HEAD_DIM // 2
ROPE_THETA = 10000.0
CONV_WIDTH = 31
CONV_PAD = CONV_WIDTH // 2
SHORT_WIDTH = 3
HY_BANDS = 16
HY_DECAY_TARGET = 1e-2
HY_SHORT_PCT = 0.3
HY_LONG_PCT = 1.5
N_EXPERTS = 16
EXPERT_FF = 1024
EC_CAPACITY_FACTOR = 2
NORM_EPS = 1e-6

LANES = 128
SUBLANES = 8
VMEM_LIMIT_BYTES = 56 * 1024 * 1024

TOKEN_TILE = 256
HALO = 16
CONV_ROW_CHUNK = 32
MOD_ROWS = 8
GATHER_ROWS = 1024
DISPATCH_TOKENS = 1024
FFN_F_TILE = 512
CUMSUM_CHUNK = 256


def _cparams(*sem):
    return pltpu.CompilerParams(dimension_semantics=sem, vmem_limit_bytes=VMEM_LIMIT_BYTES)


def _norm_mod(x, g, sh, sc):
    y = x * lax.rsqrt(jnp.mean(x * x, axis=-1, keepdims=True) + NORM_EPS)
    return (y * g) * (1.0 + sc) + sh


def _split_bf16(a):
    hi = a.astype(BF16)
    return hi, (a - hi.astype(F32)).astype(BF16)


def _dot3(a_hi, a_lo, b_hi, b_lo):
    return (jnp.dot(a_hi, b_hi, preferred_element_type=F32) + jnp.dot(a_lo, b_hi, preferred_element_type=F32)
            + jnp.dot(a_hi, b_lo, preferred_element_type=F32))


def _mod_spec(rowfn, j):
    return pl.BlockSpec((1, 1, D), lambda *idx: (rowfn(*idx) * 6 + j, 0, 0))


def _row_spec(n):
    return pl.BlockSpec((1, n), lambda *idx: (0, 0))


def _full_spec(shape):
    nd = len(shape)
    return pl.BlockSpec(shape, lambda *idx: (0,) * nd)


def _tile_spec(tm, n=D):
    return pl.BlockSpec((tm, n), lambda i: (i, 0))


def _halo_specs(rows, tm, n=D):
    hb = tm // HALO
    last = rows // HALO - 1
    return (pl.BlockSpec((HALO, n), lambda i: (jnp.maximum(i * hb - 1, 0), 0)),
            pl.BlockSpec((HALO, n), lambda i: (jnp.minimum((i + 1) * hb, last), 0)))


def _mod_kernel(c_ref, w_ref, b_ref, o_ref):
    cv = c_ref[...]
    s_hi, s_lo = _split_bf16(cv * jax.nn.sigmoid(cv))
    w_hi, w_lo = _split_bf16(w_ref[0])
    o_ref[0] = _dot3(s_hi, s_lo, w_hi, w_lo) + b_ref[0]


def _mod_all(cond, mod_w, mod_b):
    tn = 1536
    n = 6 * D
    return pl.pallas_call(
        _mod_kernel,
        grid=(DEPTH, n // tn),
        in_specs=[
            pl.BlockSpec((MOD_ROWS, D), lambda i, j: (0, 0)),
            pl.BlockSpec((1, D, tn), lambda i, j: (i, 0, j)),
            pl.BlockSpec((1, 1, tn), lambda i, j: (i, 0, j)),
        ],
        out_specs=pl.BlockSpec((1, MOD_ROWS, tn), lambda i, j: (i, 0, j)),
        out_shape=jax.ShapeDtypeStruct((DEPTH, MOD_ROWS, n), F32),
        compiler_params=_cparams("parallel", "parallel"),
        name="mod_all",
    )(cond, mod_w, mod_b.reshape(DEPTH, 1, n))


def _router_tail(x_new, g2_ref, sh2_ref, sc2_ref, wr_ref, h_ref, aff_ref):
    h = _norm_mod(x_new, g2_ref[...], sh2_ref[0], sc2_ref[0])
    h_hi, h_lo = _split_bf16(h)
    h_ref[...] = h_hi
    nt = (((1,), (1,)), ((), ()))
    by_hi = lax.dot_general(wr_ref[...], h_hi, nt, preferred_element_type=F32)
    by_lo = lax.dot_general(wr_ref[0:N_EXPERTS, :], h_lo, nt, preferred_element_type=F32)
    logits = by_hi[0:N_EXPERTS] + by_hi[N_EXPERTS:] + by_lo
    e = jnp.exp(logits - jnp.max(logits, axis=0, keepdims=True))
    aff_ref[0] = e / jnp.sum(e, axis=0, keepdims=True)


def _router_in_specs(rowfn):
    return [_row_spec(D), _mod_spec(rowfn, 3), _mod_spec(rowfn, 4), _full_spec((2 * N_EXPERTS, D))]


def _router_out(kind, tm):
    nb = kind["L"] // tm
    rows = kind["B"] * kind["L"]
    specs = [_tile_spec(tm), pl.BlockSpec((1, N_EXPERTS, tm), lambda i: (i // nb, 0, i % nb))]
    shapes = [jax.ShapeDtypeStruct((rows, D), BF16),
              jax.ShapeDtypeStruct((kind["B"], N_EXPERTS, kind["L"]), F32)]
    return specs, shapes


def _project_with_halo(xp_ref, xc_ref, xn_ref, g_ref, sh_ref, sc_ref, w_ref, b_ref, hs_ref, tm):
    g, sh, sc = g_ref[...], sh_ref[0], sc_ref[0]
    hs_ref[0:HALO, :] = _norm_mod(xp_ref[...], g, sh, sc).astype(BF16)
    hs_ref[HALO:HALO + tm, :] = _norm_mod(xc_ref[...], g, sh, sc).astype(BF16)
    hs_ref[HALO + tm:2 * HALO + tm, :] = _norm_mod(xn_ref[...], g, sh, sc).astype(BF16)
    return jnp.dot(hs_ref[...], w_ref[...], preferred_element_type=F32) + b_ref[...]


def _zero_outside_sequence(zb_ref, j, nb, tm):
    @pl.when(j == 0)
    def _():
        zb_ref[0:HALO, :] = jnp.zeros((HALO, zb_ref.shape[1]), F32)

    @pl.when(j == nb - 1)
    def _():
        zb_ref[HALO + tm:2 * HALO + tm, :] = jnp.zeros((HALO, zb_ref.shape[1]), F32)


def _conv_kernel(xp_ref, xc_ref, xn_ref, g1_ref, sh1_ref, sc1_ref, win_ref, bin_ref,
                 fwd_hi_ref, fwd_lo_ref, inv_hi_ref, inv_lo_ref, kra_ref, krb_ref, ki_ref, bdw_ref,
                 lg_ref, lb_ref, wo_ref, bo_ref, gate_ref, g2_ref, sh2_ref, sc2_ref, wr_ref,
                 o_ref, h_ref, aff_ref, hs_ref, *, nb, tm):
    j = pl.program_id(0) % nb
    n = tm + 2 * HALO
    y = _project_with_halo(xp_ref, xc_ref, xn_ref, g1_ref, sh1_ref, sc1_ref, win_ref, bin_ref, hs_ref, tm)
    z = y[:, :D] * jax.nn.sigmoid(y[:, D:])
    row = lax.broadcasted_iota(jnp.int32, (n, 1), 0)
    first_valid = jnp.where(j > 0, 0, HALO)
    end_valid = jnp.where(j < nb - 1, n, HALO + tm)
    z = jnp.where((row >= first_valid) & (row < end_valid), z, 0.0)
    zb = z.astype(BF16)
    u = (jnp.dot(fwd_hi_ref[...], zb, preferred_element_type=F32)
         + jnp.dot(fwd_lo_ref[...], zb, preferred_element_type=F32))
    ur, ui = u[:n // 2], u[n // 2:]
    ki = ki_ref[...]
    yr = ur * kra_ref[...] - ui * ki
    yi = ur * ki + ui * krb_ref[...]
    yb = jnp.concatenate([yr, yi], axis=0).astype(BF16)
    acc = (jnp.dot(inv_hi_ref[...], yb, preferred_element_type=F32)
           + jnp.dot(inv_lo_ref[...], yb, preferred_element_type=F32) + bdw_ref[...])
    xc = acc - jnp.mean(acc, axis=-1, keepdims=True)
    yn = xc * lax.rsqrt(jnp.mean(xc * xc, axis=-1, keepdims=True) + NORM_EPS)
    yn = yn * lg_ref[...] + lb_ref[...]
    act = (yn * jax.nn.sigmoid(yn)).astype(BF16)
    out = jnp.dot(act, wo_ref[...], preferred_element_type=F32) + bo_ref[...]
    x_new = xc_ref[...] + gate_ref[0] * out
    o_ref[...] = x_new
    _router_tail(x_new, g2_ref, sh2_ref, sc2_ref, wr_ref, h_ref, aff_ref)


def _conv_spectrum_kernel(cw_ref, sw_ref, sgn_ref, w_ref, kra_ref, krb_ref, ki_ref, *, n):
    w = w_ref[...]
    kr = jnp.dot(cw_ref[...], w, preferred_element_type=F32, precision=HIGHEST)
    ki = jnp.dot(sw_ref[...], w, preferred_element_type=F32, precision=HIGHEST)
    nyq = jnp.sum(w * sgn_ref[...], axis=0, keepdims=True)
    row = lax.broadcasted_iota(jnp.int32, kr.shape, 0)
    scale = jnp.where(row == 0, 1.0 / n, 2.0 / n)
    kr = kr * scale
    kra_ref[...] = kr
    krb_ref[...] = jnp.where(row == 0, nyq * (1.0 / n), kr)
    ki_ref[...] = ki * scale


def _conv_spectrum(w_dw_padded, tables):
    cw, sw, sgn = tables
    n = 2 * cw.shape[0]
    out = jax.ShapeDtypeStruct((n // 2, D), F32)
    return pl.pallas_call(
        functools.partial(_conv_spectrum_kernel, n=n),
        out_shape=[out, out, out],
        compiler_params=pltpu.CompilerParams(vmem_limit_bytes=VMEM_LIMIT_BYTES),
        name="conv_spectrum",
    )(cw, sw, sgn, w_dw_padded)


def _conv_mixer(kind, x, g1, mi, w_in, b_in, dft, spec, b_dw, ln_g, ln_b, w_out, b_out, g2, wr):
    rows = x.shape[0]
    tm = TOKEN_TILE
    n = tm + 2 * HALO
    nb = kind["L"] // tm
    rowfn = kind["rowfn"](nb)
    prev_spec, next_spec = _halo_specs(rows, tm)
    r_specs, r_shapes = _router_out(kind, tm)
    return pl.pallas_call(
        functools.partial(_conv_kernel, nb=nb, tm=tm),
        grid=(rows // tm,),
        in_specs=[prev_spec, _tile_spec(tm), next_spec,
                  _row_spec(D), _mod_spec(rowfn, 0), _mod_spec(rowfn, 1),
                  _full_spec((D, 2 * D)), _row_spec(2 * D),
                  _full_spec((n, n)), _full_spec((n, n)), _full_spec((tm, n)), _full_spec((tm, n)),
                  _full_spec((n // 2, D)), _full_spec((n // 2, D)), _full_spec((n // 2, D)),
                  _row_spec(D), _row_spec(D), _row_spec(D),
                  _full_spec((D, D)), _row_spec(D), _mod_spec(rowfn, 2)] + _router_in_specs(rowfn),
        out_specs=[_tile_spec(tm)] + r_specs,
        out_shape=[jax.ShapeDtypeStruct((rows, D), F32)] + r_shapes,
        scratch_shapes=[pltpu.VMEM((n, D), BF16)],
        compiler_params=_cparams("parallel"),
        name="conv_mixer",
    )(x, x, x, g1, mi, mi, w_in, b_in, *dft, *spec, b_dw, ln_g, ln_b, w_out, b_out, mi, g2, mi, mi, wr)


def _head_norm(seg, g):
    return seg * lax.rsqrt(jnp.mean(seg * seg, axis=-1, keepdims=True) + NORM_EPS) * g


def _proj_qkv_kernel(x_ref, g_ref, sh_ref, sc_ref, w_ref, qg_ref, kg_ref, *rest, rope):
    if rope:
        cos_ref, sin_ref, q_ref, k_ref, v_ref = rest
    else:
        q_ref, k_ref, v_ref = rest
    h = _norm_mod(x_ref[...], g_ref[...], sh_ref[0], sc_ref[0])
    y = jnp.dot(h.astype(BF16), w_ref[...], preferred_element_type=F32)
    if rope:
        cos = cos_ref[...]
        sin = sin_ref[...]
        lane = lax.broadcasted_iota(jnp.int32, (y.shape[0], HEAD_DIM), 1)
        even = (lane & 1) == 0
    for hd in range(N_HEADS + N_KV_HEADS):
        seg = y[:, hd * HEAD_DIM:(hd + 1) * HEAD_DIM]
        nrm = _head_norm(seg, qg_ref[...] if hd < N_HEADS else kg_ref[...])
        if rope:
            partner = jnp.where(even, pltpu.roll(nrm, HEAD_DIM - 1, 1), pltpu.roll(nrm, 1, 1))
            nrm = nrm * cos + partner * sin
        if hd < N_HEADS:
            q_ref[:, hd * HEAD_DIM:(hd + 1) * HEAD_DIM] = (nrm * HEAD_DIM ** -0.5).astype(BF16)
        else:
            k_ref[:, (hd - N_HEADS) * HEAD_DIM:(hd - N_HEADS + 1) * HEAD_DIM] = nrm
    v_ref[...] = y[:, NQ + NK:]


def _proj_qkv(kind, x, g, mi, w, qg, kg, cos, sin):
    rows = x.shape[0]
    tm = TOKEN_TILE
    nb = kind["L"] // tm
    rowfn = kind["rowfn"](nb)
    rope = cos is not None
    specs = [_tile_spec(tm), _row_spec(D), _mod_spec(rowfn, 0), _mod_spec(rowfn, 1),
             _full_spec((D, NQ + 2 * NK)), _row_spec(HEAD_DIM), _row_spec(HEAD_DIM)]
    args = [x, g, mi, mi, w, qg, kg]
    if rope:
        specs += [pl.BlockSpec((tm, HEAD_DIM), lambda i: (i % nb, 0))] * 2
        args += [cos, sin]
    return pl.pallas_call(
        functools.partial(_proj_qkv_kernel, rope=rope),
        grid=(rows // tm,),
        in_specs=specs,
        out_specs=[_tile_spec(tm, NQ), _tile_spec(tm, NK), _tile_spec(tm, NK)],
        out_shape=[jax.ShapeDtypeStruct((rows, NQ), BF16),
                   jax.ShapeDtypeStruct((rows, NK), F32),
                   jax.ShapeDtypeStruct((rows, NK), F32)],
        compiler_params=_cparams("parallel"),
        name="proj_qkv",
    )(*args)


def _attn_kernel(q_ref, k_ref, v_ref, wo_ref, x_ref, gate_ref, g2_ref, sh2_ref, sc2_ref, wr_ref,
                 o_ref, h_ref, aff_ref):
    outs = []
    for hd in range(N_HEADS):
        kv = hd // KV_GROUP
        qh = q_ref[:, hd * HEAD_DIM:(hd + 1) * HEAD_DIM]
        kh = k_ref[0, :, kv * HEAD_DIM:(kv + 1) * HEAD_DIM]
        vh = v_ref[0, :, kv * HEAD_DIM:(kv + 1) * HEAD_DIM]
        s = lax.dot_general(qh, kh, (((1,), (1,)), ((), ())), preferred_element_type=F32)
        p = jnp.exp(s - jnp.max(s, axis=-1, keepdims=True))
        l = jnp.sum(p, axis=-1, keepdims=True)
        o = jnp.dot(p.astype(BF16), vh, preferred_element_type=F32)
        outs.append(o / l)
    o = jnp.concatenate(outs, axis=1).astype(BF16)
    y = jnp.dot(o, wo_ref[...], preferred_element_type=F32)
    x_new = x_ref[...] + gate_ref[0] * y
    o_ref[...] = x_new
    _router_tail(x_new, g2_ref, sh2_ref, sc2_ref, wr_ref, h_ref, aff_ref)


def _attention(kind, q, k_all, v_all, w_o, x, mi, g2, wr):
    rows = x.shape[0]
    tq = TOKEN_TILE
    nb = kind["L"] // tq
    rowfn = kind["rowfn"](nb)
    s_len = k_all.shape[1]
    kv_spec = pl.BlockSpec((1, s_len, NK), lambda i: (i // nb, 0, 0))
    r_specs, r_shapes = _router_out(kind, tq)
    return pl.pallas_call(
        _attn_kernel,
        grid=(rows // tq,),
        in_specs=[_tile_spec(tq), kv_spec, kv_spec, _full_spec((D, D)), _tile_spec(tq),
                  _mod_spec(rowfn, 2)] + _router_in_specs(rowfn),
        out_specs=[_tile_spec(tq)] + r_specs,
        out_shape=[jax.ShapeDtypeStruct((rows, D), F32)] + r_shapes,
        compiler_params=_cparams("parallel"),
        name="attention",
    )(q, k_all, v_all, w_o, x, mi, g2, mi, mi, wr)


def _hy_pre_kernel(xp_ref, xc_ref, xn_ref, g1_ref, sh1_ref, sc1_ref, win_ref, bin_ref, w_ref, b_ref,
                   u_ref, ub_ref, x0_ref, hs_ref, zb_ref, *, nb, tm):
    j = pl.program_id(0) % nb
    zb_ref[...] = _project_with_halo(xp_ref, xc_ref, xn_ref, g1_ref, sh1_ref, sc1_ref, win_ref, bin_ref,
                                     hs_ref, tm)
    _zero_outside_sequence(zb_ref, j, nb, tm)
    rc = CONV_ROW_CHUNK
    for r0 in range(0, tm, rc):
        parts = []
        for part in range(3):
            cs = slice(part * D, (part + 1) * D)
            lo = r0 + HALO - SUBLANES
            nwin = rc + 2 * SUBLANES
            win = zb_ref[lo:lo + nwin, cs]
            z = b_ref[:, cs]
            for k in range(SHORT_WIDTH):
                shift = (SHORT_WIDTH // 2 - k) % nwin
                moved = win if shift == 0 else pltpu.roll(win, shift, 0)
                z = z + moved[SUBLANES:SUBLANES + rc, :] * w_ref[k:k + 1, cs]
            parts.append(z)
        x0_ref[r0:r0 + rc, :] = parts[0]
        u = parts[2] * parts[1]
        u_ref[r0:r0 + rc, :] = u
        ub_ref[r0:r0 + rc, :] = u.astype(BF16)


def _hy_pre(kind, x, g1, mi, w_in, b_in, w_short, b_short):
    rows = x.shape[0]
    tm = TOKEN_TILE
    nb = kind["L"] // tm
    rowfn = kind["rowfn"](nb)
    prev_spec, next_spec = _halo_specs(rows, tm)
    return pl.pallas_call(
        functools.partial(_hy_pre_kernel, nb=nb, tm=tm),
        grid=(rows // tm,),
        in_specs=[prev_spec, _tile_spec(tm), next_spec,
                  _row_spec(D), _mod_spec(rowfn, 0), _mod_spec(rowfn, 1),
                  _full_spec((D, 3 * D)), _row_spec(3 * D),
                  _full_spec((SHORT_WIDTH, 3 * D)), _row_spec(3 * D)],
        out_specs=[_tile_spec(tm)] * 3,
        out_shape=[jax.ShapeDtypeStruct((rows, D), F32), jax.ShapeDtypeStruct((rows, D), BF16),
                   jax.ShapeDtypeStruct((rows, D), F32)],
        scratch_shapes=[pltpu.VMEM((tm + 2 * HALO, D), BF16), pltpu.VMEM((tm + 2 * HALO, 3 * D), F32)],
        compiler_params=_cparams("parallel"),
        name="hy_pre",
    )(x, x, x, g1, mi, mi, w_in, b_in, w_short, b_short)


def _hy_filter_kernel(z_ref, w1_ref, b1_ref, f1_ref, w2_ref, b2_ref, f2_ref, w3f_ref, w3b_ref,
                      dec_ref, ha_hi_ref, ha_lo_ref, hb_hi_ref, hb_lo_ref, nyq_ref, hid_ref):
    @pl.when(pl.program_id(0) == 0)
    def _():
        h1 = jnp.sin(f1_ref[...] * (jnp.dot(z_ref[...], w1_ref[...], preferred_element_type=F32,
                                            precision=HIGHEST) + b1_ref[...]))
        hid_ref[...] = jnp.sin(f2_ref[...] * (jnp.dot(h1, w2_ref[...], preferred_element_type=F32,
                                                      precision=HIGHEST) + b2_ref[...]))

    f = hid_ref[...]
    dec = dec_ref[...]
    hf = jnp.dot(f, w3f_ref[...], preferred_element_type=F32, precision=HIGHEST) * dec
    hb = jnp.dot(f, w3b_ref[...], preferred_element_type=F32, precision=HIGHEST) * dec
    row = lax.broadcasted_iota(jnp.int32, hf.shape, 0)
    hb = jnp.where(row == 0, 0.0, hb)
    ha = hf + hb
    ha_hi_ref[...], ha_lo_ref[...] = _split_bf16(ha)
    hb_hi_ref[...], hb_lo_ref[...] = _split_bf16(hf - hb)
    sign = jnp.where((row & 1) == 0, 1.0, -1.0)
    nyq_ref[...] = jnp.sum(ha * sign, axis=0, keepdims=True)


def _hy_filter(L, zfeat, w1, b1, fr1, w2, b2, fr2, w3, decay):
    tc = 256
    nct = D // tc
    hp = w1.shape[1]
    c_spec = pl.BlockSpec((L, tc), lambda c: (0, c))
    return pl.pallas_call(
        _hy_filter_kernel,
        grid=(nct,),
        in_specs=[
            _full_spec(zfeat.shape), _full_spec(w1.shape), _row_spec(hp), _row_spec(hp),
            _full_spec(w2.shape), _row_spec(hp), _row_spec(hp),
            pl.BlockSpec((hp, tc), lambda c: (0, c)),
            pl.BlockSpec((hp, tc), lambda c: (0, nct + c)),
            c_spec,
        ],
        out_specs=[c_spec] * 4 + [pl.BlockSpec((1, tc), lambda c: (0, c))],
        out_shape=[jax.ShapeDtypeStruct((L, D), BF16)] * 4 + [jax.ShapeDtypeStruct((1, D), F32)],
        scratch_shapes=[pltpu.VMEM((L, hp), F32)],
        compiler_params=_cparams("arbitrary"),
        name="hy_filter",
    )(zfeat, w1, b1, fr1, w2, b2, fr2, w3, w3, decay)


def _hy_spectrum_kernel(fc_hi_ref, fc_lo_ref, fs_hi_ref, fs_lo_ref, ha_hi_ref, ha_lo_ref, hb_hi_ref, hb_lo_ref,
                        nyq_ref, kra_ref, krb_ref, ki_ref, *, L, ft):
    kr = _dot3(fc_hi_ref[...], fc_lo_ref[...], ha_hi_ref[...], ha_lo_ref[...])
    ki = _dot3(fs_hi_ref[...], fs_lo_ref[...], hb_hi_ref[...], hb_lo_ref[...])
    row = lax.broadcasted_iota(jnp.int32, kr.shape, 0) + pl.program_id(1) * ft
    scale = jnp.where(row == 0, 0.5 / L, 1.0 / L)
    kr = kr * scale
    kra_ref[...] = kr
    krb_ref[...] = jnp.where(row == 0, nyq_ref[...] * (0.5 / L), kr)
    ki_ref[...] = ki * scale


def _hy_spectrum(L, fc_split, fs_split, taps, nyq):
    ft = min(L, 256)
    tc = 512
    mat_spec = pl.BlockSpec((ft, L), lambda c, k: (k, 0))
    h_spec = pl.BlockSpec((L, tc), lambda c, k: (0, c))
    o_spec = pl.BlockSpec((ft, tc), lambda c, k: (k, c))
    return pl.pallas_call(
        functools.partial(_hy_spectrum_kernel, L=L, ft=ft),
        grid=(D // tc, L // ft),
        in_specs=[mat_spec] * 4 + [h_spec] * 4 + [pl.BlockSpec((1, tc), lambda c, k: (0, c))],
        out_specs=[o_spec, o_spec, o_spec],
        out_shape=[jax.ShapeDtypeStruct((L, D), F32)] * 3,
        compiler_params=_cparams("parallel", "parallel"),
        name="hy_spectrum",
    )(*fc_split, *fs_split, *taps, nyq)


def _hy_longconv_kernel(ub_ref, fc_ref, fs_ref, gc_ref, gs_ref, kra_ref, krb_ref, ki_ref, o_ref, acc_ref):
    k = pl.program_id(2)

    @pl.when(k == 0)
    def _():
        acc_ref[...] = jnp.zeros_like(acc_ref)

    ub = ub_ref[...]
    ur = jnp.dot(fc_ref[...], ub, preferred_element_type=F32)
    ui = jnp.dot(fs_ref[...], ub, preferred_element_type=F32)
    kra = kra_ref[...]
    krb = krb_ref[...]
    ki = ki_ref[...]
    yr = (ur * kra - ui * ki).astype(BF16)
    yi = (ur * ki + ui * krb).astype(BF16)
    acc_ref[...] += jnp.dot(gc_ref[...], yr, preferred_element_type=F32)
    acc_ref[...] += jnp.dot(gs_ref[...], yi, preferred_element_type=F32)

    @pl.when(k == pl.num_programs(2) - 1)
    def _():
        o_ref[...] = acc_ref[...]


def _hy_longconv(kind, ub, mats, spec):
    B, L = kind["B"], kind["L"]
    fc, fs, fst = mats
    kra, krb, ki = spec
    ft = min(L, 512)
    tc = 1024 if L <= 256 else 512
    d_spec = pl.BlockSpec((L, tc), lambda b, c, k: (b, c))
    fwd_spec = pl.BlockSpec((ft, L), lambda b, c, k: (k, 0))
    inv_spec = pl.BlockSpec((L, ft), lambda b, c, k: (0, k))
    k_spec = pl.BlockSpec((ft, tc), lambda b, c, k: (k, c))
    return pl.pallas_call(
        _hy_longconv_kernel,
        grid=(B, D // tc, L // ft),
        in_specs=[d_spec, fwd_spec, fwd_spec, inv_spec, inv_spec, k_spec, k_spec, k_spec],
        out_specs=d_spec,
        out_shape=jax.ShapeDtypeStruct((B * L, D), F32),
        scratch_shapes=[pltpu.VMEM((L, tc), F32)],
        compiler_params=_cparams("parallel", "parallel", "arbitrary"),
        name="hy_longconv",
    )(ub, fc, fs, fc, fst, kra, krb, ki)


def _hy_out_kernel(y_ref, u_ref, x0_ref, skip_ref, w_ref, b_ref, x_ref, gate_ref,
                   g2_ref, sh2_ref, sc2_ref, wr_ref, o_ref, h_ref, aff_ref):
    a = ((y_ref[...] + u_ref[...] * skip_ref[...]) * x0_ref[...]).astype(BF16)
    y = jnp.dot(a, w_ref[...], preferred_element_type=F32) + b_ref[...]
    x_new = x_ref[...] + gate_ref[0] * y
    o_ref[...] = x_new
    _router_tail(x_new, g2_ref, sh2_ref, sc2_ref, wr_ref, h_ref, aff_ref)


def _hy_out(kind, y, u, x0, skip, w, b, x, mi, g2, wr):
    rows = x.shape[0]
    tm = TOKEN_TILE
    rowfn = kind["rowfn"](kind["L"] // tm)
    t_spec = _tile_spec(tm)
    r_specs, r_shapes = _router_out(kind, tm)
    return pl.pallas_call(
        _hy_out_kernel,
        grid=(rows // tm,),
        in_specs=[t_spec, t_spec, t_spec, _row_spec(D), _full_spec((D, D)), _row_spec(D), t_spec,
                  _mod_spec(rowfn, 2)] + _router_in_specs(rowfn),
        out_specs=[t_spec] + r_specs,
        out_shape=[jax.ShapeDtypeStruct((rows, D), F32)] + r_shapes,
        compiler_params=_cparams("parallel"),
        name="hy_out",
    )(y, u, x0, skip, w, b, x, mi, g2, mi, mi, wr)


def _lane_cumsum(mask_f32, tri):
    rows, n = mask_f32.shape
    run = jnp.zeros((rows, 1), F32)
    pieces = []
    for c0 in range(0, n, CUMSUM_CHUNK):
        chunk = mask_f32[:, c0:c0 + CUMSUM_CHUNK]
        pieces.append(jnp.dot(chunk.astype(BF16), tri, preferred_element_type=F32) + run)
        run = run + jnp.sum(chunk, axis=1, keepdims=True)
    return jnp.concatenate(pieces, axis=1) if len(pieces) > 1 else pieces[0]


def _route_kernel(aff_ref, slot_ref, *, cap):
    bits = pltpu.bitcast(aff_ref[...], jnp.int32)
    rows = bits.shape[0]
    thr = jnp.zeros((rows, 1), jnp.int32)
    capf = float(cap)
    for bit in range(30, -1, -1):
        cand = thr | (1 << bit)
        cnt = jnp.sum(jnp.where(bits >= cand, 1.0, 0.0), axis=1, keepdims=True)
        thr = jnp.where(cnt >= capf, cand, thr)
    r_i = lax.broadcasted_iota(jnp.int32, (CUMSUM_CHUNK, CUMSUM_CHUNK), 0)
    c_i = lax.broadcasted_iota(jnp.int32, (CUMSUM_CHUNK, CUMSUM_CHUNK), 1)
    tri = jnp.where(r_i <= c_i, 1.0, 0.0).astype(BF16)
    gt = jnp.where(bits > thr, 1.0, 0.0)
    eq = jnp.where(bits == thr, 1.0, 0.0)
    need = capf - jnp.sum(gt, axis=1, keepdims=True)
    eq_rank = _lane_cumsum(eq, tri)
    sel = gt + eq * jnp.where(eq_rank <= need, 1.0, 0.0)
    pos = _lane_cumsum(sel, tri)
    slot_ref[...] = jnp.where(sel > 0.5, pos - 1.0, -1.0).astype(jnp.int32)


def _route(aff2d, cap):
    return pl.pallas_call(
        functools.partial(_route_kernel, cap=cap),
        out_shape=jax.ShapeDtypeStruct(aff2d.shape, jnp.int32),
        compiler_params=pltpu.CompilerParams(vmem_limit_bytes=VMEM_LIMIT_BYTES),
        name="route",
    )(aff2d)


def _dispatch_masks(slot, cap):
    eg, n = slot.shape
    iota = lax.broadcasted_iota(jnp.int32, (cap, n), 0)
    return [iota == slot[e:e + 1, :] for e in range(eg)]


def _onehot(masks):
    return jnp.concatenate([jnp.where(m, 1.0, 0.0).astype(BF16) for m in masks], axis=0)


def _gather_kernel(slot_ref, aff_ref, h_ref, xe_ref, g_ref, *, cap, eg, bt, seq):
    for b in range(bt):
        masks = _dispatch_masks(slot_ref[b], cap)
        aff = aff_ref[b]
        xe = jnp.dot(_onehot(masks), h_ref[b * seq:(b + 1) * seq, :], preferred_element_type=F32)
        xe_ref[:, b * cap:(b + 1) * cap, :] = xe.reshape(eg, cap, D).astype(BF16)
        for e in range(eg):
            gsel = jnp.sum(jnp.where(masks[e], aff[e:e + 1, :], 0.0), axis=1, keepdims=True)
            g_ref[e, b * cap:(b + 1) * cap, :] = jnp.broadcast_to(gsel, (cap, LANES))


def _gather(kind, slot3, aff3, h2):
    B, L, cap, eg, bt = kind["B"], kind["L"], kind["cap"], kind["eg"], kind["bt"]
    ng = N_EXPERTS // eg
    assert bt == 1 or ng == 1
    r_spec = pl.BlockSpec((bt, eg, L), lambda b, e: (b * ng + e, 0, 0))
    return pl.pallas_call(
        functools.partial(_gather_kernel, cap=cap, eg=eg, bt=bt, seq=L),
        grid=(B // bt, ng),
        in_specs=[r_spec, r_spec, pl.BlockSpec((bt * L, D), lambda b, e: (b, 0))],
        out_specs=[pl.BlockSpec((eg, bt * cap, D), lambda b, e: (e, b, 0)),
                   pl.BlockSpec((eg, bt * cap, LANES), lambda b, e: (e, b, 0))],
        out_shape=[jax.ShapeDtypeStruct((N_EXPERTS, B * cap, D), BF16),
                   jax.ShapeDtypeStruct((N_EXPERTS, B * cap, LANES), F32)],
        compiler_params=_cparams("parallel", "parallel"),
        name="moe_gather",
    )(slot3, aff3, h2)


def _ffn_kernel(xa_ref, xb_ref, ga_ref, gb_ref, wg_ref, wu_ref, wd_ref, ya_ref, yb_ref, acc_ref):
    f = pl.program_id(1)

    @pl.when(f == 0)
    def _():
        acc_ref[...] = jnp.zeros_like(acc_ref)

    wg = wg_ref[0, 0].astype(BF16)
    wu = wu_ref[0, 0].astype(BF16)
    wd = wd_ref[0, 0].astype(BF16)
    na = xa_ref.shape[1]
    for x_ref, r0 in ((xa_ref, 0), (xb_ref, na)):
        x = x_ref[0]
        a = jnp.dot(x, wg, preferred_element_type=F32)
        u = jnp.dot(x, wu, preferred_element_type=F32)
        mid = (a * jax.nn.sigmoid(a) * u).astype(BF16)
        acc_ref[r0:r0 + x.shape[0], :] += jnp.dot(mid, wd, preferred_element_type=F32)

    @pl.when(f == pl.num_programs(1) - 1)
    def _():
        ya_ref[0] = (acc_ref[0:na, :] * ga_ref[0][:, 0:1]).astype(BF16)
        yb_ref[0] = (acc_ref[na:, :] * gb_ref[0][:, 0:1]).astype(BF16)


def _ffn(xe_a, xe_b, g_a, g_b, w_gate, w_up, w_down, layer):
    na, nb_ = xe_a.shape[1], xe_b.shape[1]
    tf = FFN_F_TILE
    return pl.pallas_call(
        _ffn_kernel,
        grid=(N_EXPERTS, EXPERT_FF // tf),
        in_specs=[
            pl.BlockSpec((1, na, D), lambda e, f: (e, 0, 0)),
            pl.BlockSpec((1, nb_, D), lambda e, f: (e, 0, 0)),
            pl.BlockSpec((1, na, LANES), lambda e, f: (e, 0, 0)),
            pl.BlockSpec((1, nb_, LANES), lambda e, f: (e, 0, 0)),
            pl.BlockSpec((1, 1, D, tf), lambda e, f: (layer, e, 0, f)),
            pl.BlockSpec((1, 1, D, tf), lambda e, f: (layer, e, 0, f)),
            pl.BlockSpec((1, 1, tf, D), lambda e, f: (layer, e, f, 0)),
        ],
        out_specs=[pl.BlockSpec((1, na, D), lambda e, f: (e, 0, 0)),
                   pl.BlockSpec((1, nb_, D), lambda e, f: (e, 0, 0))],
        out_shape=[jax.ShapeDtypeStruct((N_EXPERTS, na, D), BF16),
                   jax.ShapeDtypeStruct((N_EXPERTS, nb_, D), BF16)],
        scratch_shapes=[pltpu.VMEM((na + nb_, D), F32)],
        compiler_params=_cparams("parallel", "arbitrary"),
        name="moe_ffn",
    )(xe_a, xe_b, g_a, g_b, w_gate, w_up, w_down)


def _scatter_kernel(slot_ref, ye_ref, x_ref, gate_ref, fg_ref, o_ref, acc_ref, *, cap, eg, bt, lt, final):
    e = pl.program_id(2)

    @pl.when(e == 0)
    def _():
        acc_ref[...] = jnp.zeros_like(acc_ref)

    tn = (((0,), (0,)), ((), ()))
    for b in range(bt):
        onehot = _onehot(_dispatch_masks(slot_ref[b], cap))
        ye = ye_ref[:, b * cap:(b + 1) * cap, :].reshape(eg * cap, D)
        acc_ref[b * lt:(b + 1) * lt, :] += lax.dot_general(onehot, ye, tn, preferred_element_type=F32)

    @pl.when(e == pl.num_programs(2) - 1)
    def _():
        x = x_ref[...] + gate_ref[0] * acc_ref[...]
        if final:
            x = x * lax.rsqrt(jnp.mean(x * x, axis=-1, keepdims=True) + NORM_EPS) * fg_ref[...]
        o_ref[...] = x


def _scatter(kind, slot3, ye, x, mi, final_g, final):
    B, L, cap, eg, bt = kind["B"], kind["L"], kind["cap"], kind["eg"], kind["bt"]
    ng = N_EXPERTS // eg
    lt = min(L, DISPATCH_TOKENS)
    nl = L // lt
    assert bt == 1 or (ng == 1 and nl == 1)
    rowfn = lambda b, l, e: kind["rowfn"](1)(b)
    x_spec = pl.BlockSpec((bt * lt, D), lambda b, l, e: (b * nl + l, 0))
    return pl.pallas_call(
        functools.partial(_scatter_kernel, cap=cap, eg=eg, bt=bt, lt=lt, final=final),
        grid=(B // bt, nl, ng),
        in_specs=[pl.BlockSpec((bt, eg, lt), lambda b, l, e: (b * ng + e, 0, l)),
                  pl.BlockSpec((eg, bt * cap, D), lambda b, l, e: (e, b, 0)),
                  x_spec, _mod_spec(rowfn, 5), _row_spec(D)],
        out_specs=x_spec,
        out_shape=jax.ShapeDtypeStruct((B * L, D), F32),
        scratch_shapes=[pltpu.VMEM((bt * lt, D), F32)],
        compiler_params=_cparams("parallel", "parallel", "arbitrary"),
        name="moe_scatter",
    )(slot3, ye, x, mi, final_g)


def _rope_tables(L):
    n_rows = L // GRID_W
    rows = jnp.repeat(jnp.arange(n_rows), GRID_W).astype(F32)
    cols = jnp.tile(jnp.arange(GRID_W), n_rows).astype(F32)
    inv = ROPE_THETA ** (-jnp.arange(0, ROPE_AXIS_DIM, 2, dtype=F32) / ROPE_AXIS_DIM)
    ang = jnp.concatenate([rows[:, None] * inv, cols[:, None] * inv], axis=-1)
    cos = jnp.repeat(jnp.cos(ang), 2, axis=-1)
    sin = jnp.repeat(jnp.sin(ang), 2, axis=-1)
    sign = jnp.tile(jnp.array([-1.0, 1.0], F32), HEAD_DIM // 2)
    return cos, sin * sign


def _dft_tables(L):
    r = min(L, 64)
    s = jnp.arange(L, dtype=jnp.int32)

    def small(f):
        ang = ((f[:, None] * s[None, :]) % (2 * L)).astype(F32) * (math.pi / L)
        return jnp.cos(ang), jnp.sin(ang)

    c0, s0 = small(jnp.arange(r, dtype=jnp.int32))
    c1, s1 = small(jnp.arange(L // r, dtype=jnp.int32) * r)
    fc = (c1[:, None, :] * c0[None] - s1[:, None, :] * s0[None]).reshape(L, L)
    fs = -(s1[:, None, :] * c0[None] + c1[:, None, :] * s0[None]).reshape(L, L)
    sign = jnp.where(s % 2 == 0, 1.0, -1.0).astype(F32)
    fs_fwd = jnp.where(s[:, None] == 0, sign[None, :], fs)
    fs_inv = jnp.where(s[None, :] == 0, sign[:, None], fs)
    return fc, fs, fs_fwd, fs_inv


def _conv_dft_tables(tm):
    n = tm + 2 * HALO
    f = jnp.arange(n // 2, dtype=jnp.int32)
    r = jnp.arange(n, dtype=jnp.int32)

    def cs(pos):
        ang = ((f[:, None] * pos[None, :]) % n).astype(F32) * (2.0 * math.pi / n)
        return jnp.cos(ang), -jnp.sin(ang)

    fc, fs = cs(r)
    fs = fs.at[0, :].set(jnp.where(r % 2 == 0, 1.0, -1.0).astype(F32))
    fwd = jnp.concatenate([fc, fs], axis=0)
    inv = jnp.concatenate([fc[:, HALO:HALO + tm].T, fs[:, HALO:HALO + tm].T], axis=1)
    k = jnp.arange(2 * SUBLANES * ((CONV_WIDTH + 2 * SUBLANES - 1) // (2 * SUBLANES)), dtype=jnp.int32)
    lag = (CONV_PAD - k) % n
    cw, sw = cs(lag)
    live = (k < CONV_WIDTH)[None, :]
    cw, sw = jnp.where(live, cw, 0.0), jnp.where(live, sw, 0.0)
    sgn = jnp.where((CONV_PAD - k) % 2 == 0, 1.0, -1.0).astype(F32)[:, None]
    return _split_bf16(fwd) + _split_bf16(inv), (cw, sw, sgn)


def _filter_features(L):
    t = jnp.arange(L, dtype=F32) / L
    bands = jnp.arange(1, HY_BANDS + 1, dtype=F32)
    ph = 2.0 * math.pi * t[:, None] * bands
    z = jnp.concatenate([t[:, None], jnp.sin(ph), jnp.cos(ph)], axis=-1)
    z = jnp.pad(z, ((0, 0), (0, LANES - z.shape[1])))
    rates = jnp.abs(jnp.linspace(math.log(HY_DECAY_TARGET) / HY_LONG_PCT,
                                 math.log(HY_DECAY_TARGET) / HY_SHORT_PCT, D, dtype=F32))
    return z, jnp.exp(-t[:, None] * rates)


def _pad_to(a, shape):
    return jnp.pad(a, [(0, s - d) for d, s in zip(a.shape, shape)])


def kernel(x_prompt, x_sample, cache_k, cache_v, c, c_ctx, mod_w, mod_b, norm1_g, norm2_g, cv_w_in, cv_b_in, cv_w_dw, cv_b_dw, cv_ln_g, cv_ln_b, cv_w_out, cv_b_out, at_w_qkv, at_w_o, at_q_norm, at_k_norm, hy_w_in, hy_b_in, hy_w_short, hy_b_short, hy_f_w1, hy_f_b1, hy_f_freq1, hy_f_w2, hy_f_b2, hy_f_freq2, hy_f_w3, hy_skip, hy_w_out, hy_b_out, moe_router, moe_w_gate, moe_w_up, moe_w_down, final_g):
    b_ctx, l_ctx, _ = x_prompt.shape
    b_lat, l_lat, _ = x_sample.shape
    kinds = [
        dict(B=b_ctx, L=l_ctx, lat=False, rowfn=lambda nb: (lambda i: 0)),
        dict(B=b_lat, L=l_lat, lat=True, rowfn=lambda nb: (lambda i: 1 + i // nb)),
    ]
    for kd in kinds:
        kd["cap"] = EC_CAPACITY_FACTOR * kd["L"] // N_EXPERTS
        kd["eg"] = min(N_EXPERTS, GATHER_ROWS // kd["cap"])
        kd["bt"] = 1 if kd["lat"] else max(1, DISPATCH_TOKENS // kd["L"])
    xs = [x_prompt.reshape(b_ctx * l_ctx, D), x_sample.reshape(b_lat * l_lat, D)]

    cond = jnp.concatenate([c_ctx[None, :], c, jnp.zeros((MOD_ROWS - 1 - b_lat, D), F32)], axis=0)
    mod = _mod_all(cond, mod_w, mod_b)

    row = lambda v: v.reshape(1, -1)
    final_row = row(final_g)
    new_k = new_v = None

    for i in range(DEPTH):
        mixer, j = i % N_MIXERS, i // N_MIXERS
        mi = mod[i].reshape(MOD_ROWS * 6, 1, D)
        g1 = row(norm1_g[i])
        g2 = row(norm2_g[i])
        wr = jnp.concatenate(_split_bf16(moe_router[i].T), axis=0)
        h2s, affs = [None, None], [None, None]
        if mixer == 0:
            w_in = cv_w_in[j].astype(BF16)
            w_out = cv_w_out[j].astype(BF16)
            dft, tap_tables = _conv_dft_tables(TOKEN_TILE)
            spec = _conv_spectrum(_pad_to(cv_w_dw[j], (tap_tables[0].shape[1], D)), tap_tables)
            for n, kd in enumerate(kinds):
                xs[n], h2s[n], affs[n] = _conv_mixer(
                    kd, xs[n], g1, mi, w_in, row(cv_b_in[j]), dft, spec, row(cv_b_dw[j]),
                    row(cv_ln_g[j]), row(cv_ln_b[j]), w_out, row(cv_b_out[j]), g2, wr)
        elif mixer == 1:
            w_qkv = at_w_qkv[j].astype(BF16)
            w_o = at_w_o[j].astype(BF16)
            qg, kg = row(at_q_norm[j]), row(at_k_norm[j])
            for n, kd in enumerate(kinds):
                if kd["lat"]:
                    cos, sin = _rope_tables(kd["L"])
                    q, k, v = _proj_qkv(kd, xs[n], g1, mi, w_qkv, qg, kg, cos, sin)
                    ck = cache_k[:, j].reshape(kd["B"], -1, NK)
                    cv = cache_v[:, j].reshape(kd["B"], -1, NK)
                    k_all = jnp.concatenate([ck, k.reshape(kd["B"], kd["L"], NK)], axis=1).astype(BF16)
                    v_all = jnp.concatenate([cv, v.reshape(kd["B"], kd["L"], NK)], axis=1).astype(BF16)
                else:
                    q, k, v = _proj_qkv(kd, xs[n], g1, mi, w_qkv, qg, kg, None, None)
                    new_k = k.reshape(kd["B"], 1, kd["L"], N_KV_HEADS, HEAD_DIM)
                    new_v = v.reshape(kd["B"], 1, kd["L"], N_KV_HEADS, HEAD_DIM)
                    k_all = k.reshape(kd["B"], kd["L"], NK).astype(BF16)
                    v_all = v.reshape(kd["B"], kd["L"], NK).astype(BF16)
                xs[n], h2s[n], affs[n] = _attention(kd, q, k_all, v_all, w_o, xs[n], mi, g2, wr)
        else:
            w_in = hy_w_in[j].astype(BF16)
            w_out = hy_w_out[j].astype(BF16)
            hp = LANES
            w1 = _pad_to(hy_f_w1[j], (LANES, hp))
            w2 = _pad_to(hy_f_w2[j], (hp, hp))
            w3 = _pad_to(hy_f_w3[j], (hp, 2 * D))
            b1, fr1 = _pad_to(row(hy_f_b1[j]), (1, hp)), _pad_to(row(hy_f_freq1[j]), (1, hp))
            b2, fr2 = _pad_to(row(hy_f_b2[j]), (1, hp)), _pad_to(row(hy_f_freq2[j]), (1, hp))
            for n, kd in enumerate(kinds):
                L = kd["L"]
                zfeat, decay = _filter_features(L)
                fc, fs, fs_fwd, fs_inv = _dft_tables(L)
                *taps, nyq = _hy_filter(L, zfeat, w1, b1, fr1, w2, b2, fr2, w3, decay)
                fc_split = _split_bf16(fc)
                spec = _hy_spectrum(L, fc_split, _split_bf16(fs), taps, nyq)
                u, ub, x0 = _hy_pre(kd, xs[n], g1, mi, w_in, row(hy_b_in[j]), hy_w_short[j], row(hy_b_short[j]))
                mats = (fc_split[0], fs_fwd.astype(BF16), fs_inv.astype(BF16))
                y = _hy_longconv(kd, ub, mats, spec)
                xs[n], h2s[n], affs[n] = _hy_out(kd, y, u, x0, row(hy_skip[j]), w_out, row(hy_b_out[j]),
                                                  xs[n], mi, g2, wr)

        slots, xes, gs = [], [], []
        for n, kd in enumerate(kinds):
            slot = _route(affs[n].reshape(kd["B"] * N_EXPERTS, kd["L"]), kd["cap"])
            ng = N_EXPERTS // kd["eg"]
            slot3 = slot.reshape(kd["B"] * ng, kd["eg"], kd["L"])
            aff3 = affs[n].reshape(kd["B"] * ng, kd["eg"], kd["L"])
            xe, gsel = _gather(kd, slot3, aff3, h2s[n])
            slots.append(slot3)
            xes.append(xe)
            gs.append(gsel)
        yes = _ffn(xes[0], xes[1], gs[0], gs[1], moe_w_gate, moe_w_up, moe_w_down, i)
        for n, kd in enumerate(kinds):
            xs[n] = _scatter(kd, slots[n], yes[n], xs[n], mi, final_row, final=(i == DEPTH - 1))

    y_prompt = xs[0].reshape(b_ctx, l_ctx, D)
    y_sample = xs[1].reshape(b_lat, l_lat, D)
    return (y_prompt, y_sample, new_k, new_v)
```

```python
import functools
import math

import jax
import jax.numpy as jnp
from jax import lax
from jax.experimental import pallas as pl
from jax.experimental.pallas import tpu as pltpu

F32 = jnp.float32
BF16 = jnp.bfloat16
HIGHEST = lax.Precision.HIGHEST

D = 1024
DEPTH = 4
GRID_W = 64
N_MIXERS = 3
HEAD_DIM = 128
N_HEADS = 8
N_KV_HEADS = 2
KV_GROUP = N_HEADS // N_KV_HEADS
NQ = N_HEADS * HEAD_DIM
NK = N_KV_HEADS * HEAD_DIM
ROPE_AXIS_DIM = HEAD_DIM // 2
ROPE_THETA = 10000.0
CONV_WIDTH = 31
CONV_PAD = CONV_WIDTH // 2
SHORT_WIDTH = 3
HY_BANDS = 16
HY_DECAY_TARGET = 1e-2
HY_SHORT_PCT = 0.3
HY_LONG_PCT = 1.5
N_EXPERTS = 16
EXPERT_FF = 1024
EC_CAPACITY_FACTOR = 2
NORM_EPS = 1e-6

LANES = 128
SUBLANES = 8
VMEM_LIMIT_BYTES = 56 * 1024 * 1024

TOKEN_TILE = 256
HALO = 16
CONV_ROW_CHUNK = 32
MOD_ROWS = 8
GATHER_ROWS = 1024
DISPATCH_TOKENS = 1024
FFN_F_TILE = 512
CUMSUM_CHUNK = 256
ATTN_STACK_MAX_KEYS = 512


def _cparams(*sem):
    return pltpu.CompilerParams(dimension_semantics=sem, vmem_limit_bytes=VMEM_LIMIT_BYTES)


def _norm_mod(x, g, sh, sc):
    y = x * lax.rsqrt(jnp.mean(x * x, axis=-1, keepdims=True) + NORM_EPS)
    return (y * g) * (1.0 + sc) + sh


def _split_bf16(a):
    hi = a.astype(BF16)
    return hi, (a - hi.astype(F32)).astype(BF16)


def _dot3(a_hi, a_lo, b_hi, b_lo):
    return (jnp.dot(a_hi, b_hi, preferred_element_type=F32) + jnp.dot(a_lo, b_hi, preferred_element_type=F32)
            + jnp.dot(a_hi, b_lo, preferred_element_type=F32))


def _mod_spec(rowfn, j):
    return pl.BlockSpec((1, 1, D), lambda *idx: (rowfn(*idx) * 6 + j, 0, 0))


def _row_spec(n):
    return pl.BlockSpec((1, n), lambda *idx: (0, 0))


def _full_spec(shape):
    nd = len(shape)
    return pl.BlockSpec(shape, lambda *idx: (0,) * nd)


def _tile_spec(tm, n=D):
    return pl.BlockSpec((tm, n), lambda i: (i, 0))


def _halo_specs(rows, tm, n=D):
    hb = tm // HALO
    last = rows // HALO - 1
    return (pl.BlockSpec((HALO, n), lambda i: (jnp.maximum(i * hb - 1, 0), 0)),
            pl.BlockSpec((HALO, n), lambda i: (jnp.minimum((i + 1) * hb, last), 0)))


def _mod_kernel(c_ref, w_ref, b_ref, o_ref):
    cv = c_ref[...]
    s_hi, s_lo = _split_bf16(cv * jax.nn.sigmoid(cv))
    w_hi, w_lo = _split_bf16(w_ref[0])
    o_ref[0] = _dot3(s_hi, s_lo, w_hi, w_lo) + b_ref[0]


def _mod_all(cond, mod_w, mod_b):
    tn = 1536
    n = 6 * D
    return pl.pallas_call(
        _mod_kernel,
        grid=(DEPTH, n // tn),
        in_specs=[
            pl.BlockSpec((MOD_ROWS, D), lambda i, j: (0, 0)),
            pl.BlockSpec((1, D, tn), lambda i, j: (i, 0, j)),
            pl.BlockSpec((1, 1, tn), lambda i, j: (i, 0, j)),
        ],
        out_specs=pl.BlockSpec((1, MOD_ROWS, tn), lambda i, j: (i, 0, j)),
        out_shape=jax.ShapeDtypeStruct((DEPTH, MOD_ROWS, n), F32),
        compiler_params=_cparams("parallel", "parallel"),
        name="mod_all",
    )(cond, mod_w, mod_b.reshape(DEPTH, 1, n))


def _router_tail(x_new, g2_ref, sh2_ref, sc2_ref, wr_ref, h_ref, aff_ref):
    h = _norm_mod(x_new, g2_ref[...], sh2_ref[0], sc2_ref[0])
    h_hi, h_lo = _split_bf16(h)
    h_ref[...] = h_hi
    nt = (((1,), (1,)), ((), ()))
    by_hi = lax.dot_general(wr_ref[...], h_hi, nt, preferred_element_type=F32)
    by_lo = lax.dot_general(wr_ref[0:N_EXPERTS, :], h_lo, nt, preferred_element_type=F32)
    logits = by_hi[0:N_EXPERTS] + by_hi[N_EXPERTS:] + by_lo
    e = jnp.exp(logits - jnp.max(logits, axis=0, keepdims=True))
    aff_ref[0] = e / jnp.sum(e, axis=0, keepdims=True)


def _router_in_specs(rowfn):
    return [_row_spec(D), _mod_spec(rowfn, 3), _mod_spec(rowfn, 4), _full_spec((2 * N_EXPERTS, D))]


def _router_out(kind, tm):
    nb = kind["L"] // tm
    rows = kind["B"] * kind["L"]
    specs = [_tile_spec(tm), pl.BlockSpec((1, N_EXPERTS, tm), lambda i: (i // nb, 0, i % nb))]
    shapes = [jax.ShapeDtypeStruct((rows, D), BF16),
              jax.ShapeDtypeStruct((kind["B"], N_EXPERTS, kind["L"]), F32)]
    return specs, shapes


def _project_with_halo(xp_ref, xc_ref, xn_ref, g_ref, sh_ref, sc_ref, w_ref, b_ref, hs_ref, tm):
    g, sh, sc = g_ref[...], sh_ref[0], sc_ref[0]
    hs_ref[0:HALO, :] = _norm_mod(xp_ref[...], g, sh, sc).astype(BF16)
    hs_ref[HALO:HALO + tm, :] = _norm_mod(xc_ref[...], g, sh, sc).astype(BF16)
    hs_ref[HALO + tm:2 * HALO + tm, :] = _norm_mod(xn_ref[...], g, sh, sc).astype(BF16)
    return jnp.dot(hs_ref[...], w_ref[...], preferred_element_type=F32) + b_ref[...]


def _zero_outside_sequence(zb_ref, j, nb, tm):
    @pl.when(j == 0)
    def _():
        zb_ref[0:HALO, :] = jnp.zeros((HALO, zb_ref.shape[1]), F32)

    @pl.when(j == nb - 1)
    def _():
        zb_ref[HALO + tm:2 * HALO + tm, :] = jnp.zeros((HALO, zb_ref.shape[1]), F32)


def _conv_kernel(xp_ref, xc_ref, xn_ref, g1_ref, sh1_ref, sc1_ref, win_ref, bin_ref,
                 fwd_hi_ref, fwd_lo_ref, inv_hi_ref, inv_lo_ref, kra_ref, krb_ref, ki_ref, bdw_ref,
                 lg_ref, lb_ref, wo_ref, bo_ref, gate_ref, g2_ref, sh2_ref, sc2_ref, wr_ref,
                 o_ref, h_ref, aff_ref, hs_ref, *, nb, tm):
    j = pl.program_id(0) % nb
    n = tm + 2 * HALO
    y = _project_with_halo(xp_ref, xc_ref, xn_ref, g1_ref, sh1_ref, sc1_ref, win_ref, bin_ref, hs_ref, tm)
    z = y[:, :D] * jax.nn.sigmoid(y[:, D:])
    row = lax.broadcasted_iota(jnp.int32, (n, 1), 0)
    first_valid = jnp.where(j > 0, 0, HALO)
    end_valid = jnp.where(j < nb - 1, n, HALO + tm)
    z = jnp.where((row >= first_valid) & (row < end_valid), z, 0.0)
    zb = z.astype(BF16)
    u = (jnp.dot(fwd_hi_ref[...], zb, preferred_element_type=F32)
         + jnp.dot(fwd_lo_ref[...], zb, preferred_element_type=F32))
    ur, ui = u[:n // 2], u[n // 2:]
    ki = ki_ref[...]
    yr = ur * kra_ref[...] - ui * ki
    yi = ur * ki + ui * krb_ref[...]
    yb = jnp.concatenate([yr, yi], axis=0).astype(BF16)
    acc = (jnp.dot(inv_hi_ref[...], yb, preferred_element_type=F32)
           + jnp.dot(inv_lo_ref[...], yb, preferred_element_type=F32) + bdw_ref[...])
    xc = acc - jnp.mean(acc, axis=-1, keepdims=True)
    yn = xc * lax.rsqrt(jnp.mean(xc * xc, axis=-1, keepdims=True) + NORM_EPS)
    yn = yn * lg_ref[...] + lb_ref[...]
    act = (yn * jax.nn.sigmoid(yn)).astype(BF16)
    out = jnp.dot(act, wo_ref[...], preferred_element_type=F32) + bo_ref[...]
    x_new = xc_ref[...] + gate_ref[0] * out
    o_ref[...] = x_new
    _router_tail(x_new, g2_ref, sh2_ref, sc2_ref, wr_ref, h_ref, aff_ref)


def _conv_spectrum_kernel(cw_ref, sw_ref, sgn_ref, w_ref, kra_ref, krb_ref, ki_ref, *, n):
    w = w_ref[...]
    kr = jnp.dot(cw_ref[...], w, preferred_element_type=F32, precision=HIGHEST)
    ki = jnp.dot(sw_ref[...], w, preferred_element_type=F32, precision=HIGHEST)
    nyq = jnp.sum(w * sgn_ref[...], axis=0, keepdims=True)
    row = lax.broadcasted_iota(jnp.int32, kr.shape, 0)
    scale = jnp.where(row == 0, 1.0 / n, 2.0 / n)
    kr = kr * scale
    kra_ref[...] = kr
    krb_ref[...] = jnp.where(row == 0, nyq * (1.0 / n), kr)
    ki_ref[...] = ki * scale


def _conv_spectrum(w_dw_padded, tables):
    cw, sw, sgn = tables
    n = 2 * cw.shape[0]
    out = jax.ShapeDtypeStruct((n // 2, D), F32)
    return pl.pallas_call(
        functools.partial(_conv_spectrum_kernel, n=n),
        out_shape=[out, out, out],
        compiler_params=pltpu.CompilerParams(vmem_limit_bytes=VMEM_LIMIT_BYTES),
        name="conv_spectrum",
    )(cw, sw, sgn, w_dw_padded)


def _conv_mixer(kind, x, g1, mi, w_in, b_in, dft, spec, b_dw, ln_g, ln_b, w_out, b_out, g2, wr):
    rows = x.shape[0]
    tm = TOKEN_TILE
    n = tm + 2 * HALO
    nb = kind["L"] // tm
    rowfn = kind["rowfn"](nb)
    prev_spec, next_spec = _halo_specs(rows, tm)
    r_specs, r_shapes = _router_out(kind, tm)
    return pl.pallas_call(
        functools.partial(_conv_kernel, nb=nb, tm=tm),
        grid=(rows // tm,),
        in_specs=[prev_spec, _tile_spec(tm), next_spec,
                  _row_spec(D), _mod_spec(rowfn, 0), _mod_spec(rowfn, 1),
                  _full_spec((D, 2 * D)), _row_spec(2 * D),
                  _full_spec((n, n)), _full_spec((n, n)), _full_spec((tm, n)), _full_spec((tm, n)),
                  _full_spec((n // 2, D)), _full_spec((n // 2, D)), _full_spec((n // 2, D)),
                  _row_spec(D), _row_spec(D), _row_spec(D),
                  _full_spec((D, D)), _row_spec(D), _mod_spec(rowfn, 2)] + _router_in_specs(rowfn),
        out_specs=[_tile_spec(tm)] + r_specs,
        out_shape=[jax.ShapeDtypeStruct((rows, D), F32)] + r_shapes,
        scratch_shapes=[pltpu.VMEM((n, D), BF16)],
        compiler_params=_cparams("parallel"),
        name="conv_mixer",
    )(x, x, x, g1, mi, mi, w_in, b_in, *dft, *spec, b_dw, ln_g, ln_b, w_out, b_out, mi, g2, mi, mi, wr)


def _head_norm(seg, g):
    return seg * lax.rsqrt(jnp.mean(seg * seg, axis=-1, keepdims=True) + NORM_EPS) * g


def _proj_qkv_kernel(x_ref, g_ref, sh_ref, sc_ref, w_ref, qg_ref, kg_ref, *rest, rope):
    if rope:
        cos_ref, sin_ref, q_ref, k_ref, v_ref = rest
    else:
        q_ref, k_ref, v_ref = rest
    h = _norm_mod(x_ref[...], g_ref[...], sh_ref[0], sc_ref[0])
    y = jnp.dot(h.astype(BF16), w_ref[...], preferred_element_type=F32)
    if rope:
        cos = cos_ref[...]
        sin = sin_ref[...]
        lane = lax.broadcasted_iota(jnp.int32, (y.shape[0], HEAD_DIM), 1)
        even = (lane & 1) == 0
    for hd in range(N_HEADS + N_KV_HEADS):
        seg = y[:, hd * HEAD_DIM:(hd + 1) * HEAD_DIM]
        nrm = _head_norm(seg, qg_ref[...] if hd < N_HEADS else kg_ref[...])
        if rope:
            partner = jnp.where(even, pltpu.roll(nrm, HEAD_DIM - 1, 1), pltpu.roll(nrm, 1, 1))
            nrm = nrm * cos + partner * sin
        if hd < N_HEADS:
            q_ref[:, hd * HEAD_DIM:(hd + 1) * HEAD_DIM] = (nrm * HEAD_DIM ** -0.5).astype(BF16)
        else:
            k_ref[:, (hd - N_HEADS) * HEAD_DIM:(hd - N_HEADS + 1) * HEAD_DIM] = nrm
    v_ref[...] = y[:, NQ + NK:]


def _proj_qkv(kind, x, g, mi, w, qg, kg, cos, sin):
    rows = x.shape[0]
    tm = TOKEN_TILE
    nb = kind["L"] // tm
    rowfn = kind["rowfn"](nb)
    rope = cos is not None
    specs = [_tile_spec(tm), _row_spec(D), _mod_spec(rowfn, 0), _mod_spec(rowfn, 1),
             _full_spec((D, NQ + 2 * NK)), _row_spec(HEAD_DIM), _row_spec(HEAD_DIM)]
    args = [x, g, mi, mi, w, qg, kg]
    if rope:
        specs += [pl.BlockSpec((tm, HEAD_DIM), lambda i: (i % nb, 0))] * 2
        args += [cos, sin]
    return pl.pallas_call(
        functools.partial(_proj_qkv_kernel, rope=rope),
        grid=(rows // tm,),
        in_specs=specs,
        out_specs=[_tile_spec(tm, NQ), _tile_spec(tm, NK), _tile_spec(tm, NK)],
        out_shape=[jax.ShapeDtypeStruct((rows, NQ), BF16),
                   jax.ShapeDtypeStruct((rows, NK), F32),
                   jax.ShapeDtypeStruct((rows, NK), F32)],
        compiler_params=_cparams("parallel"),
        name="proj_qkv",
    )(*args)


def _attn_kernel(q_ref, k_ref, v_ref, wo_ref, x_ref, gate_ref, g2_ref, sh2_ref, sc2_ref, wr_ref,
                 o_ref, h_ref, aff_ref, *, stack_heads):
    tq = q_ref.shape[0]
    outs = [None] * N_HEADS
    if stack_heads:
        groups = [list(range(kv * KV_GROUP, (kv + 1) * KV_GROUP)) for kv in range(N_KV_HEADS)]
    else:
        groups = [[hd] for hd in range(N_HEADS)]
    for heads in groups:
        kv = heads[0] // KV_GROUP
        qs = jnp.concatenate([q_ref[:, hd * HEAD_DIM:(hd + 1) * HEAD_DIM] for hd in heads], axis=0)
        kh = k_ref[0, :, kv * HEAD_DIM:(kv + 1) * HEAD_DIM]
        vh = v_ref[0, :, kv * 2 * HEAD_DIM:(kv + 1) * 2 * HEAD_DIM]
        s = lax.dot_general(qs, kh, (((1,), (1,)), ((), ())), preferred_element_type=F32)
        p = jnp.exp((s - jnp.max(s, axis=-1, keepdims=True)).astype(BF16))
        oa = jnp.dot(p, vh, preferred_element_type=F32)
        on = oa[:, :HEAD_DIM] / oa[:, HEAD_DIM:HEAD_DIM + 1]
        for g, hd in enumerate(heads):
            outs[hd] = on[g * tq:(g + 1) * tq, :]
    o = jnp.concatenate(outs, axis=1).astype(BF16)
    y = jnp.dot(o, wo_ref[...], preferred_element_type=F32)
    x_new = x_ref[...] + gate_ref[0] * y
    o_ref[...] = x_new
    _router_tail(x_new, g2_ref, sh2_ref, sc2_ref, wr_ref, h_ref, aff_ref)


def _attention(kind, q, k_all, v_all, w_o, x, mi, g2, wr):
    rows = x.shape[0]
    tq = TOKEN_TILE
    nb = kind["L"] // tq
    rowfn = kind["rowfn"](nb)
    s_len = k_all.shape[1]
    kv_spec = pl.BlockSpec((1, s_len, NK), lambda i: (i // nb, 0, 0))
    v_spec = pl.BlockSpec((1, s_len, 2 * NK), lambda i: (i // nb, 0, 0))
    r_specs, r_shapes = _router_out(kind, tq)
    return pl.pallas_call(
        functools.partial(_attn_kernel, stack_heads=s_len <= ATTN_STACK_MAX_KEYS),
        grid=(rows // tq,),
        in_specs=[_tile_spec(tq), kv_spec, v_spec, _full_spec((D, D)), _tile_spec(tq),
                  _mod_spec(rowfn, 2)] + _router_in_specs(rowfn),
        out_specs=[_tile_spec(tq)] + r_specs,
        out_shape=[jax.ShapeDtypeStruct((rows, D), F32)] + r_shapes,
        compiler_params=_cparams("parallel"),
        name="attention",
    )(q, k_all, v_all, w_o, x, mi, g2, mi, mi, wr)


def _hy_pre_kernel(xp_ref, xc_ref, xn_ref, g1_ref, sh1_ref, sc1_ref, win_ref, bin_ref, w_ref, b_ref,
                   ub_ref, x0_ref, hs_ref, zb_ref, *, nb, tm):
    j = pl.program_id(0) % nb
    zb_ref[...] = _project_with_halo(xp_ref, xc_ref, xn_ref, g1_ref, sh1_ref, sc1_ref, win_ref, bin_ref,
                                     hs_ref, tm)
    _zero_outside_sequence(zb_ref, j, nb, tm)
    rc = CONV_ROW_CHUNK
    for r0 in range(0, tm, rc):
        parts = []
        for part in range(3):
            cs = slice(part * D, (part + 1) * D)
            lo = r0 + HALO - SUBLANES
            nwin = rc + 2 * SUBLANES
            win = zb_ref[lo:lo + nwin, cs]
            z = b_ref[:, cs]
            for k in range(SHORT_WIDTH):
                shift = (SHORT_WIDTH // 2 - k) % nwin
                moved = win if shift == 0 else pltpu.roll(win, shift, 0)
                z = z + moved[SUBLANES:SUBLANES + rc, :] * w_ref[k:k + 1, cs]
            parts.append(z)
        x0_ref[r0:r0 + rc, :] = parts[0]
        ub_ref[r0:r0 + rc, :] = (parts[2] * parts[1]).astype(BF16)


def _hy_pre(kind, x, g1, mi, w_in, b_in, w_short, b_short):
    rows = x.shape[0]
    tm = TOKEN_TILE
    nb = kind["L"] // tm
    rowfn = kind["rowfn"](nb)
    prev_spec, next_spec = _halo_specs(rows, tm)
    return pl.pallas_call(
        functools.partial(_hy_pre_kernel, nb=nb, tm=tm),
        grid=(rows // tm,),
        in_specs=[prev_spec, _tile_spec(tm), next_spec,
                  _row_spec(D), _mod_spec(rowfn, 0), _mod_spec(rowfn, 1),
                  _full_spec((D, 3 * D)), _row_spec(3 * D),
                  _full_spec((SHORT_WIDTH, 3 * D)), _row_spec(3 * D)],
        out_specs=[_tile_spec(tm)] * 2,
        out_shape=[jax.ShapeDtypeStruct((rows, D), BF16), jax.ShapeDtypeStruct((rows, D), F32)],
        scratch_shapes=[pltpu.VMEM((tm + 2 * HALO, D), BF16), pltpu.VMEM((tm + 2 * HALO, 3 * D), F32)],
        compiler_params=_cparams("parallel"),
        name="hy_pre",
    )(x, x, x, g1, mi, mi, w_in, b_in, w_short, b_short)


def _hy_filter_kernel(z_ref, w1_ref, b1_ref, f1_ref, w2_ref, b2_ref, f2_ref, w3f_ref, w3b_ref,
                      dec_ref, ha_hi_ref, ha_lo_ref, hb_hi_ref, hb_lo_ref, nyq_ref, hid_ref):
    @pl.when(pl.program_id(0) == 0)
    def _():
        h1 = jnp.sin(f1_ref[...] * (jnp.dot(z_ref[...], w1_ref[...], preferred_element_type=F32,
                                            precision=HIGHEST) + b1_ref[...]))
        hid_ref[...] = jnp.sin(f2_ref[...] * (jnp.dot(h1, w2_ref[...], preferred_element_type=F32,
                                                      precision=HIGHEST) + b2_ref[...]))

    f = hid_ref[...]
    dec = dec_ref[...]
    hf = jnp.dot(f, w3f_ref[...], preferred_element_type=F32, precision=HIGHEST) * dec
    hb = jnp.dot(f, w3b_ref[...], preferred_element_type=F32, precision=HIGHEST) * dec
    row = lax.broadcasted_iota(jnp.int32, hf.shape, 0)
    hb = jnp.where(row == 0, 0.0, hb)
    ha = hf + hb
    ha_hi_ref[...], ha_lo_ref[...] = _split_bf16(ha)
    hb_hi_ref[...], hb_lo_ref[...] = _split_bf16(hf - hb)
    sign = jnp.where((row & 1) == 0, 1.0, -1.0)
    nyq_ref[...] = jnp.sum(ha * sign, axis=0, keepdims=True)


def _hy_filter(L, zfeat, w1, b1, fr1, w2, b2, fr2, w3, decay):
    tc = 256
    nct = D // tc
    hp = w1.shape[1]
    c_spec = pl.BlockSpec((L, tc), lambda c: (0, c))
    return pl.pallas_call(
        _hy_filter_kernel,
        grid=(nct,),
        in_specs=[
            _full_spec(zfeat.shape), _full_spec(w1.shape), _row_spec(hp), _row_spec(hp),
            _full_spec(w2.shape), _row_spec(hp), _row_spec(hp),
            pl.BlockSpec((hp, tc), lambda c: (0, c)),
            pl.BlockSpec((hp, tc), lambda c: (0, nct + c)),
            c_spec,
        ],
        out_specs=[c_spec] * 4 + [pl.BlockSpec((1, tc), lambda c: (0, c))],
        out_shape=[jax.ShapeDtypeStruct((L, D), BF16)] * 4 + [jax.ShapeDtypeStruct((1, D), F32)],
        scratch_shapes=[pltpu.VMEM((L, hp), F32)],
        compiler_params=_cparams("arbitrary"),
        name="hy_filter",
    )(zfeat, w1, b1, fr1, w2, b2, fr2, w3, w3, decay)


def _hy_spectrum_kernel(fc_hi_ref, fc_lo_ref, fs_hi_ref, fs_lo_ref, ha_hi_ref, ha_lo_ref, hb_hi_ref, hb_lo_ref,
                        nyq_ref, kra_ref, krb_ref, ki_ref, *, L, ft):
    kr = _dot3(fc_hi_ref[...], fc_lo_ref[...], ha_hi_ref[...], ha_lo_ref[...])
    ki = _dot3(fs_hi_ref[...], fs_lo_ref[...], hb_hi_ref[...], hb_lo_ref[...])
    row = lax.broadcasted_iota(jnp.int32, kr.shape, 0) + pl.program_id(1) * ft
    scale = jnp.where(row == 0, 0.5 / L, 1.0 / L)
    kr = kr * scale
    kra_ref[...] = kr
    krb_ref[...] = jnp.where(row == 0, nyq_ref[...] * (0.5 / L), kr)
    ki_ref[...] = ki * scale


def _hy_spectrum(L, fc_split, fs_split, taps, nyq):
    ft = min(L, 256)
    tc = 512
    mat_spec = pl.BlockSpec((ft, L), lambda c, k: (k, 0))
    h_spec = pl.BlockSpec((L, tc), lambda c, k: (0, c))
    o_spec = pl.BlockSpec((ft, tc), lambda c, k: (k, c))
    return pl.pallas_call(
        functools.partial(_hy_spectrum_kernel, L=L, ft=ft),
        grid=(D // tc, L // ft),
        in_specs=[mat_spec] * 4 + [h_spec] * 4 + [pl.BlockSpec((1, tc), lambda c, k: (0, c))],
        out_specs=[o_spec, o_spec, o_spec],
        out_shape=[jax.ShapeDtypeStruct((L, D), F32)] * 3,
        compiler_params=_cparams("parallel", "parallel"),
        name="hy_spectrum",
    )(*fc_split, *fs_split, *taps, nyq)


def _hy_longconv_kernel(ub_ref, fc_ref, fs_ref, gc_ref, gs_ref, kra_ref, krb_ref, ki_ref, x0_ref, skip_ref,
                        o_ref, acc_ref):
    k = pl.program_id(2)

    @pl.when(k == 0)
    def _():
        acc_ref[...] = jnp.zeros_like(acc_ref)

    ub = ub_ref[...]
    ur = jnp.dot(fc_ref[...], ub, preferred_element_type=F32)
    ui = jnp.dot(fs_ref[...], ub, preferred_element_type=F32)
    kra = kra_ref[...]
    krb = krb_ref[...]
    ki = ki_ref[...]
    yr = (ur * kra - ui * ki).astype(BF16)
    yi = (ur * ki + ui * krb).astype(BF16)
    acc_ref[...] += jnp.dot(gc_ref[...], yr, preferred_element_type=F32)
    acc_ref[...] += jnp.dot(gs_ref[...], yi, preferred_element_type=F32)

    @pl.when(k == pl.num_programs(2) - 1)
    def _():
        y = acc_ref[...] + ub_ref[...].astype(F32) * skip_ref[...]
        o_ref[...] = (y * x0_ref[...]).astype(BF16)


def _hy_longconv(kind, ub, mats, spec, x0, skip):
    B, L = kind["B"], kind["L"]
    fc, fs, fst = mats
    kra, krb, ki = spec
    ft = min(L, 512)
    tc = 1024 if L <= 256 else 512
    d_spec = pl.BlockSpec((L, tc), lambda b, c, k: (b, c))
    fwd_spec = pl.BlockSpec((ft, L), lambda b, c, k: (k, 0))
    inv_spec = pl.BlockSpec((L, ft), lambda b, c, k: (0, k))
    k_spec = pl.BlockSpec((ft, tc), lambda b, c, k: (k, c))
    return pl.pallas_call(
        _hy_longconv_kernel,
        grid=(B, D // tc, L // ft),
        in_specs=[d_spec, fwd_spec, fwd_spec, inv_spec, inv_spec, k_spec, k_spec, k_spec, d_spec,
                  pl.BlockSpec((1, tc), lambda b, c, k: (0, c))],
        out_specs=d_spec,
        out_shape=jax.ShapeDtypeStruct((B * L, D), BF16),
        scratch_shapes=[pltpu.VMEM((L, tc), F32)],
        compiler_params=_cparams("parallel", "parallel", "arbitrary"),
        name="hy_longconv",
    )(ub, fc, fs, fc, fst, kra, krb, ki, x0, skip)


def _hy_out_kernel(a_ref, w_ref, b_ref, x_ref, gate_ref,
                   g2_ref, sh2_ref, sc2_ref, wr_ref, o_ref, h_ref, aff_ref):
    y = jnp.dot(a_ref[...], w_ref[...], preferred_element_type=F32) + b_ref[...]
    x_new = x_ref[...] + gate_ref[0] * y
    o_ref[...] = x_new
    _router_tail(x_new, g2_ref, sh2_ref, sc2_ref, wr_ref, h_ref, aff_ref)


def _hy_out(kind, a, w, b, x, mi, g2, wr):
    rows = x.shape[0]
    tm = TOKEN_TILE
    rowfn = kind["rowfn"](kind["L"] // tm)
    t_spec = _tile_spec(tm)
    r_specs, r_shapes = _router_out(kind, tm)
    return pl.pallas_call(
        _hy_out_kernel,
        grid=(rows // tm,),
        in_specs=[t_spec, _full_spec((D, D)), _row_spec(D), t_spec,
                  _mod_spec(rowfn, 2)] + _router_in_specs(rowfn),
        out_specs=[t_spec] + r_specs,
        out_shape=[jax.ShapeDtypeStruct((rows, D), F32)] + r_shapes,
        compiler_params=_cparams("parallel"),
        name="hy_out",
    )(a, w, b, x, mi, g2, mi, mi, wr)


def _lane_cumsum(mask_f32, tri):
    rows, n = mask_f32.shape
    run = jnp.zeros((rows, 1), F32)
    pieces = []
    for c0 in range(0, n, CUMSUM_CHUNK):
        chunk = mask_f32[:, c0:c0 + CUMSUM_CHUNK]
        pieces.append(jnp.dot(chunk.astype(BF16), tri, preferred_element_type=F32) + run)
        run = run + jnp.sum(chunk, axis=1, keepdims=True)
    return jnp.concatenate(pieces, axis=1) if len(pieces) > 1 else pieces[0]


def _route_kernel(aff_ref, slot_ref, *, cap):
    bits = pltpu.bitcast(aff_ref[...], jnp.int32)
    rows = bits.shape[0]
    thr = jnp.zeros((rows, 1), jnp.int32)
    capf = float(cap)
    for bit in range(30, -1, -1):
        cand = thr | (1 << bit)
        cnt = jnp.sum(jnp.where(bits >= cand, 1.0, 0.0), axis=1, keepdims=True)
        thr = jnp.where(cnt >= capf, cand, thr)
    r_i = lax.broadcasted_iota(jnp.int32, (CUMSUM_CHUNK, CUMSUM_CHUNK), 0)
    c_i = lax.broadcasted_iota(jnp.int32, (CUMSUM_CHUNK, CUMSUM_CHUNK), 1)
    tri = jnp.where(r_i <= c_i, 1.0, 0.0).astype(BF16)
    gt = jnp.where(bits > thr, 1.0, 0.0)
    eq = jnp.where(bits == thr, 1.0, 0.0)
    need = capf - jnp.sum(gt, axis=1, keepdims=True)
    eq_rank = _lane_cumsum(eq, tri)
    sel = gt + eq * jnp.where(eq_rank <= need, 1.0, 0.0)
    pos = _lane_cumsum(sel, tri)
    slot_ref[...] = jnp.where(sel > 0.5, pos - 1.0, -1.0).astype(jnp.int32)


def _route(aff2d, cap):
    return pl.pallas_call(
        functools.partial(_route_kernel, cap=cap),
        out_shape=jax.ShapeDtypeStruct(aff2d.shape, jnp.int32),
        compiler_params=pltpu.CompilerParams(vmem_limit_bytes=VMEM_LIMIT_BYTES),
        name="route",
    )(aff2d)


def _dispatch_masks(slot, cap):
    eg, n = slot.shape
    iota = lax.broadcasted_iota(jnp.int32, (cap, n), 0)
    return [iota == slot[e:e + 1, :] for e in range(eg)]


def _onehot(masks):
    return jnp.concatenate([jnp.where(m, 1.0, 0.0).astype(BF16) for m in masks], axis=0)


def _gather_kernel(slot_ref, aff_ref, h_ref, xe_ref, g_ref, *, cap, eg, bt, seq):
    for b in range(bt):
        masks = _dispatch_masks(slot_ref[b], cap)
        aff = aff_ref[b]
        xe = jnp.dot(_onehot(masks), h_ref[b * seq:(b + 1) * seq, :], preferred_element_type=F32)
        xe_ref[:, b * cap:(b + 1) * cap, :] = xe.reshape(eg, cap, D).astype(BF16)
        for e in range(eg):
            gsel = jnp.sum(jnp.where(masks[e], aff[e:e + 1, :], 0.0), axis=1, keepdims=True)
            g_ref[e, b * cap:(b + 1) * cap, :] = jnp.broadcast_to(gsel, (cap, LANES))


def _gather(kind, slot3, aff3, h2):
    B, L, cap, eg, bt = kind["B"], kind["L"], kind["cap"], kind["eg"], kind["bt"]
    ng = N_EXPERTS // eg
    assert bt == 1 or ng == 1
    r_spec = pl.BlockSpec((bt, eg, L), lambda b, e: (b * ng + e, 0, 0))
    return pl.pallas_call(
        functools.partial(_gather_kernel, cap=cap, eg=eg, bt=bt, seq=L),
        grid=(B // bt, ng),
        in_specs=[r_spec, r_spec, pl.BlockSpec((bt * L, D), lambda b, e: (b, 0))],
        out_specs=[pl.BlockSpec((eg, bt * cap, D), lambda b, e: (e, b, 0)),
                   pl.BlockSpec((eg, bt * cap, LANES), lambda b, e: (e, b, 0))],
        out_shape=[jax.ShapeDtypeStruct((N_EXPERTS, B * cap, D), BF16),
                   jax.ShapeDtypeStruct((N_EXPERTS, B * cap, LANES), F32)],
        compiler_params=_cparams("parallel", "parallel"),
        name="moe_gather",
    )(slot3, aff3, h2)


def _ffn_kernel(xa_ref, xb_ref, ga_ref, gb_ref, wg_ref, wu_ref, wd_ref, ya_ref, yb_ref, acc_ref):
    f = pl.program_id(1)

    @pl.when(f == 0)
    def _():
        acc_ref[...] = jnp.zeros_like(acc_ref)

    wg = wg_ref[0, 0].astype(BF16)
    wu = wu_ref[0, 0].astype(BF16)
    wd = wd_ref[0, 0].astype(BF16)
    na = xa_ref.shape[1]
    for x_ref, r0 in ((xa_ref, 0), (xb_ref, na)):
        x = x_ref[0]
        a = jnp.dot(x, wg, preferred_element_type=F32)
        u = jnp.dot(x, wu, preferred_element_type=F32)
        mid = (a * jax.nn.sigmoid(a) * u).astype(BF16)
        acc_ref[r0:r0 + x.shape[0], :] += jnp.dot(mid, wd, preferred_element_type=F32)

    @pl.when(f == pl.num_programs(1) - 1)
    def _():
        ya_ref[0] = (acc_ref[0:na, :] * ga_ref[0][:, 0:1]).astype(BF16)
        yb_ref[0] = (acc_ref[na:, :] * gb_ref[0][:, 0:1]).astype(BF16)


def _ffn(xe_a, xe_b, g_a, g_b, w_gate, w_up, w_down, layer):
    na, nb_ = xe_a.shape[1], xe_b.shape[1]
    tf = FFN_F_TILE
    return pl.pallas_call(
        _ffn_kernel,
        grid=(N_EXPERTS, EXPERT_FF // tf),
        in_specs=[
            pl.BlockSpec((1, na, D), lambda e, f: (e, 0, 0)),
            pl.BlockSpec((1, nb_, D), lambda e, f: (e, 0, 0)),
            pl.BlockSpec((1, na, LANES), lambda e, f: (e, 0, 0)),
            pl.BlockSpec((1, nb_, LANES), lambda e, f: (e, 0, 0)),
            pl.BlockSpec((1, 1, D, tf), lambda e, f: (layer, e, 0, f)),
            pl.BlockSpec((1, 1, D, tf), lambda e, f: (layer, e, 0, f)),
            pl.BlockSpec((1, 1, tf, D), lambda e, f: (layer, e, f, 0)),
        ],
        out_specs=[pl.BlockSpec((1, na, D), lambda e, f: (e, 0, 0)),
                   pl.BlockSpec((1, nb_, D), lambda e, f: (e, 0, 0))],
        out_shape=[jax.ShapeDtypeStruct((N_EXPERTS, na, D), BF16),
                   jax.ShapeDtypeStruct((N_EXPERTS, nb_, D), BF16)],
        scratch_shapes=[pltpu.VMEM((na + nb_, D), F32)],
        compiler_params=_cparams("parallel", "arbitrary"),
        name="moe_ffn",
    )(xe_a, xe_b, g_a, g_b, w_gate, w_up, w_down)


def _scatter_kernel(slot_ref, ye_ref, x_ref, gate_ref, fg_ref, o_ref, acc_ref, *, cap, eg, bt, lt, final):
    e = pl.program_id(2)

    @pl.when(e == 0)
    def _():
        acc_ref[...] = jnp.zeros_like(acc_ref)

    tn = (((0,), (0,)), ((), ()))
    for b in range(bt):
        onehot = _onehot(_dispatch_masks(slot_ref[b], cap))
        ye = ye_ref[:, b * cap:(b + 1) * cap, :].reshape(eg * cap, D)
        acc_ref[b * lt:(b + 1) * lt, :] += lax.dot_general(onehot, ye, tn, preferred_element_type=F32)

    @pl.when(e == pl.num_programs(2) - 1)
    def _():
        x = x_ref[...] + gate_ref[0] * acc_ref[...]
        if final:
            x = x * lax.rsqrt(jnp.mean(x * x, axis=-1, keepdims=True) + NORM_EPS) * fg_ref[...]
        o_ref[...] = x


def _scatter(kind, slot3, ye, x, mi, final_g, final):
    B, L, cap, eg, bt = kind["B"], kind["L"], kind["cap"], kind["eg"], kind["bt"]
    ng = N_EXPERTS // eg
    lt = min(L, DISPATCH_TOKENS)
    nl = L // lt
    assert bt == 1 or (ng == 1 and nl == 1)
    rowfn = lambda b, l, e: kind["rowfn"](1)(b)
    x_spec = pl.BlockSpec((bt * lt, D), lambda b, l, e: (b * nl + l, 0))
    return pl.pallas_call(
        functools.partial(_scatter_kernel, cap=cap, eg=eg, bt=bt, lt=lt, final=final),
        grid=(B // bt, nl, ng),
        in_specs=[pl.BlockSpec((bt, eg, lt), lambda b, l, e: (b * ng + e, 0, l)),
                  pl.BlockSpec((eg, bt * cap, D), lambda b, l, e: (e, b, 0)),
                  x_spec, _mod_spec(rowfn, 5), _row_spec(D)],
        out_specs=x_spec,
        out_shape=jax.ShapeDtypeStruct((B * L, D), F32),
        scratch_shapes=[pltpu.VMEM((bt * lt, D), F32)],
        compiler_params=_cparams("parallel", "parallel", "arbitrary"),
        name="moe_scatter",
    )(slot3, ye, x, mi, final_g)


def _rope_tables(L):
    n_rows = L // GRID_W
    rows = jnp.repeat(jnp.arange(n_rows), GRID_W).astype(F32)
    cols = jnp.tile(jnp.arange(GRID_W), n_rows).astype(F32)
    inv = ROPE_THETA ** (-jnp.arange(0, ROPE_AXIS_DIM, 2, dtype=F32) / ROPE_AXIS_DIM)
    ang = jnp.concatenate([rows[:, None] * inv, cols[:, None] * inv], axis=-1)
    cos = jnp.repeat(jnp.cos(ang), 2, axis=-1)
    sin = jnp.repeat(jnp.sin(ang), 2, axis=-1)
    sign = jnp.tile(jnp.array([-1.0, 1.0], F32), HEAD_DIM // 2)
    return cos, sin * sign


def _dft_tables(L):
    r = min(L, 64)
    s = jnp.arange(L, dtype=jnp.int32)

    def small(f):
        ang = ((f[:, None] * s[None, :]) % (2 * L)).astype(F32) * (math.pi / L)
        return jnp.cos(ang), jnp.sin(ang)

    c0, s0 = small(jnp.arange(r, dtype=jnp.int32))
    c1, s1 = small(jnp.arange(L // r, dtype=jnp.int32) * r)
    fc = (c1[:, None, :] * c0[None] - s1[:, None, :] * s0[None]).reshape(L, L)
    fs = -(s1[:, None, :] * c0[None] + c1[:, None, :] * s0[None]).reshape(L, L)
    sign = jnp.where(s % 2 == 0, 1.0, -1.0).astype(F32)
    fs_fwd = jnp.where(s[:, None] == 0, sign[None, :], fs)
    fs_inv = jnp.where(s[None, :] == 0, sign[:, None], fs)
    return fc, fs, fs_fwd, fs_inv


def _conv_dft_tables(tm):
    n = tm + 2 * HALO
    f = jnp.arange(n // 2, dtype=jnp.int32)
    r = jnp.arange(n, dtype=jnp.int32)

    def cs(pos):
        ang = ((f[:, None] * pos[None, :]) % n).astype(F32) * (2.0 * math.pi / n)
        return jnp.cos(ang), -jnp.sin(ang)

    fc, fs = cs(r)
    fs = fs.at[0, :].set(jnp.where(r % 2 == 0, 1.0, -1.0).astype(F32))
    fwd = jnp.concatenate([fc, fs], axis=0)
    inv = jnp.concatenate([fc[:, HALO:HALO + tm].T, fs[:, HALO:HALO + tm].T], axis=1)
    k = jnp.arange(2 * SUBLANES * ((CONV_WIDTH + 2 * SUBLANES - 1) // (2 * SUBLANES)), dtype=jnp.int32)
    lag = (CONV_PAD - k) % n
    cw, sw = cs(lag)
    live = (k < CONV_WIDTH)[None, :]
    cw, sw = jnp.where(live, cw, 0.0), jnp.where(live, sw, 0.0)
    sgn = jnp.where((CONV_PAD - k) % 2 == 0, 1.0, -1.0).astype(F32)[:, None]
    return _split_bf16(fwd) + _split_bf16(inv), (cw, sw, sgn)


def _filter_features(L):
    t = jnp.arange(L, dtype=F32) / L
    bands = jnp.arange(1, HY_BANDS + 1, dtype=F32)
    ph = 2.0 * math.pi * t[:, None] * bands
    z = jnp.concatenate([t[:, None], jnp.sin(ph), jnp.cos(ph)], axis=-1)
    z = jnp.pad(z, ((0, 0), (0, LANES - z.shape[1])))
    rates = jnp.abs(jnp.linspace(math.log(HY_DECAY_TARGET) / HY_LONG_PCT,
                                 math.log(HY_DECAY_TARGET) / HY_SHORT_PCT, D, dtype=F32))
    return z, jnp.exp(-t[:, None] * rates)


def _pad_to(a, shape):
    return jnp.pad(a, [(0, s - d) for d, s in zip(a.shape, shape)])


def kernel(x_prompt, x_sample, cache_k, cache_v, c, c_ctx, mod_w, mod_b, norm1_g, norm2_g, cv_w_in, cv_b_in, cv_w_dw, cv_b_dw, cv_ln_g, cv_ln_b, cv_w_out, cv_b_out, at_w_qkv, at_w_o, at_q_norm, at_k_norm, hy_w_in, hy_b_in, hy_w_short, hy_b_short, hy_f_w1, hy_f_b1, hy_f_freq1, hy_f_w2, hy_f_b2, hy_f_freq2, hy_f_w3, hy_skip, hy_w_out, hy_b_out, moe_router, moe_w_gate, moe_w_up, moe_w_down, final_g):
    b_ctx, l_ctx, _ = x_prompt.shape
    b_lat, l_lat, _ = x_sample.shape
    kinds = [
        dict(B=b_ctx, L=l_ctx, lat=False, rowfn=lambda nb: (lambda i: 0)),
        dict(B=b_lat, L=l_lat, lat=True, rowfn=lambda nb: (lambda i: 1 + i // nb)),
    ]
    for kd in kinds:
        kd["cap"] = EC_CAPACITY_FACTOR * kd["L"] // N_EXPERTS
        kd["eg"] = min(N_EXPERTS, GATHER_ROWS // kd["cap"])
        kd["bt"] = 1 if kd["lat"] else max(1, DISPATCH_TOKENS // kd["L"])
    xs = [x_prompt.reshape(b_ctx * l_ctx, D), x_sample.reshape(b_lat * l_lat, D)]

    cond = jnp.concatenate([c_ctx[None, :], c, jnp.zeros((MOD_ROWS - 1 - b_lat, D), F32)], axis=0)
    mod = _mod_all(cond, mod_w, mod_b)

    row = lambda v: v.reshape(1, -1)
    final_row = row(final_g)
    new_k = new_v = None

    for i in range(DEPTH):
        mixer, j = i % N_MIXERS, i // N_MIXERS
        mi = mod[i].reshape(MOD_ROWS * 6, 1, D)
        g1 = row(norm1_g[i])
        g2 = row(norm2_g[i])
        wr = jnp.concatenate(_split_bf16(moe_router[i].T), axis=0)
        h2s, affs = [None, None], [None, None]
        if mixer == 0:
            w_in = cv_w_in[j].astype(BF16)
            w_out = cv_w_out[j].astype(BF16)
            dft, tap_tables = _conv_dft_tables(TOKEN_TILE)
            spec = _conv_spectrum(_pad_to(cv_w_dw[j], (tap_tables[0].shape[1], D)), tap_tables)
            for n, kd in enumerate(kinds):
                xs[n], h2s[n], affs[n] = _conv_mixer(
                    kd, xs[n], g1, mi, w_in, row(cv_b_in[j]), dft, spec, row(cv_b_dw[j]),
                    row(cv_ln_g[j]), row(cv_ln_b[j]), w_out, row(cv_b_out[j]), g2, wr)
        elif mixer == 1:
            w_qkv = at_w_qkv[j].astype(BF16)
            w_o = at_w_o[j].astype(BF16)
            qg, kg = row(at_q_norm[j]), row(at_k_norm[j])
            for n, kd in enumerate(kinds):
                if kd["lat"]:
                    cos, sin = _rope_tables(kd["L"])
                    q, k, v = _proj_qkv(kd, xs[n], g1, mi, w_qkv, qg, kg, cos, sin)
                    ck = cache_k[:, j].reshape(kd["B"], -1, NK)
                    cv = cache_v[:, j].reshape(kd["B"], -1, NK)
                    k_all = jnp.concatenate([ck, k.reshape(kd["B"], kd["L"], NK)], axis=1).astype(BF16)
                    v_all = jnp.concatenate([cv, v.reshape(kd["B"], kd["L"], NK)], axis=1).astype(BF16)
                else:
                    q, k, v = _proj_qkv(kd, xs[n], g1, mi, w_qkv, qg, kg, None, None)
                    new_k = k.reshape(kd["B"], 1, kd["L"], N_KV_HEADS, HEAD_DIM)
                    new_v = v.reshape(kd["B"], 1, kd["L"], N_KV_HEADS, HEAD_DIM)
                    k_all = k.reshape(kd["B"], kd["L"], NK).astype(BF16)
                    v_all = v.reshape(kd["B"], kd["L"], NK).astype(BF16)
                v_heads = v_all.reshape(kd["B"], -1, N_KV_HEADS, HEAD_DIM)
                v_ones = jnp.concatenate([v_heads, jnp.ones_like(v_heads)], axis=-1).reshape(kd["B"], -1, 2 * NK)
                xs[n], h2s[n], affs[n] = _attention(kd, q, k_all, v_ones, w_o, xs[n], mi, g2, wr)
        else:
            w_in = hy_w_in[j].astype(BF16)
            w_out = hy_w_out[j].astype(BF16)
            hp = LANES
            w1 = _pad_to(hy_f_w1[j], (LANES, hp))
            w2 = _pad_to(hy_f_w2[j], (hp, hp))
            w3 = _pad_to(hy_f_w3[j], (hp, 2 * D))
            b1, fr1 = _pad_to(row(hy_f_b1[j]), (1, hp)), _pad_to(row(hy_f_freq1[j]), (1, hp))
            b2, fr2 = _pad_to(row(hy_f_b2[j]), (1, hp)), _pad_to(row(hy_f_freq2[j]), (1, hp))
            for n, kd in enumerate(kinds):
                L = kd["L"]
                zfeat, decay = _filter_features(L)
                fc, fs, fs_fwd, fs_inv = _dft_tables(L)
                *taps, nyq = _hy_filter(L, zfeat, w1, b1, fr1, w2, b2, fr2, w3, decay)
                fc_split = _split_bf16(fc)
                spec = _hy_spectrum(L, fc_split, _split_bf16(fs), taps, nyq)
                ub, x0 = _hy_pre(kd, xs[n], g1, mi, w_in, row(hy_b_in[j]), hy_w_short[j], row(hy_b_short[j]))
                mats = (fc_split[0], fs_fwd.astype(BF16), fs_inv.astype(BF16))
                a = _hy_longconv(kd, ub, mats, spec, x0, row(hy_skip[j]))
                xs[n], h2s[n], affs[n] = _hy_out(kd, a, w_out, row(hy_b_out[j]), xs[n], mi, g2, wr)

        slots, xes, gs = [], [], []
        for n, kd in enumerate(kinds):
            slot = _route(affs[n].reshape(kd["B"] * N_EXPERTS, kd["L"]), kd["cap"])
            ng = N_EXPERTS // kd["eg"]
            slot3 = slot.reshape(kd["B"] * ng, kd["eg"], kd["L"])
            aff3 = affs[n].reshape(kd["B"] * ng, kd["eg"], kd["L"])
            xe, gsel = _gather(kd, slot3, aff3, h2s[n])
            slots.append(slot3)
            xes.append(xe)
            gs.append(gsel)
        yes = _ffn(xes[0], xes[1], gs[0], gs[1], moe_w_gate, moe_w_up, moe_w_down, i)
        for n, kd in enumerate(kinds):
            xs[n] = _scatter(kd, slots[n], yes[n], xs[n], mi, final_row, final=(i == DEPTH - 1))

    y_prompt = xs[0].reshape(b_ctx, l_ctx, D)
    y_sample = xs[1].reshape(b_lat, l_lat, D)
    return (y_prompt, y_sample, new_k, new_v)
```

```python
import functools
import math

import jax
import jax.numpy as jnp
from jax import lax
from jax.experimental import pallas as pl
from jax.experimental.pallas import tpu as pltpu

F32 = jnp.float32
BF16 = jnp.bfloat16
HIGHEST = lax.Precision.HIGHEST

D = 1024
DEPTH = 4
GRID_W = 64
N_MIXERS = 3
HEAD_DIM = 128
N_HEADS = 8
N_KV_HEADS = 2
KV_GROUP = N_HEADS // N_KV_HEADS
NQ = N_HEADS * HEAD_DIM
NK = N_KV_HEADS * HEAD_DIM
ROPE_AXIS_DIM = HEAD_DIM // 2
ROPE_THETA = 10000.0
CONV_WIDTH = 31
CONV_PAD = CONV_WIDTH // 2
SHORT_WIDTH = 3
HY_BANDS = 16
HY_DECAY_TARGET = 1e-2
HY_SHORT_PCT = 0.3
HY_LONG_PCT = 1.5
N_EXPERTS = 16
EXPERT_FF = 1024
EC_CAPACITY_FACTOR = 2
NORM_EPS = 1e-6

LANES = 128
SUBLANES = 8
VMEM_LIMIT_BYTES = 56 * 1024 * 1024

TOKEN_TILE = 256
HALO = 16
CONV_ROW_CHUNK = 32
MOD_ROWS = 8
GATHER_ROWS = 1024
DISPATCH_TOKENS = 1024
FFN_F_TILE = 512
CUMSUM_CHUNK = 256
ATTN_STACK_MAX_KEYS = 512


def _cparams(*sem):
    return pltpu.CompilerParams(dimension_semantics=sem, vmem_limit_bytes=VMEM_LIMIT_BYTES)


def _norm_mod(x, g, sh, sc):
    y = x * lax.rsqrt(jnp.mean(x * x, axis=-1, keepdims=True) + NORM_EPS)
    return (y * g) * (1.0 + sc) + sh


def _split_bf16(a):
    hi = a.astype(BF16)
    return hi, (a - hi.astype(F32)).astype(BF16)


def _dot3(a_hi, a_lo, b_hi, b_lo):
    return (jnp.dot(a_hi, b_hi, preferred_element_type=F32) + jnp.dot(a_lo, b_hi, preferred_element_type=F32)
            + jnp.dot(a_hi, b_lo, preferred_element_type=F32))


def _mod_spec(rowfn, j):
    return pl.BlockSpec((1, 1, D), lambda *idx: (rowfn(*idx) * 6 + j, 0, 0))


def _row_spec(n):
    return pl.BlockSpec((1, n), lambda *idx: (0, 0))


def _full_spec(shape):
    nd = len(shape)
    return pl.BlockSpec(shape, lambda *idx: (0,) * nd)


def _tile_spec(tm, n=D):
    return pl.BlockSpec((tm, n), lambda i: (i, 0))


def _halo_specs(rows, tm, n=D):
    hb = tm // HALO
    last = rows // HALO - 1
    return (pl.BlockSpec((HALO, n), lambda i: (jnp.maximum(i * hb - 1, 0), 0)),
            pl.BlockSpec((HALO, n), lambda i: (jnp.minimum((i + 1) * hb, last), 0)))


def _mod_kernel(c_ref, w_ref, b_ref, o_ref):
    cv = c_ref[...]
    s_hi, s_lo = _split_bf16(cv * jax.nn.sigmoid(cv))
    w_hi, w_lo = _split_bf16(w_ref[0])
    o_ref[0] = _dot3(s_hi, s_lo, w_hi, w_lo) + b_ref[0]


def _mod_all(cond, mod_w, mod_b):
    tn = 1536
    n = 6 * D
    return pl.pallas_call(
        _mod_kernel,
        grid=(DEPTH, n // tn),
        in_specs=[
            pl.BlockSpec((MOD_ROWS, D), lambda i, j: (0, 0)),
            pl.BlockSpec((1, D, tn), lambda i, j: (i, 0, j)),
            pl.BlockSpec((1, 1, tn), lambda i, j: (i, 0, j)),
        ],
        out_specs=pl.BlockSpec((1, MOD_ROWS, tn), lambda i, j: (i, 0, j)),
        out_shape=jax.ShapeDtypeStruct((DEPTH, MOD_ROWS, n), F32),
        compiler_params=_cparams("parallel", "parallel"),
        name="mod_all",
    )(cond, mod_w, mod_b.reshape(DEPTH, 1, n))


def _router_tail(x_new, g2_ref, sh2_ref, sc2_ref, wr_ref, h_ref, aff_ref):
    h = _norm_mod(x_new, g2_ref[...], sh2_ref[0], sc2_ref[0])
    h_hi, h_lo = _split_bf16(h)
    h_ref[...] = h_hi
    nt = (((1,), (1,)), ((), ()))
    by_hi = lax.dot_general(wr_ref[...], h_hi, nt, preferred_element_type=F32)
    by_lo = lax.dot_general(wr_ref[0:N_EXPERTS, :], h_lo, nt, preferred_element_type=F32)
    logits = by_hi[0:N_EXPERTS] + by_hi[N_EXPERTS:] + by_lo
    e = jnp.exp(logits - jnp.max(logits, axis=0, keepdims=True))
    aff_ref[0] = e / jnp.sum(e, axis=0, keepdims=True)


def _router_in_specs(rowfn):
    return [_row_spec(D), _mod_spec(rowfn, 3), _mod_spec(rowfn, 4), _full_spec((2 * N_EXPERTS, D))]


def _router_out(kind, tm):
    nb = kind["L"] // tm
    rows = kind["B"] * kind["L"]
    specs = [_tile_spec(tm), pl.BlockSpec((1, N_EXPERTS, tm), lambda i: (i // nb, 0, i % nb))]
    shapes = [jax.ShapeDtypeStruct((rows, D), BF16),
              jax.ShapeDtypeStruct((kind["B"], N_EXPERTS, kind["L"]), F32)]
    return specs, shapes


def _project_with_halo(xp_ref, xc_ref, xn_ref, g_ref, sh_ref, sc_ref, w_ref, b_ref, hs_ref, tm):
    g, sh, sc = g_ref[...], sh_ref[0], sc_ref[0]
    hs_ref[0:HALO, :] = _norm_mod(xp_ref[...], g, sh, sc).astype(BF16)
    hs_ref[HALO:HALO + tm, :] = _norm_mod(xc_ref[...], g, sh, sc).astype(BF16)
    hs_ref[HALO + tm:2 * HALO + tm, :] = _norm_mod(xn_ref[...], g, sh, sc).astype(BF16)
    return jnp.dot(hs_ref[...], w_ref[...], preferred_element_type=F32) + b_ref[...]


def _zero_outside_sequence(zb_ref, j, nb, tm):
    @pl.when(j == 0)
    def _():
        zb_ref[0:HALO, :] = jnp.zeros((HALO, zb_ref.shape[1]), F32)

    @pl.when(j == nb - 1)
    def _():
        zb_ref[HALO + tm:2 * HALO + tm, :] = jnp.zeros((HALO, zb_ref.shape[1]), F32)


def _conv_kernel(xp_ref, xc_ref, xn_ref, g1_ref, sh1_ref, sc1_ref, win_ref, bin_ref,
                 fwd_hi_ref, fwd_lo_ref, inv_hi_ref, inv_lo_ref, kra_ref, krb_ref, ki_ref, bdw_ref,
                 lg_ref, lb_ref, wo_ref, bo_ref, gate_ref, g2_ref, sh2_ref, sc2_ref, wr_ref,
                 o_ref, h_ref, aff_ref, hs_ref, *, nb, tm):
    j = pl.program_id(0) % nb
    n = tm + 2 * HALO
    y = _project_with_halo(xp_ref, xc_ref, xn_ref, g1_ref, sh1_ref, sc1_ref, win_ref, bin_ref, hs_ref, tm)
    z = y[:, :D] * jax.nn.sigmoid(y[:, D:])
    row = lax.broadcasted_iota(jnp.int32, (n, 1), 0)
    first_valid = jnp.where(j > 0, 0, HALO)
    end_valid = jnp.where(j < nb - 1, n, HALO + tm)
    z = jnp.where((row >= first_valid) & (row < end_valid), z, 0.0)
    zb = z.astype(BF16)
    u = (jnp.dot(fwd_hi_ref[...], zb, preferred_element_type=F32)
         + jnp.dot(fwd_lo_ref[...], zb, preferred_element_type=F32))
    ur, ui = u[:n // 2], u[n // 2:]
    ki = ki_ref[...]
    yr = ur * kra_ref[...] - ui * ki
    yi = ur * ki + ui * krb_ref[...]
    yb = jnp.concatenate([yr, yi], axis=0).astype(BF16)
    acc = (jnp.dot(inv_hi_ref[...], yb, preferred_element_type=F32)
           + jnp.dot(inv_lo_ref[...], yb, preferred_element_type=F32) + bdw_ref[...])
    xc = acc - jnp.mean(acc, axis=-1, keepdims=True)
    yn = xc * lax.rsqrt(jnp.mean(xc * xc, axis=-1, keepdims=True) + NORM_EPS)
    yn = yn * lg_ref[...] + lb_ref[...]
    act = (yn * jax.nn.sigmoid(yn)).astype(BF16)
    out = jnp.dot(act, wo_ref[...], preferred_element_type=F32) + bo_ref[...]
    x_new = xc_ref[...] + gate_ref[0] * out
    o_ref[...] = x_new
    _router_tail(x_new, g2_ref, sh2_ref, sc2_ref, wr_ref, h_ref, aff_ref)


def _conv_spectrum_kernel(cw_ref, sw_ref, sgn_ref, w_ref, kra_ref, krb_ref, ki_ref, *, n):
    w = w_ref[...]
    kr = jnp.dot(cw_ref[...], w, preferred_element_type=F32, precision=HIGHEST)
    ki = jnp.dot(sw_ref[...], w, preferred_element_type=F32, precision=HIGHEST)
    nyq = jnp.sum(w * sgn_ref[...], axis=0, keepdims=True)
    row = lax.broadcasted_iota(jnp.int32, kr.shape, 0)
    scale = jnp.where(row == 0, 1.0 / n, 2.0 / n)
    kr = kr * scale
    kra_ref[...] = kr
    krb_ref[...] = jnp.where(row == 0, nyq * (1.0 / n), kr)
    ki_ref[...] = ki * scale


def _conv_spectrum(w_dw_padded, tables):
    cw, sw, sgn = tables
    n = 2 * cw.shape[0]
    out = jax.ShapeDtypeStruct((n // 2, D), F32)
    return pl.pallas_call(
        functools.partial(_conv_spectrum_kernel, n=n),
        out_shape=[out, out, out],
        compiler_params=pltpu.CompilerParams(vmem_limit_bytes=VMEM_LIMIT_BYTES),
        name="conv_spectrum",
    )(cw, sw, sgn, w_dw_padded)


def _conv_mixer(kind, x, g1, mi, w_in, b_in, dft, spec, b_dw, ln_g, ln_b, w_out, b_out, g2, wr):
    rows = x.shape[0]
    tm = TOKEN_TILE
    n = tm + 2 * HALO
    nb = kind["L"] // tm
    rowfn = kind["rowfn"](nb)
    prev_spec, next_spec = _halo_specs(rows, tm)
    r_specs, r_shapes = _router_out(kind, tm)
    return pl.pallas_call(
        functools.partial(_conv_kernel, nb=nb, tm=tm),
        grid=(rows // tm,),
        in_specs=[prev_spec, _tile_spec(tm), next_spec,
                  _row_spec(D), _mod_spec(rowfn, 0), _mod_spec(rowfn, 1),
                  _full_spec((D, 2 * D)), _row_spec(2 * D),
                  _full_spec((n, n)), _full_spec((n, n)), _full_spec((tm, n)), _full_spec((tm, n)),
                  _full_spec((n // 2, D)), _full_spec((n // 2, D)), _full_spec((n // 2, D)),
                  _row_spec(D), _row_spec(D), _row_spec(D),
                  _full_spec((D, D)), _row_spec(D), _mod_spec(rowfn, 2)] + _router_in_specs(rowfn),
        out_specs=[_tile_spec(tm)] + r_specs,
        out_shape=[jax.ShapeDtypeStruct((rows, D), F32)] + r_shapes,
        scratch_shapes=[pltpu.VMEM((n, D), BF16)],
        compiler_params=_cparams("parallel"),
        name="conv_mixer",
    )(x, x, x, g1, mi, mi, w_in, b_in, *dft, *spec, b_dw, ln_g, ln_b, w_out, b_out, mi, g2, mi, mi, wr)


def _head_norm(seg, g):
    return seg * lax.rsqrt(jnp.mean(seg * seg, axis=-1, keepdims=True) + NORM_EPS) * g


def _proj_qkv_kernel(x_ref, g_ref, sh_ref, sc_ref, w_ref, qg_ref, kg_ref, *rest, rope):
    if rope:
        cos_ref, sin_ref, q_ref, k_ref, v_ref = rest
    else:
        q_ref, k_ref, v_ref = rest
    h = _norm_mod(x_ref[...], g_ref[...], sh_ref[0], sc_ref[0])
    y = jnp.dot(h.astype(BF16), w_ref[...], preferred_element_type=F32)
    if rope:
        cos = cos_ref[...]
        sin = sin_ref[...]
        lane = lax.broadcasted_iota(jnp.int32, (y.shape[0], HEAD_DIM), 1)
        even = (lane & 1) == 0
    for hd in range(N_HEADS + N_KV_HEADS):
        seg = y[:, hd * HEAD_DIM:(hd + 1) * HEAD_DIM]
        nrm = _head_norm(seg, qg_ref[...] if hd < N_HEADS else kg_ref[...])
        if rope:
            partner = jnp.where(even, pltpu.roll(nrm, HEAD_DIM - 1, 1), pltpu.roll(nrm, 1, 1))
            nrm = nrm * cos + partner * sin
        if hd < N_HEADS:
            q_ref[:, hd * HEAD_DIM:(hd + 1) * HEAD_DIM] = (nrm * HEAD_DIM ** -0.5).astype(BF16)
        else:
            k_ref[:, (hd - N_HEADS) * HEAD_DIM:(hd - N_HEADS + 1) * HEAD_DIM] = nrm
    v_ref[...] = y[:, NQ + NK:]


def _proj_qkv(kind, x, g, mi, w, qg, kg, cos, sin):
    rows = x.shape[0]
    tm = TOKEN_TILE
    nb = kind["L"] // tm
    rowfn = kind["rowfn"](nb)
    rope = cos is not None
    specs = [_tile_spec(tm), _row_spec(D), _mod_spec(rowfn, 0), _mod_spec(rowfn, 1),
             _full_spec((D, NQ + 2 * NK)), _row_spec(HEAD_DIM), _row_spec(HEAD_DIM)]
    args = [x, g, mi, mi, w, qg, kg]
    if rope:
        specs += [pl.BlockSpec((tm, HEAD_DIM), lambda i: (i % nb, 0))] * 2
        args += [cos, sin]
    return pl.pallas_call(
        functools.partial(_proj_qkv_kernel, rope=rope),
        grid=(rows // tm,),
        in_specs=specs,
        out_specs=[_tile_spec(tm, NQ), _tile_spec(tm, NK), _tile_spec(tm, NK)],
        out_shape=[jax.ShapeDtypeStruct((rows, NQ), BF16),
                   jax.ShapeDtypeStruct((rows, NK), F32),
                   jax.ShapeDtypeStruct((rows, NK), F32)],
        compiler_params=_cparams("parallel"),
        name="proj_qkv",
    )(*args)


def _attn_kernel(q_ref, k_ref, v_ref, wo_ref, x_ref, gate_ref, g2_ref, sh2_ref, sc2_ref, wr_ref,
                 o_ref, h_ref, aff_ref, *, stack_heads):
    tq = q_ref.shape[0]
    outs = [None] * N_HEADS
    if stack_heads:
        groups = [list(range(kv * KV_GROUP, (kv + 1) * KV_GROUP)) for kv in range(N_KV_HEADS)]
    else:
        groups = [[hd] for hd in range(N_HEADS)]
    for heads in groups:
        kv = heads[0] // KV_GROUP
        qs = jnp.concatenate([q_ref[:, hd * HEAD_DIM:(hd + 1) * HEAD_DIM] for hd in heads], axis=0)
        kh = k_ref[0, :, kv * HEAD_DIM:(kv + 1) * HEAD_DIM]
        vh = v_ref[0, :, kv * 2 * HEAD_DIM:(kv + 1) * 2 * HEAD_DIM]
        s = lax.dot_general(qs, kh, (((1,), (1,)), ((), ())), preferred_element_type=F32)
        p = jnp.exp((s - jnp.max(s, axis=-1, keepdims=True)).astype(BF16))
        oa = jnp.dot(p, vh, preferred_element_type=F32)
        on = oa[:, :HEAD_DIM] / oa[:, HEAD_DIM:HEAD_DIM + 1]
        for g, hd in enumerate(heads):
            outs[hd] = on[g * tq:(g + 1) * tq, :]
    o = jnp.concatenate(outs, axis=1).astype(BF16)
    y = jnp.dot(o, wo_ref[...], preferred_element_type=F32)
    x_new = x_ref[...] + gate_ref[0] * y
    o_ref[...] = x_new
    _router_tail(x_new, g2_ref, sh2_ref, sc2_ref, wr_ref, h_ref, aff_ref)


def _attention(kind, q, k_all, v_all, w_o, x, mi, g2, wr):
    rows = x.shape[0]
    tq = TOKEN_TILE
    nb = kind["L"] // tq
    rowfn = kind["rowfn"](nb)
    s_len = k_all.shape[1]
    kv_spec = pl.BlockSpec((1, s_len, NK), lambda i: (i // nb, 0, 0))
    v_spec = pl.BlockSpec((1, s_len, 2 * NK), lambda i: (i // nb, 0, 0))
    r_specs, r_shapes = _router_out(kind, tq)
    return pl.pallas_call(
        functools.partial(_attn_kernel, stack_heads=s_len <= ATTN_STACK_MAX_KEYS),
        grid=(rows // tq,),
        in_specs=[_tile_spec(tq), kv_spec, v_spec, _full_spec((D, D)), _tile_spec(tq),
                  _mod_spec(rowfn, 2)] + _router_in_specs(rowfn),
        out_specs=[_tile_spec(tq)] + r_specs,
        out_shape=[jax.ShapeDtypeStruct((rows, D), F32)] + r_shapes,
        compiler_params=_cparams("parallel"),
        name="attention",
    )(q, k_all, v_all, w_o, x, mi, g2, mi, mi, wr)


def _hy_pre_kernel(xp_ref, xc_ref, xn_ref, g1_ref, sh1_ref, sc1_ref, win_ref, bin_ref, w_ref, b_ref,
                   ub_ref, x0_ref, hs_ref, zb_ref, *, nb, tm):
    j = pl.program_id(0) % nb
    zb_ref[...] = _project_with_halo(xp_ref, xc_ref, xn_ref, g1_ref, sh1_ref, sc1_ref, win_ref, bin_ref,
                                     hs_ref, tm)
    _zero_outside_sequence(zb_ref, j, nb, tm)
    rc = CONV_ROW_CHUNK
    for r0 in range(0, tm, rc):
        parts = []
        for part in range(3):
            cs = slice(part * D, (part + 1) * D)
            lo = r0 + HALO - SUBLANES
            nwin = rc + 2 * SUBLANES
            win = zb_ref[lo:lo + nwin, cs]
            z = b_ref[:, cs]
            for k in range(SHORT_WIDTH):
                shift = (SHORT_WIDTH // 2 - k) % nwin
                moved = win if shift == 0 else pltpu.roll(win, shift, 0)
                z = z + moved[SUBLANES:SUBLANES + rc, :] * w_ref[k:k + 1, cs]
            parts.append(z)
        x0_ref[r0:r0 + rc, :] = parts[0]
        ub_ref[r0:r0 + rc, :] = (parts[2] * parts[1]).astype(BF16)


def _hy_pre(kind, x, g1, mi, w_in, b_in, w_short, b_short):
    rows = x.shape[0]
    tm = TOKEN_TILE
    nb = kind["L"] // tm
    rowfn = kind["rowfn"](nb)
    prev_spec, next_spec = _halo_specs(rows, tm)
    return pl.pallas_call(
        functools.partial(_hy_pre_kernel, nb=nb, tm=tm),
        grid=(rows // tm,),
        in_specs=[prev_spec, _tile_spec(tm), next_spec,
                  _row_spec(D), _mod_spec(rowfn, 0), _mod_spec(rowfn, 1),
                  _full_spec((D, 3 * D)), _row_spec(3 * D),
                  _full_spec((SHORT_WIDTH, 3 * D)), _row_spec(3 * D)],
        out_specs=[_tile_spec(tm)] * 2,
        out_shape=[jax.ShapeDtypeStruct((rows, D), BF16), jax.ShapeDtypeStruct((rows, D), F32)],
        scratch_shapes=[pltpu.VMEM((tm + 2 * HALO, D), BF16), pltpu.VMEM((tm + 2 * HALO, 3 * D), F32)],
        compiler_params=_cparams("parallel"),
        name="hy_pre",
    )(x, x, x, g1, mi, mi, w_in, b_in, w_short, b_short)


def _hy_filter_kernel(z_ref, w1_ref, b1_ref, f1_ref, w2_ref, b2_ref, f2_ref, w3f_ref, w3b_ref,
                      dec_ref, ha_hi_ref, ha_lo_ref, hb_hi_ref, hb_lo_ref, nyq_ref, hid_ref):
    @pl.when(pl.program_id(0) == 0)
    def _():
        h1 = jnp.sin(f1_ref[...] * (jnp.dot(z_ref[...], w1_ref[...], preferred_element_type=F32,
                                            precision=HIGHEST) + b1_ref[...]))
        hid_ref[...] = jnp.sin(f2_ref[...] * (jnp.dot(h1, w2_ref[...], preferred_element_type=F32,
                                                      precision=HIGHEST) + b2_ref[...]))

    f = hid_ref[...]
    dec = dec_ref[...]
    hf = jnp.dot(f, w3f_ref[...], preferred_element_type=F32, precision=HIGHEST) * dec
    hb = jnp.dot(f, w3b_ref[...], preferred_element_type=F32, precision=HIGHEST) * dec
    row = lax.broadcasted_iota(jnp.int32, hf.shape, 0)
    hb = jnp.where(row == 0, 0.0, hb)
    ha = hf + hb
    ha_hi_ref[...], ha_lo_ref[...] = _split_bf16(ha)
    hb_hi_ref[...], hb_lo_ref[...] = _split_bf16(hf - hb)
    sign = jnp.where((row & 1) == 0, 1.0, -1.0)
    nyq_ref[...] = jnp.sum(ha * sign, axis=0, keepdims=True)


def _hy_filter(L, zfeat, w1, b1, fr1, w2, b2, fr2, w3, decay):
    tc = 256
    nct = D // tc
    hp = w1.shape[1]
    c_spec = pl.BlockSpec((L, tc), lambda c: (0, c))
    return pl.pallas_call(
        _hy_filter_kernel,
        grid=(nct,),
        in_specs=[
            _full_spec(zfeat.shape), _full_spec(w1.shape), _row_spec(hp), _row_spec(hp),
            _full_spec(w2.shape), _row_spec(hp), _row_spec(hp),
            pl.BlockSpec((hp, tc), lambda c: (0, c)),
            pl.BlockSpec((hp, tc), lambda c: (0, nct + c)),
            c_spec,
        ],
        out_specs=[c_spec] * 4 + [pl.BlockSpec((1, tc), lambda c: (0, c))],
        out_shape=[jax.ShapeDtypeStruct((L, D), BF16)] * 4 + [jax.ShapeDtypeStruct((1, D), F32)],
        scratch_shapes=[pltpu.VMEM((L, hp), F32)],
        compiler_params=_cparams("arbitrary"),
        name="hy_filter",
    )(zfeat, w1, b1, fr1, w2, b2, fr2, w3, w3, decay)


def _hy_spectrum_kernel(fc_hi_ref, fc_lo_ref, fs_hi_ref, fs_lo_ref, ha_hi_ref, ha_lo_ref, hb_hi_ref, hb_lo_ref,
                        nyq_ref, kra_ref, krb_ref, ki_ref, *, L, ft):
    kr = _dot3(fc_hi_ref[...], fc_lo_ref[...], ha_hi_ref[...], ha_lo_ref[...])
    ki = _dot3(fs_hi_ref[...], fs_lo_ref[...], hb_hi_ref[...], hb_lo_ref[...])
    row = lax.broadcasted_iota(jnp.int32, kr.shape, 0) + pl.program_id(1) * ft
    scale = jnp.where(row == 0, 0.5 / L, 1.0 / L)
    kr = kr * scale
    kra_ref[...] = kr
    krb_ref[...] = jnp.where(row == 0, nyq_ref[...] * (0.5 / L), kr)
    ki_ref[...] = ki * scale


def _hy_spectrum(L, fc_split, fs_split, taps, nyq):
    ft = min(L, 256)
    tc = 512
    mat_spec = pl.BlockSpec((ft, L), lambda c, k: (k, 0))
    h_spec = pl.BlockSpec((L, tc), lambda c, k: (0, c))
    o_spec = pl.BlockSpec((ft, tc), lambda c, k: (k, c))
    return pl.pallas_call(
        functools.partial(_hy_spectrum_kernel, L=L, ft=ft),
        grid=(D // tc, L // ft),
        in_specs=[mat_spec] * 4 + [h_spec] * 4 + [pl.BlockSpec((1, tc), lambda c, k: (0, c))],
        out_specs=[o_spec, o_spec, o_spec],
        out_shape=[jax.ShapeDtypeStruct((L, D), F32)] * 3,
        compiler_params=_cparams("parallel", "parallel"),
        name="hy_spectrum",
    )(*fc_split, *fs_split, *taps, nyq)


def _hy_longconv_kernel(ub_ref, fc_ref, fs_ref, gc_ref, gs_ref, kra_ref, krb_ref, ki_ref, x0_ref, skip_ref,
                        o_ref, acc_ref):
    k = pl.program_id(2)

    @pl.when(k == 0)
    def _():
        acc_ref[...] = jnp.zeros_like(acc_ref)

    ub = ub_ref[...]
    ur = jnp.dot(fc_ref[...], ub, preferred_element_type=F32)
    ui = jnp.dot(fs_ref[...], ub, preferred_element_type=F32)
    kra = kra_ref[...]
    krb = krb_ref[...]
    ki = ki_ref[...]
    yr = (ur * kra - ui * ki).astype(BF16)
    yi = (ur * ki + ui * krb).astype(BF16)
    acc_ref[...] += jnp.dot(gc_ref[...], yr, preferred_element_type=F32)
    acc_ref[...] += jnp.dot(gs_ref[...], yi, preferred_element_type=F32)

    @pl.when(k == pl.num_programs(2) - 1)
    def _():
        y = acc_ref[...] + ub_ref[...].astype(F32) * skip_ref[...]
        o_ref[...] = (y * x0_ref[...]).astype(BF16)


def _hy_longconv(kind, ub, mats, spec, x0, skip):
    B, L = kind["B"], kind["L"]
    fc, fs, fst = mats
    kra, krb, ki = spec
    ft = min(L, 512)
    tc = 1024 if L <= 256 else 512
    d_spec = pl.BlockSpec((L, tc), lambda b, c, k: (b, c))
    fwd_spec = pl.BlockSpec((ft, L), lambda b, c, k: (k, 0))
    inv_spec = pl.BlockSpec((L, ft), lambda b, c, k: (0, k))
    k_spec = pl.BlockSpec((ft, tc), lambda b, c, k: (k, c))
    return pl.pallas_call(
        _hy_longconv_kernel,
        grid=(B, D // tc, L // ft),
        in_specs=[d_spec, fwd_spec, fwd_spec, inv_spec, inv_spec, k_spec, k_spec, k_spec, d_spec,
                  pl.BlockSpec((1, tc), lambda b, c, k: (0, c))],
        out_specs=d_spec,
        out_shape=jax.ShapeDtypeStruct((B * L, D), BF16),
        scratch_shapes=[pltpu.VMEM((L, tc), F32)],
        compiler_params=_cparams("parallel", "parallel", "arbitrary"),
        name="hy_longconv",
    )(ub, fc, fs, fc, fst, kra, krb, ki, x0, skip)


def _hy_out_kernel(a_ref, w_ref, b_ref, x_ref, gate_ref,
                   g2_ref, sh2_ref, sc2_ref, wr_ref, o_ref, h_ref, aff_ref):
    y = jnp.dot(a_ref[...], w_ref[...], preferred_element_type=F32) + b_ref[...]
    x_new = x_ref[...] + gate_ref[0] * y
    o_ref[...] = x_new
    _router_tail(x_new, g2_ref, sh2_ref, sc2_ref, wr_ref, h_ref, aff_ref)


def _hy_out(kind, a, w, b, x, mi, g2, wr):
    rows = x.shape[0]
    tm = TOKEN_TILE
    rowfn = kind["rowfn"](kind["L"] // tm)
    t_spec = _tile_spec(tm)
    r_specs, r_shapes = _router_out(kind, tm)
    return pl.pallas_call(
        _hy_out_kernel,
        grid=(rows // tm,),
        in_specs=[t_spec, _full_spec((D, D)), _row_spec(D), t_spec,
                  _mod_spec(rowfn, 2)] + _router_in_specs(rowfn),
        out_specs=[t_spec] + r_specs,
        out_shape=[jax.ShapeDtypeStruct((rows, D), F32)] + r_shapes,
        compiler_params=_cparams("parallel"),
        name="hy_out",
    )(a, w, b, x, mi, g2, mi, mi, wr)


def _lane_cumsum(mask_f32, tri):
    rows, n = mask_f32.shape
    run = jnp.zeros((rows, 1), F32)
    pieces = []
    for c0 in range(0, n, CUMSUM_CHUNK):
        chunk = mask_f32[:, c0:c0 + CUMSUM_CHUNK]
        pieces.append(jnp.dot(chunk.astype(BF16), tri, preferred_element_type=F32) + run)
        run = run + jnp.sum(chunk, axis=1, keepdims=True)
    return jnp.concatenate(pieces, axis=1) if len(pieces) > 1 else pieces[0]


def _route_kernel(aff_ref, slot_ref, *, cap):
    bits = pltpu.bitcast(aff_ref[...], jnp.int32)
    rows = bits.shape[0]
    thr = jnp.zeros((rows, 1), jnp.int32)
    capf = float(cap)
    for bit in range(30, -1, -1):
        cand = thr | (1 << bit)
        cnt = jnp.sum(jnp.where(bits >= cand, 1.0, 0.0), axis=1, keepdims=True)
        thr = jnp.where(cnt >= capf, cand, thr)
    r_i = lax.broadcasted_iota(jnp.int32, (CUMSUM_CHUNK, CUMSUM_CHUNK), 0)
    c_i = lax.broadcasted_iota(jnp.int32, (CUMSUM_CHUNK, CUMSUM_CHUNK), 1)
    tri = jnp.where(r_i <= c_i, 1.0, 0.0).astype(BF16)
    gt = jnp.where(bits > thr, 1.0, 0.0)
    eq = jnp.where(bits == thr, 1.0, 0.0)
    need = capf - jnp.sum(gt, axis=1, keepdims=True)
    eq_rank = _lane_cumsum(eq, tri)
    sel = gt + eq * jnp.where(eq_rank <= need, 1.0, 0.0)
    pos = _lane_cumsum(sel, tri)
    slot_ref[...] = jnp.where(sel > 0.5, pos - 1.0, -1.0).astype(jnp.int32)


def _route(aff2d, cap):
    return pl.pallas_call(
        functools.partial(_route_kernel, cap=cap),
        out_shape=jax.ShapeDtypeStruct(aff2d.shape, jnp.int32),
        compiler_params=pltpu.CompilerParams(vmem_limit_bytes=VMEM_LIMIT_BYTES),
        name="route",
    )(aff2d)


def _dispatch_masks(slot, cap):
    eg, n = slot.shape
    iota = lax.broadcasted_iota(jnp.int32, (cap, n), 0)
    return [iota == slot[e:e + 1, :] for e in range(eg)]


def _onehot(masks):
    return jnp.concatenate([jnp.where(m, 1.0, 0.0).astype(BF16) for m in masks], axis=0)


def _gather_kernel(slot_ref, aff_ref, h_ref, xe_ref, g_ref, *, cap, eg, bt, seq):
    for b in range(bt):
        masks = _dispatch_masks(slot_ref[b], cap)
        aff = aff_ref[b]
        xe = jnp.dot(_onehot(masks), h_ref[b * seq:(b + 1) * seq, :], preferred_element_type=F32)
        xe_ref[:, b * cap:(b + 1) * cap, :] = xe.reshape(eg, cap, D).astype(BF16)
        for e in range(eg):
            gsel = jnp.sum(jnp.where(masks[e], aff[e:e + 1, :], 0.0), axis=1, keepdims=True)
            g_ref[e, b * cap:(b + 1) * cap, :] = jnp.broadcast_to(gsel, (cap, LANES))


def _gather(kind, slot3, aff3, h2):
    B, L, cap, eg, bt = kind["B"], kind["L"], kind["cap"], kind["eg"], kind["bt"]
    ng = N_EXPERTS // eg
    assert bt == 1 or ng == 1
    r_spec = pl.BlockSpec((bt, eg, L), lambda b, e: (b * ng + e, 0, 0))
    return pl.pallas_call(
        functools.partial(_gather_kernel, cap=cap, eg=eg, bt=bt, seq=L),
        grid=(B // bt, ng),
        in_specs=[r_spec, r_spec, pl.BlockSpec((bt * L, D), lambda b, e: (b, 0))],
        out_specs=[pl.BlockSpec((eg, bt * cap, D), lambda b, e: (e, b, 0)),
                   pl.BlockSpec((eg, bt * cap, LANES), lambda b, e: (e, b, 0))],
        out_shape=[jax.ShapeDtypeStruct((N_EXPERTS, B * cap, D), BF16),
                   jax.ShapeDtypeStruct((N_EXPERTS, B * cap, LANES), F32)],
        compiler_params=_cparams("parallel", "parallel"),
        name="moe_gather",
    )(slot3, aff3, h2)


def _ffn_kernel(xa_ref, xb_ref, ga_ref, gb_ref, wg_ref, wu_ref, wd_ref, ya_ref, yb_ref, acc_ref):
    f = pl.program_id(1)

    @pl.when(f == 0)
    def _():
        acc_ref[...] = jnp.zeros_like(acc_ref)

    wg = wg_ref[0, 0].astype(BF16)
    wu = wu_ref[0, 0].astype(BF16)
    wd = wd_ref[0, 0].astype(BF16)
    na = xa_ref.shape[1]
    for x_ref, r0 in ((xa_ref, 0), (xb_ref, na)):
        x = x_ref[0]
        a = jnp.dot(x, wg, preferred_element_type=F32)
        u = jnp.dot(x, wu, preferred_element_type=F32)
        mid = (a * jax.nn.sigmoid(a) * u).astype(BF16)
        acc_ref[r0:r0 + x.shape[0], :] += jnp.dot(mid, wd, preferred_element_type=F32)

    @pl.when(f == pl.num_programs(1) - 1)
    def _():
        ya_ref[0] = (acc_ref[0:na, :] * ga_ref[0][:, 0:1]).astype(BF16)
        yb_ref[0] = (acc_ref[na:, :] * gb_ref[0][:, 0:1]).astype(BF16)


def _ffn(xe_a, xe_b, g_a, g_b, w_gate, w_up, w_down, layer):
    na, nb_ = xe_a.shape[1], xe_b.shape[1]
    tf = FFN_F_TILE
    return pl.pallas_call(
        _ffn_kernel,
        grid=(N_EXPERTS, EXPERT_FF // tf),
        in_specs=[
            pl.BlockSpec((1, na, D), lambda e, f: (e, 0, 0)),
            pl.BlockSpec((1, nb_, D), lambda e, f: (e, 0, 0)),
            pl.BlockSpec((1, na, LANES), lambda e, f: (e, 0, 0)),
            pl.BlockSpec((1, nb_, LANES), lambda e, f: (e, 0, 0)),
            pl.BlockSpec((1, 1, D, tf), lambda e, f: (layer, e, 0, f)),
            pl.BlockSpec((1, 1, D, tf), lambda e, f: (layer, e, 0, f)),
            pl.BlockSpec((1, 1, tf, D), lambda e, f: (layer, e, f, 0)),
        ],
        out_specs=[pl.BlockSpec((1, na, D), lambda e, f: (e, 0, 0)),
                   pl.BlockSpec((1, nb_, D), lambda e, f: (e, 0, 0))],
        out_shape=[jax.ShapeDtypeStruct((N_EXPERTS, na, D), BF16),
                   jax.ShapeDtypeStruct((N_EXPERTS, nb_, D), BF16)],
        scratch_shapes=[pltpu.VMEM((na + nb_, D), F32)],
        compiler_params=_cparams("parallel", "arbitrary"),
        name="moe_ffn",
    )(xe_a, xe_b, g_a, g_b, w_gate, w_up, w_down)


def _scatter_kernel(slot_ref, ye_ref, x_ref, gate_ref, fg_ref, o_ref, acc_ref, *, cap, eg, bt, lt, final):
    e = pl.program_id(2)

    @pl.when(e == 0)
    def _():
        acc_ref[...] = jnp.zeros_like(acc_ref)

    tn = (((0,), (0,)), ((), ()))
    for b in range(bt):
        onehot = _onehot(_dispatch_masks(slot_ref[b], cap))
        ye = ye_ref[:, b * cap:(b + 1) * cap, :].reshape(eg * cap, D)
        acc_ref[b * lt:(b + 1) * lt, :] += lax.dot_general(onehot, ye, tn, preferred_element_type=F32)

    @pl.when(e == pl.num_programs(2) - 1)
    def _():
        x = x_ref[...] + gate_ref[0] * acc_ref[...]
        if final:
            x = x * lax.rsqrt(jnp.mean(x * x, axis=-1, keepdims=True) + NORM_EPS) * fg_ref[...]
        o_ref[...] = x


def _scatter(kind, slot3, ye, x, mi, final_g, final):
    B, L, cap, eg, bt = kind["B"], kind["L"], kind["cap"], kind["eg"], kind["bt"]
    ng = N_EXPERTS // eg
    lt = min(L, DISPATCH_TOKENS)
    nl = L // lt
    assert bt == 1 or (ng == 1 and nl == 1)
    rowfn = lambda b, l, e: kind["rowfn"](1)(b)
    x_spec = pl.BlockSpec((bt * lt, D), lambda b, l, e: (b * nl + l, 0))
    return pl.pallas_call(
        functools.partial(_scatter_kernel, cap=cap, eg=eg, bt=bt, lt=lt, final=final),
        grid=(B // bt, nl, ng),
        in_specs=[pl.BlockSpec((bt, eg, lt), lambda b, l, e: (b * ng + e, 0, l)),
                  pl.BlockSpec((eg, bt * cap, D), lambda b, l, e: (e, b, 0)),
                  x_spec, _mod_spec(rowfn, 5), _row_spec(D)],
        out_specs=x_spec,
        out_shape=jax.ShapeDtypeStruct((B * L, D), F32),
        scratch_shapes=[pltpu.VMEM((bt * lt, D), F32)],
        compiler_params=_cparams("parallel", "parallel", "arbitrary"),
        name="moe_scatter",
    )(slot3, ye, x, mi, final_g)


def _rope_tables(L):
    n_rows = L // GRID_W
    rows = jnp.repeat(jnp.arange(n_rows), GRID_W).astype(F32)
    cols = jnp.tile(jnp.arange(GRID_W), n_rows).astype(F32)
    inv = ROPE_THETA ** (-jnp.arange(0, ROPE_AXIS_DIM, 2, dtype=F32) / ROPE_AXIS_DIM)
    ang = jnp.concatenate([rows[:, None] * inv, cols[:, None] * inv], axis=-1)
    cos = jnp.repeat(jnp.cos(ang), 2, axis=-1)
    sin = jnp.repeat(jnp.sin(ang), 2, axis=-1)
    sign = jnp.tile(jnp.array([-1.0, 1.0], F32), HEAD_DIM // 2)
    return cos, sin * sign


def _dft_tables(L):
    r = min(L, 64)
    s = jnp.arange(L, dtype=jnp.int32)

    def small(f):
        ang = ((f[:, None] * s[None, :]) % (2 * L)).astype(F32) * (math.pi / L)
        return jnp.cos(ang), jnp.sin(ang)

    c0, s0 = small(jnp.arange(r, dtype=jnp.int32))
    c1, s1 = small(jnp.arange(L // r, dtype=jnp.int32) * r)
    fc = (c1[:, None, :] * c0[None] - s1[:, None, :] * s0[None]).reshape(L, L)
    fs = -(s1[:, None, :] * c0[None] + c1[:, None, :] * s0[None]).reshape(L, L)
    sign = jnp.where(s % 2 == 0, 1.0, -1.0).astype(F32)
    fs_fwd = jnp.where(s[:, None] == 0, sign[None, :], fs)
    fs_inv = jnp.where(s[None, :] == 0, sign[:, None], fs)
    return fc, fs, fs_fwd, fs_inv


def _conv_dft_tables(tm):
    n = tm + 2 * HALO
    f = jnp.arange(n // 2, dtype=jnp.int32)
    r = jnp.arange(n, dtype=jnp.int32)

    def cs(pos):
        ang = ((f[:, None] * pos[None, :]) % n).astype(F32) * (2.0 * math.pi / n)
        return jnp.cos(ang), -jnp.sin(ang)

    fc, fs = cs(r)
    fs = fs.at[0, :].set(jnp.where(r % 2 == 0, 1.0, -1.0).astype(F32))
    fwd = jnp.concatenate([fc, fs], axis=0)
    inv = jnp.concatenate([fc[:, HALO:HALO + tm].T, fs[:, HALO:HALO + tm].T], axis=1)
    k = jnp.arange(2 * SUBLANES * ((CONV_WIDTH + 2 * SUBLANES - 1) // (2 * SUBLANES)), dtype=jnp.int32)
    lag = (CONV_PAD - k) % n
    cw, sw = cs(lag)
    live = (k < CONV_WIDTH)[None, :]
    cw, sw = jnp.where(live, cw, 0.0), jnp.where(live, sw, 0.0)
    sgn = jnp.where((CONV_PAD - k) % 2 == 0, 1.0, -1.0).astype(F32)[:, None]
    return _split_bf16(fwd) + _split_bf16(inv), (cw, sw, sgn)


def _filter_features(L):
    t = jnp.arange(L, dtype=F32) / L
    bands = jnp.arange(1, HY_BANDS + 1, dtype=F32)
    ph = 2.0 * math.pi * t[:, None] * bands
    z = jnp.concatenate([t[:, None], jnp.sin(ph), jnp.cos(ph)], axis=-1)
    z = jnp.pad(z, ((0, 0), (0, LANES - z.shape[1])))
    rates = jnp.abs(jnp.linspace(math.log(HY_DECAY_TARGET) / HY_LONG_PCT,
                                 math.log(HY_DECAY_TARGET) / HY_SHORT_PCT, D, dtype=F32))
    return z, jnp.exp(-t[:, None] * rates)


def _pad_to(a, shape):
    return jnp.pad(a, [(0, s - d) for d, s in zip(a.shape, shape)])


def kernel(x_prompt, x_sample, cache_k, cache_v, c, c_ctx, mod_w, mod_b, norm1_g, norm2_g, cv_w_in, cv_b_in, cv_w_dw, cv_b_dw, cv_ln_g, cv_ln_b, cv_w_out, cv_b_out, at_w_qkv, at_w_o, at_q_norm, at_k_norm, hy_w_in, hy_b_in, hy_w_short, hy_b_short, hy_f_w1, hy_f_b1, hy_f_freq1, hy_f_w2, hy_f_b2, hy_f_freq2, hy_f_w3, hy_skip, hy_w_out, hy_b_out, moe_router, moe_w_gate, moe_w_up, moe_w_down, final_g):
    b_ctx, l_ctx, _ = x_prompt.shape
    b_lat, l_lat, _ = x_sample.shape
    kinds = [
        dict(B=b_ctx, L=l_ctx, lat=False, rowfn=lambda nb: (lambda i: 0)),
        dict(B=b_lat, L=l_lat, lat=True, rowfn=lambda nb: (lambda i: 1 + i // nb)),
    ]
    for kd in kinds:
        kd["cap"] = EC_CAPACITY_FACTOR * kd["L"] // N_EXPERTS
        kd["eg"] = min(N_EXPERTS, GATHER_ROWS // kd["cap"])
        kd["bt"] = 1 if kd["lat"] else max(1, DISPATCH_TOKENS // kd["L"])
    xs = [x_prompt.reshape(b_ctx * l_ctx, D), x_sample.reshape(b_lat * l_lat, D)]

    cond = jnp.concatenate([c_ctx[None, :], c, jnp.zeros((MOD_ROWS - 1 - b_lat, D), F32)], axis=0)
    mod = _mod_all(cond, mod_w, mod_b)

    row = lambda v: v.reshape(1, -1)
    final_row = row(final_g)
    new_k = new_v = None

    for i in range(DEPTH):
        mixer, j = i % N_MIXERS, i // N_MIXERS
        mi = mod[i].reshape(MOD_ROWS * 6, 1, D)
        g1 = row(norm1_g[i])
        g2 = row(norm2_g[i])
        wr = jnp.concatenate(_split_bf16(moe_router[i].T), axis=0)
        h2s, affs = [None, None], [None, None]
        if mixer == 0:
            w_in = cv_w_in[j].astype(BF16)
            w_out = cv_w_out[j].astype(BF16)
            dft, tap_tables = _conv_dft_tables(TOKEN_TILE)
            spec = _conv_spectrum(_pad_to(cv_w_dw[j], (tap_tables[0].shape[1], D)), tap_tables)
            for n, kd in enumerate(kinds):
                xs[n], h2s[n], affs[n] = _conv_mixer(
                    kd, xs[n], g1, mi, w_in, row(cv_b_in[j]), dft, spec, row(cv_b_dw[j]),
                    row(cv_ln_g[j]), row(cv_ln_b[j]), w_out, row(cv_b_out[j]), g2, wr)
        elif mixer == 1:
            w_qkv = at_w_qkv[j].astype(BF16)
            w_o = at_w_o[j].astype(BF16)
            qg, kg = row(at_q_norm[j]), row(at_k_norm[j])
            for n, kd in enumerate(kinds):
                if kd["lat"]:
                    cos, sin = _rope_tables(kd["L"])
                    q, k, v = _proj_qkv(kd, xs[n], g1, mi, w_qkv, qg, kg, cos, sin)
                    ck = cache_k[:, j].reshape(kd["B"], -1, NK)
                    cv = cache_v[:, j].reshape(kd["B"], -1, NK)
                    k_all = jnp.concatenate([ck, k.reshape(kd["B"], kd["L"], NK)], axis=1).astype(BF16)
                    v_all = jnp.concatenate([cv, v.reshape(kd["B"], kd["L"], NK)], axis=1).astype(BF16)
                else:
                    q, k, v = _proj_qkv(kd, xs[n], g1, mi, w_qkv, qg, kg, None, None)
                    new_k = k.reshape(kd["B"], 1, kd["L"], N_KV_HEADS, HEAD_DIM)
                    new_v = v.reshape(kd["B"], 1, kd["L"], N_KV_HEADS, HEAD_DIM)
                    k_all = k.reshape(kd["B"], kd["L"], NK).astype(BF16)
                    v_all = v.reshape(kd["B"], kd["L"], NK).astype(BF16)
                ones = jnp.ones(v_all.shape[:2] + (HEAD_DIM,), BF16)
                v_ones = jnp.concatenate(
                    [part for kvh in range(N_KV_HEADS)
                     for part in (v_all[..., kvh * HEAD_DIM:(kvh + 1) * HEAD_DIM], ones)], axis=-1)
                xs[n], h2s[n], affs[n] = _attention(kd, q, k_all, v_ones, w_o, xs[n], mi, g2, wr)
        else:
            w_in = hy_w_in[j].astype(BF16)
            w_out = hy_w_out[j].astype(BF16)
            hp = LANES
            w1 = _pad_to(hy_f_w1[j], (LANES, hp))
            w2 = _pad_to(hy_f_w2[j], (hp, hp))
            w3 = _pad_to(hy_f_w3[j], (hp, 2 * D))
            b1, fr1 = _pad_to(row(hy_f_b1[j]), (1, hp)), _pad_to(row(hy_f_freq1[j]), (1, hp))
            b2, fr2 = _pad_to(row(hy_f_b2[j]), (1, hp)), _pad_to(row(hy_f_freq2[j]), (1, hp))
            for n, kd in enumerate(kinds):
                L = kd["L"]
                zfeat, decay = _filter_features(L)
                fc, fs, fs_fwd, fs_inv = _dft_tables(L)
                *taps, nyq = _hy_filter(L, zfeat, w1, b1, fr1, w2, b2, fr2, w3, decay)
                fc_split = _split_bf16(fc)
                spec = _hy_spectrum(L, fc_split, _split_bf16(fs), taps, nyq)
                ub, x0 = _hy_pre(kd, xs[n], g1, mi, w_in, row(hy_b_in[j]), hy_w_short[j], row(hy_b_short[j]))
                mats = (fc_split[0], fs_fwd.astype(BF16), fs_inv.astype(BF16))
                a = _hy_longconv(kd, ub, mats, spec, x0, row(hy_skip[j]))
                xs[n], h2s[n], affs[n] = _hy_out(kd, a, w_out, row(hy_b_out[j]), xs[n], mi, g2, wr)

        slots, xes, gs = [], [], []
        for n, kd in enumerate(kinds):
            slot = _route(affs[n].reshape(kd["B"] * N_EXPERTS, kd["L"]), kd["cap"])
            ng = N_EXPERTS // kd["eg"]
            slot3 = slot.reshape(kd["B"] * ng, kd["eg"], kd["L"])
            aff3 = affs[n].reshape(kd["B"] * ng, kd["eg"], kd["L"])
            xe, gsel = _gather(kd, slot3, aff3, h2s[n])
            slots.append(slot3)
            xes.append(xe)
            gs.append(gsel)
        yes = _ffn(xes[0], xes[1], gs[0], gs[1], moe_w_gate, moe_w_up, moe_w_down, i)
        for n, kd in enumerate(kinds):
            xs[n] = _scatter(kd, slots[n], yes[n], xs[n], mi, final_row, final=(i == DEPTH - 1))

    y_prompt = xs[0].reshape(b_ctx, l_ctx, D)
    y_sample = xs[1].reshape(b_lat, l_lat, D)
    return (y_prompt, y_sample, new_k, new_v)
```

```python
import functools
import math

import jax
import jax.numpy as jnp
from jax import lax
from jax.experimental import pallas as pl
from jax.experimental.pallas import tpu as pltpu

F32 = jnp.float32
BF16 = jnp.bfloat16
HIGHEST = lax.Precision.HIGHEST

D = 1024
DEPTH = 4
GRID_W = 64
N_MIXERS = 3
HEAD_DIM = 128
N_HEADS = 8
N_KV_HEADS = 2
KV_GROUP = N_HEADS // N_KV_HEADS
NQ = N_HEADS * HEAD_DIM
NK = N_KV_HEADS * HEAD_DIM
ROPE_AXIS_DIM = HEAD_DIM // 2
ROPE_THETA = 10000.0
CONV_WIDTH = 31
CONV_PAD = CONV_WIDTH // 2
SHORT_WIDTH = 3
HY_BANDS = 16
HY_DECAY_TARGET = 1e-2
HY_SHORT_PCT = 0.3
HY_LONG_PCT = 1.5
N_EXPERTS = 16
EXPERT_FF = 1024
EC_CAPACITY_FACTOR = 2
NORM_EPS = 1e-6

LANES = 128
SUBLANES = 8
VMEM_LIMIT_BYTES = 56 * 1024 * 1024

TOKEN_TILE = 256
HALO = 16
CONV_ROW_CHUNK = 32
MOD_ROWS = 8
GATHER_ROWS = 1024
DISPATCH_TOKENS = 1024
SCATTER_GROUP_FACTOR = 2
FFN_F_TILE = 512
CUMSUM_CHUNK = 256
ATTN_STACK_MAX_KEYS = 512


def _cparams(*sem):
    return pltpu.CompilerParams(dimension_semantics=sem, vmem_limit_bytes=VMEM_LIMIT_BYTES)


def _norm_mod(x, g, sh, sc):
    y = x * lax.rsqrt(jnp.mean(x * x, axis=-1, keepdims=True) + NORM_EPS)
    return (y * g) * (1.0 + sc) + sh


def _split_bf16(a):
    hi = a.astype(BF16)
    return hi, (a - hi.astype(F32)).astype(BF16)


def _dot3(a_hi, a_lo, b_hi, b_lo):
    return (jnp.dot(a_hi, b_hi, preferred_element_type=F32) + jnp.dot(a_lo, b_hi, preferred_element_type=F32)
            + jnp.dot(a_hi, b_lo, preferred_element_type=F32))


def _mod_spec(rowfn, j):
    return pl.BlockSpec((1, 1, D), lambda *idx: (rowfn(*idx) * 6 + j, 0, 0))


def _row_spec(n):
    return pl.BlockSpec((1, n), lambda *idx: (0, 0))


def _full_spec(shape):
    nd = len(shape)
    return pl.BlockSpec(shape, lambda *idx: (0,) * nd)


def _tile_spec(tm, n=D):
    return pl.BlockSpec((tm, n), lambda i: (i, 0))


def _halo_specs(rows, tm, n=D):
    hb = tm // HALO
    last = rows // HALO - 1
    return (pl.BlockSpec((HALO, n), lambda i: (jnp.maximum(i * hb - 1, 0), 0)),
            pl.BlockSpec((HALO, n), lambda i: (jnp.minimum((i + 1) * hb, last), 0)))


def _mod_kernel(c_ref, w_ref, b_ref, o_ref):
    cv = c_ref[...]
    s_hi, s_lo = _split_bf16(cv * jax.nn.sigmoid(cv))
    w_hi, w_lo = _split_bf16(w_ref[0])
    o_ref[0] = _dot3(s_hi, s_lo, w_hi, w_lo) + b_ref[0]


def _mod_all(cond, mod_w, mod_b):
    tn = 1536
    n = 6 * D
    return pl.pallas_call(
        _mod_kernel,
        grid=(DEPTH, n // tn),
        in_specs=[
            pl.BlockSpec((MOD_ROWS, D), lambda i, j: (0, 0)),
            pl.BlockSpec((1, D, tn), lambda i, j: (i, 0, j)),
            pl.BlockSpec((1, 1, tn), lambda i, j: (i, 0, j)),
        ],
        out_specs=pl.BlockSpec((1, MOD_ROWS, tn), lambda i, j: (i, 0, j)),
        out_shape=jax.ShapeDtypeStruct((DEPTH, MOD_ROWS, n), F32),
        compiler_params=_cparams("parallel", "parallel"),
        name="mod_all",
    )(cond, mod_w, mod_b.reshape(DEPTH, 1, n))


def _router_tail(x_new, g2_ref, sh2_ref, sc2_ref, wr_ref, h_ref, aff_ref):
    h = _norm_mod(x_new, g2_ref[...], sh2_ref[0], sc2_ref[0])
    h_hi, h_lo = _split_bf16(h)
    h_ref[...] = h_hi
    nt = (((1,), (1,)), ((), ()))
    by_hi = lax.dot_general(wr_ref[...], h_hi, nt, preferred_element_type=F32)
    by_lo = lax.dot_general(wr_ref[0:N_EXPERTS, :], h_lo, nt, preferred_element_type=F32)
    logits = by_hi[0:N_EXPERTS] + by_hi[N_EXPERTS:] + by_lo
    e = jnp.exp(logits - jnp.max(logits, axis=0, keepdims=True))
    aff_ref[0] = e / jnp.sum(e, axis=0, keepdims=True)


def _router_in_specs(rowfn):
    return [_row_spec(D), _mod_spec(rowfn, 3), _mod_spec(rowfn, 4), _full_spec((2 * N_EXPERTS, D))]


def _router_out(kind, tm):
    nb = kind["L"] // tm
    rows = kind["B"] * kind["L"]
    specs = [_tile_spec(tm), pl.BlockSpec((1, N_EXPERTS, tm), lambda i: (i // nb, 0, i % nb))]
    shapes = [jax.ShapeDtypeStruct((rows, D), BF16),
              jax.ShapeDtypeStruct((kind["B"], N_EXPERTS, kind["L"]), F32)]
    return specs, shapes


def _project_with_halo(xp_ref, xc_ref, xn_ref, g_ref, sh_ref, sc_ref, w_ref, b_ref, hs_ref, tm):
    g, sh, sc = g_ref[...], sh_ref[0], sc_ref[0]
    hs_ref[0:HALO, :] = _norm_mod(xp_ref[...], g, sh, sc).astype(BF16)
    hs_ref[HALO:HALO + tm, :] = _norm_mod(xc_ref[...], g, sh, sc).astype(BF16)
    hs_ref[HALO + tm:2 * HALO + tm, :] = _norm_mod(xn_ref[...], g, sh, sc).astype(BF16)
    return jnp.dot(hs_ref[...], w_ref[...], preferred_element_type=F32) + b_ref[...]


def _zero_outside_sequence(zb_ref, j, nb, tm):
    @pl.when(j == 0)
    def _():
        zb_ref[0:HALO, :] = jnp.zeros((HALO, zb_ref.shape[1]), F32)

    @pl.when(j == nb - 1)
    def _():
        zb_ref[HALO + tm:2 * HALO + tm, :] = jnp.zeros((HALO, zb_ref.shape[1]), F32)


def _conv_kernel(xp_ref, xc_ref, xn_ref, g1_ref, sh1_ref, sc1_ref, win_ref, bin_ref,
                 fwd_hi_ref, fwd_lo_ref, inv_hi_ref, inv_lo_ref, kra_ref, krb_ref, ki_ref, bdw_ref,
                 lg_ref, lb_ref, wo_ref, bo_ref, gate_ref, g2_ref, sh2_ref, sc2_ref, wr_ref,
                 o_ref, h_ref, aff_ref, hs_ref, *, nb, tm):
    j = pl.program_id(0) % nb
    n = tm + 2 * HALO
    y = _project_with_halo(xp_ref, xc_ref, xn_ref, g1_ref, sh1_ref, sc1_ref, win_ref, bin_ref, hs_ref, tm)
    z = y[:, :D] * jax.nn.sigmoid(y[:, D:])
    row = lax.broadcasted_iota(jnp.int32, (n, 1), 0)
    first_valid = jnp.where(j > 0, 0, HALO)
    end_valid = jnp.where(j < nb - 1, n, HALO + tm)
    z = jnp.where((row >= first_valid) & (row < end_valid), z, 0.0)
    zb = z.astype(BF16)
    u = (jnp.dot(fwd_hi_ref[...], zb, preferred_element_type=F32)
         + jnp.dot(fwd_lo_ref[...], zb, preferred_element_type=F32))
    ur, ui = u[:n // 2], u[n // 2:]
    ki = ki_ref[...]
    yr = ur * kra_ref[...] - ui * ki
    yi = ur * ki + ui * krb_ref[...]
    yb = jnp.concatenate([yr, yi], axis=0).astype(BF16)
    acc = (jnp.dot(inv_hi_ref[...], yb, preferred_element_type=F32)
           + jnp.dot(inv_lo_ref[...], yb, preferred_element_type=F32) + bdw_ref[...])
    xc = acc - jnp.mean(acc, axis=-1, keepdims=True)
    yn = xc * lax.rsqrt(jnp.mean(xc * xc, axis=-1, keepdims=True) + NORM_EPS)
    yn = yn * lg_ref[...] + lb_ref[...]
    act = (yn * jax.nn.sigmoid(yn)).astype(BF16)
    out = jnp.dot(act, wo_ref[...], preferred_element_type=F32) + bo_ref[...]
    x_new = xc_ref[...] + gate_ref[0] * out
    o_ref[...] = x_new
    _router_tail(x_new, g2_ref, sh2_ref, sc2_ref, wr_ref, h_ref, aff_ref)


def _conv_spectrum_kernel(cw_ref, sw_ref, sgn_ref, w_ref, kra_ref, krb_ref, ki_ref, *, n):
    w = w_ref[...]
    kr = jnp.dot(cw_ref[...], w, preferred_element_type=F32, precision=HIGHEST)
    ki = jnp.dot(sw_ref[...], w, preferred_element_type=F32, precision=HIGHEST)
    nyq = jnp.sum(w * sgn_ref[...], axis=0, keepdims=True)
    row = lax.broadcasted_iota(jnp.int32, kr.shape, 0)
    scale = jnp.where(row == 0, 1.0 / n, 2.0 / n)
    kr = kr * scale
    kra_ref[...] = kr
    krb_ref[...] = jnp.where(row == 0, nyq * (1.0 / n), kr)
    ki_ref[...] = ki * scale


def _conv_spectrum(w_dw_padded, tables):
    cw, sw, sgn = tables
    n = 2 * cw.shape[0]
    out = jax.ShapeDtypeStruct((n // 2, D), F32)
    return pl.pallas_call(
        functools.partial(_conv_spectrum_kernel, n=n),
        out_shape=[out, out, out],
        compiler_params=pltpu.CompilerParams(vmem_limit_bytes=VMEM_LIMIT_BYTES),
        name="conv_spectrum",
    )(cw, sw, sgn, w_dw_padded)


def _conv_mixer(kind, x, g1, mi, w_in, b_in, dft, spec, b_dw, ln_g, ln_b, w_out, b_out, g2, wr):
    rows = x.shape[0]
    tm = TOKEN_TILE
    n = tm + 2 * HALO
    nb = kind["L"] // tm
    rowfn = kind["rowfn"](nb)
    prev_spec, next_spec = _halo_specs(rows, tm)
    r_specs, r_shapes = _router_out(kind, tm)
    return pl.pallas_call(
        functools.partial(_conv_kernel, nb=nb, tm=tm),
        grid=(rows // tm,),
        in_specs=[prev_spec, _tile_spec(tm), next_spec,
                  _row_spec(D), _mod_spec(rowfn, 0), _mod_spec(rowfn, 1),
                  _full_spec((D, 2 * D)), _row_spec(2 * D),
                  _full_spec((n, n)), _full_spec((n, n)), _full_spec((tm, n)), _full_spec((tm, n)),
                  _full_spec((n // 2, D)), _full_spec((n // 2, D)), _full_spec((n // 2, D)),
                  _row_spec(D), _row_spec(D), _row_spec(D),
                  _full_spec((D, D)), _row_spec(D), _mod_spec(rowfn, 2)] + _router_in_specs(rowfn),
        out_specs=[_tile_spec(tm)] + r_specs,
        out_shape=[jax.ShapeDtypeStruct((rows, D), F32)] + r_shapes,
        scratch_shapes=[pltpu.VMEM((n, D), BF16)],
        compiler_params=_cparams("parallel"),
        name="conv_mixer",
    )(x, x, x, g1, mi, mi, w_in, b_in, *dft, *spec, b_dw, ln_g, ln_b, w_out, b_out, mi, g2, mi, mi, wr)


def _head_norm(seg, g):
    return seg * lax.rsqrt(jnp.mean(seg * seg, axis=-1, keepdims=True) + NORM_EPS) * g


def _proj_qkv_kernel(x_ref, g_ref, sh_ref, sc_ref, w_ref, qg_ref, kg_ref, *rest, rope):
    if rope:
        cos_ref, sin_ref, q_ref, k_ref, v_ref = rest
    else:
        q_ref, k_ref, v_ref = rest
    h = _norm_mod(x_ref[...], g_ref[...], sh_ref[0], sc_ref[0])
    y = jnp.dot(h.astype(BF16), w_ref[...], preferred_element_type=F32)
    if rope:
        cos = cos_ref[...]
        sin = sin_ref[...]
        lane = lax.broadcasted_iota(jnp.int32, (y.shape[0], HEAD_DIM), 1)
        even = (lane & 1) == 0
    for hd in range(N_HEADS + N_KV_HEADS):
        seg = y[:, hd * HEAD_DIM:(hd + 1) * HEAD_DIM]
        nrm = _head_norm(seg, qg_ref[...] if hd < N_HEADS else kg_ref[...])
        if rope:
            partner = jnp.where(even, pltpu.roll(nrm, HEAD_DIM - 1, 1), pltpu.roll(nrm, 1, 1))
            nrm = nrm * cos + partner * sin
        if hd < N_HEADS:
            q_ref[:, hd * HEAD_DIM:(hd + 1) * HEAD_DIM] = (nrm * HEAD_DIM ** -0.5).astype(BF16)
        else:
            k_ref[:, (hd - N_HEADS) * HEAD_DIM:(hd - N_HEADS + 1) * HEAD_DIM] = nrm
    v_ref[...] = y[:, NQ + NK:]


def _proj_qkv(kind, x, g, mi, w, qg, kg, cos, sin):
    rows = x.shape[0]
    tm = TOKEN_TILE
    nb = kind["L"] // tm
    rowfn = kind["rowfn"](nb)
    rope = cos is not None
    specs = [_tile_spec(tm), _row_spec(D), _mod_spec(rowfn, 0), _mod_spec(rowfn, 1),
             _full_spec((D, NQ + 2 * NK)), _row_spec(HEAD_DIM), _row_spec(HEAD_DIM)]
    args = [x, g, mi, mi, w, qg, kg]
    if rope:
        specs += [pl.BlockSpec((tm, HEAD_DIM), lambda i: (i % nb, 0))] * 2
        args += [cos, sin]
    return pl.pallas_call(
        functools.partial(_proj_qkv_kernel, rope=rope),
        grid=(rows // tm,),
        in_specs=specs,
        out_specs=[_tile_spec(tm, NQ), _tile_spec(tm, NK), _tile_spec(tm, NK)],
        out_shape=[jax.ShapeDtypeStruct((rows, NQ), BF16),
                   jax.ShapeDtypeStruct((rows, NK), F32),
                   jax.ShapeDtypeStruct((rows, NK), F32)],
        compiler_params=_cparams("parallel"),
        name="proj_qkv",
    )(*args)


def _attn_kernel(q_ref, k_ref, v_ref, wo_ref, x_ref, gate_ref, g2_ref, sh2_ref, sc2_ref, wr_ref,
                 o_ref, h_ref, aff_ref, *, stack_heads):
    tq = q_ref.shape[0]
    outs = [None] * N_HEADS
    if stack_heads:
        groups = [list(range(kv * KV_GROUP, (kv + 1) * KV_GROUP)) for kv in range(N_KV_HEADS)]
    else:
        groups = [[hd] for hd in range(N_HEADS)]
    for heads in groups:
        kv = heads[0] // KV_GROUP
        qs = jnp.concatenate([q_ref[:, hd * HEAD_DIM:(hd + 1) * HEAD_DIM] for hd in heads], axis=0)
        kh = k_ref[0, :, kv * HEAD_DIM:(kv + 1) * HEAD_DIM]
        vh = v_ref[0, :, kv * 2 * HEAD_DIM:(kv + 1) * 2 * HEAD_DIM]
        s = lax.dot_general(qs, kh, (((1,), (1,)), ((), ())), preferred_element_type=F32)
        p = jnp.exp((s - jnp.max(s, axis=-1, keepdims=True)).astype(BF16))
        oa = jnp.dot(p, vh, preferred_element_type=F32)
        on = oa[:, :HEAD_DIM] / oa[:, HEAD_DIM:HEAD_DIM + 1]
        for g, hd in enumerate(heads):
            outs[hd] = on[g * tq:(g + 1) * tq, :]
    o = jnp.concatenate(outs, axis=1).astype(BF16)
    y = jnp.dot(o, wo_ref[...], preferred_element_type=F32)
    x_new = x_ref[...] + gate_ref[0] * y
    o_ref[...] = x_new
    _router_tail(x_new, g2_ref, sh2_ref, sc2_ref, wr_ref, h_ref, aff_ref)


def _attention(kind, q, k_all, v_all, w_o, x, mi, g2, wr):
    rows = x.shape[0]
    tq = TOKEN_TILE
    nb = kind["L"] // tq
    rowfn = kind["rowfn"](nb)
    s_len = k_all.shape[1]
    kv_spec = pl.BlockSpec((1, s_len, NK), lambda i: (i // nb, 0, 0))
    v_spec = pl.BlockSpec((1, s_len, 2 * NK), lambda i: (i // nb, 0, 0))
    r_specs, r_shapes = _router_out(kind, tq)
    return pl.pallas_call(
        functools.partial(_attn_kernel, stack_heads=s_len <= ATTN_STACK_MAX_KEYS),
        grid=(rows // tq,),
        in_specs=[_tile_spec(tq), kv_spec, v_spec, _full_spec((D, D)), _tile_spec(tq),
                  _mod_spec(rowfn, 2)] + _router_in_specs(rowfn),
        out_specs=[_tile_spec(tq)] + r_specs,
        out_shape=[jax.ShapeDtypeStruct((rows, D), F32)] + r_shapes,
        compiler_params=_cparams("parallel"),
        name="attention",
    )(q, k_all, v_all, w_o, x, mi, g2, mi, mi, wr)


def _hy_pre_kernel(xp_ref, xc_ref, xn_ref, g1_ref, sh1_ref, sc1_ref, win_ref, bin_ref, w_ref, b_ref,
                   ub_ref, x0_ref, hs_ref, zb_ref, *, nb, tm):
    j = pl.program_id(0) % nb
    zb_ref[...] = _project_with_halo(xp_ref, xc_ref, xn_ref, g1_ref, sh1_ref, sc1_ref, win_ref, bin_ref,
                                     hs_ref, tm)
    _zero_outside_sequence(zb_ref, j, nb, tm)
    rc = CONV_ROW_CHUNK
    for r0 in range(0, tm, rc):
        parts = []
        for part in range(3):
            cs = slice(part * D, (part + 1) * D)
            lo = r0 + HALO - SUBLANES
            nwin = rc + 2 * SUBLANES
            win = zb_ref[lo:lo + nwin, cs]
            z = b_ref[:, cs]
            for k in range(SHORT_WIDTH):
                shift = (SHORT_WIDTH // 2 - k) % nwin
                moved = win if shift == 0 else pltpu.roll(win, shift, 0)
                z = z + moved[SUBLANES:SUBLANES + rc, :] * w_ref[k:k + 1, cs]
            parts.append(z)
        x0_ref[r0:r0 + rc, :] = parts[0]
        ub_ref[r0:r0 + rc, :] = (parts[2] * parts[1]).astype(BF16)


def _hy_pre(kind, x, g1, mi, w_in, b_in, w_short, b_short):
    rows = x.shape[0]
    tm = TOKEN_TILE
    nb = kind["L"] // tm
    rowfn = kind["rowfn"](nb)
    prev_spec, next_spec = _halo_specs(rows, tm)
    return pl.pallas_call(
        functools.partial(_hy_pre_kernel, nb=nb, tm=tm),
        grid=(rows // tm,),
        in_specs=[prev_spec, _tile_spec(tm), next_spec,
                  _row_spec(D), _mod_spec(rowfn, 0), _mod_spec(rowfn, 1),
                  _full_spec((D, 3 * D)), _row_spec(3 * D),
                  _full_spec((SHORT_WIDTH, 3 * D)), _row_spec(3 * D)],
        out_specs=[_tile_spec(tm)] * 2,
        out_shape=[jax.ShapeDtypeStruct((rows, D), BF16), jax.ShapeDtypeStruct((rows, D), F32)],
        scratch_shapes=[pltpu.VMEM((tm + 2 * HALO, D), BF16), pltpu.VMEM((tm + 2 * HALO, 3 * D), F32)],
        compiler_params=_cparams("parallel"),
        name="hy_pre",
    )(x, x, x, g1, mi, mi, w_in, b_in, w_short, b_short)


def _hy_filter_kernel(z_ref, w1_ref, b1_ref, f1_ref, w2_ref, b2_ref, f2_ref, w3f_ref, w3b_ref,
                      dec_ref, ha_hi_ref, ha_lo_ref, hb_hi_ref, hb_lo_ref, nyq_ref, hid_ref):
    @pl.when(pl.program_id(0) == 0)
    def _():
        h1 = jnp.sin(f1_ref[...] * (jnp.dot(z_ref[...], w1_ref[...], preferred_element_type=F32,
                                            precision=HIGHEST) + b1_ref[...]))
        hid_ref[...] = jnp.sin(f2_ref[...] * (jnp.dot(h1, w2_ref[...], preferred_element_type=F32,
                                                      precision=HIGHEST) + b2_ref[...]))

    f = hid_ref[...]
    dec = dec_ref[...]
    hf = jnp.dot(f, w3f_ref[...], preferred_element_type=F32, precision=HIGHEST) * dec
    hb = jnp.dot(f, w3b_ref[...], preferred_element_type=F32, precision=HIGHEST) * dec
    row = lax.broadcasted_iota(jnp.int32, hf.shape, 0)
    hb = jnp.where(row == 0, 0.0, hb)
    ha = hf + hb
    ha_hi_ref[...], ha_lo_ref[...] = _split_bf16(ha)
    hb_hi_ref[...], hb_lo_ref[...] = _split_bf16(hf - hb)
    sign = jnp.where((row & 1) == 0, 1.0, -1.0)
    nyq_ref[...] = jnp.sum(ha * sign, axis=0, keepdims=True)


def _hy_filter(L, zfeat, w1, b1, fr1, w2, b2, fr2, w3, decay):
    tc = 256
    nct = D // tc
    hp = w1.shape[1]
    c_spec = pl.BlockSpec((L, tc), lambda c: (0, c))
    return pl.pallas_call(
        _hy_filter_kernel,
        grid=(nct,),
        in_specs=[
            _full_spec(zfeat.shape), _full_spec(w1.shape), _row_spec(hp), _row_spec(hp),
            _full_spec(w2.shape), _row_spec(hp), _row_spec(hp),
            pl.BlockSpec((hp, tc), lambda c: (0, c)),
            pl.BlockSpec((hp, tc), lambda c: (0, nct + c)),
            c_spec,
        ],
        out_specs=[c_spec] * 4 + [pl.BlockSpec((1, tc), lambda c: (0, c))],
        out_shape=[jax.ShapeDtypeStruct((L, D), BF16)] * 4 + [jax.ShapeDtypeStruct((1, D), F32)],
        scratch_shapes=[pltpu.VMEM((L, hp), F32)],
        compiler_params=_cparams("arbitrary"),
        name="hy_filter",
    )(zfeat, w1, b1, fr1, w2, b2, fr2, w3, w3, decay)


def _hy_spectrum_kernel(fc_hi_ref, fc_lo_ref, fs_hi_ref, fs_lo_ref, ha_hi_ref, ha_lo_ref, hb_hi_ref, hb_lo_ref,
                        nyq_ref, kra_ref, krb_ref, ki_ref, *, L, ft):
    kr = _dot3(fc_hi_ref[...], fc_lo_ref[...], ha_hi_ref[...], ha_lo_ref[...])
    ki = _dot3(fs_hi_ref[...], fs_lo_ref[...], hb_hi_ref[...], hb_lo_ref[...])
    row = lax.broadcasted_iota(jnp.int32, kr.shape, 0) + pl.program_id(1) * ft
    scale = jnp.where(row == 0, 0.5 / L, 1.0 / L)
    kr = kr * scale
    kra_ref[...] = kr
    krb_ref[...] = jnp.where(row == 0, nyq_ref[...] * (0.5 / L), kr)
    ki_ref[...] = ki * scale


def _hy_spectrum(L, fc_split, fs_split, taps, nyq):
    ft = min(L, 256)
    tc = 512
    mat_spec = pl.BlockSpec((ft, L), lambda c, k: (k, 0))
    h_spec = pl.BlockSpec((L, tc), lambda c, k: (0, c))
    o_spec = pl.BlockSpec((ft, tc), lambda c, k: (k, c))
    return pl.pallas_call(
        functools.partial(_hy_spectrum_kernel, L=L, ft=ft),
        grid=(D // tc, L // ft),
        in_specs=[mat_spec] * 4 + [h_spec] * 4 + [pl.BlockSpec((1, tc), lambda c, k: (0, c))],
        out_specs=[o_spec, o_spec, o_spec],
        out_shape=[jax.ShapeDtypeStruct((L, D), F32)] * 3,
        compiler_params=_cparams("parallel", "parallel"),
        name="hy_spectrum",
    )(*fc_split, *fs_split, *taps, nyq)


def _hy_longconv_kernel(ub_ref, fc_ref, fs_ref, gc_ref, gs_ref, kra_ref, krb_ref, ki_ref, x0_ref, skip_ref,
                        o_ref, acc_ref):
    k = pl.program_id(2)

    @pl.when(k == 0)
    def _():
        acc_ref[...] = jnp.zeros_like(acc_ref)

    ub = ub_ref[...]
    ur = jnp.dot(fc_ref[...], ub, preferred_element_type=F32)
    ui = jnp.dot(fs_ref[...], ub, preferred_element_type=F32)
    kra = kra_ref[...]
    krb = krb_ref[...]
    ki = ki_ref[...]
    yr = (ur * kra - ui * ki).astype(BF16)
    yi = (ur * ki + ui * krb).astype(BF16)
    acc_ref[...] += jnp.dot(gc_ref[...], yr, preferred_element_type=F32)
    acc_ref[...] += jnp.dot(gs_ref[...], yi, preferred_element_type=F32)

    @pl.when(k == pl.num_programs(2) - 1)
    def _():
        y = acc_ref[...] + ub_ref[...].astype(F32) * skip_ref[...]
        o_ref[...] = (y * x0_ref[...]).astype(BF16)


def _hy_longconv(kind, ub, mats, spec, x0, skip):
    B, L = kind["B"], kind["L"]
    fc, fs, fst = mats
    kra, krb, ki = spec
    ft = min(L, 512)
    tc = 1024 if L <= 256 else 512
    d_spec = pl.BlockSpec((L, tc), lambda b, c, k: (b, c))
    fwd_spec = pl.BlockSpec((ft, L), lambda b, c, k: (k, 0))
    inv_spec = pl.BlockSpec((L, ft), lambda b, c, k: (0, k))
    k_spec = pl.BlockSpec((ft, tc), lambda b, c, k: (k, c))
    return pl.pallas_call(
        _hy_longconv_kernel,
        grid=(B, D // tc, L // ft),
        in_specs=[d_spec, fwd_spec, fwd_spec, inv_spec, inv_spec, k_spec, k_spec, k_spec, d_spec,
                  pl.BlockSpec((1, tc), lambda b, c, k: (0, c))],
        out_specs=d_spec,
        out_shape=jax.ShapeDtypeStruct((B * L, D), BF16),
        scratch_shapes=[pltpu.VMEM((L, tc), F32)],
        compiler_params=_cparams("parallel", "parallel", "arbitrary"),
        name="hy_longconv",
    )(ub, fc, fs, fc, fst, kra, krb, ki, x0, skip)


def _hy_out_kernel(a_ref, w_ref, b_ref, x_ref, gate_ref,
                   g2_ref, sh2_ref, sc2_ref, wr_ref, o_ref, h_ref, aff_ref):
    y = jnp.dot(a_ref[...], w_ref[...], preferred_element_type=F32) + b_ref[...]
    x_new = x_ref[...] + gate_ref[0] * y
    o_ref[...] = x_new
    _router_tail(x_new, g2_ref, sh2_ref, sc2_ref, wr_ref, h_ref, aff_ref)


def _hy_out(kind, a, w, b, x, mi, g2, wr):
    rows = x.shape[0]
    tm = TOKEN_TILE
    rowfn = kind["rowfn"](kind["L"] // tm)
    t_spec = _tile_spec(tm)
    r_specs, r_shapes = _router_out(kind, tm)
    return pl.pallas_call(
        _hy_out_kernel,
        grid=(rows // tm,),
        in_specs=[t_spec, _full_spec((D, D)), _row_spec(D), t_spec,
                  _mod_spec(rowfn, 2)] + _router_in_specs(rowfn),
        out_specs=[t_spec] + r_specs,
        out_shape=[jax.ShapeDtypeStruct((rows, D), F32)] + r_shapes,
        compiler_params=_cparams("parallel"),
        name="hy_out",
    )(a, w, b, x, mi, g2, mi, mi, wr)


def _lane_cumsum(mask_f32, tri):
    rows, n = mask_f32.shape
    run = jnp.zeros((rows, 1), F32)
    pieces = []
    for c0 in range(0, n, CUMSUM_CHUNK):
        chunk = mask_f32[:, c0:c0 + CUMSUM_CHUNK]
        pieces.append(jnp.dot(chunk.astype(BF16), tri, preferred_element_type=F32) + run)
        run = run + jnp.sum(chunk, axis=1, keepdims=True)
    return jnp.concatenate(pieces, axis=1) if len(pieces) > 1 else pieces[0]


def _route_kernel(aff_ref, slot_ref, *, cap):
    bits = pltpu.bitcast(aff_ref[...], jnp.int32)
    rows = bits.shape[0]
    thr = jnp.zeros((rows, 1), jnp.int32)
    capf = float(cap)
    for bit in range(30, -1, -1):
        cand = thr | (1 << bit)
        cnt = jnp.sum(jnp.where(bits >= cand, 1.0, 0.0), axis=1, keepdims=True)
        thr = jnp.where(cnt >= capf, cand, thr)
    r_i = lax.broadcasted_iota(jnp.int32, (CUMSUM_CHUNK, CUMSUM_CHUNK), 0)
    c_i = lax.broadcasted_iota(jnp.int32, (CUMSUM_CHUNK, CUMSUM_CHUNK), 1)
    tri = jnp.where(r_i <= c_i, 1.0, 0.0).astype(BF16)
    gt = jnp.where(bits > thr, 1.0, 0.0)
    eq = jnp.where(bits == thr, 1.0, 0.0)
    need = capf - jnp.sum(gt, axis=1, keepdims=True)
    eq_rank = _lane_cumsum(eq, tri)
    sel = gt + eq * jnp.where(eq_rank <= need, 1.0, 0.0)
    pos = _lane_cumsum(sel, tri)
    slot_ref[...] = jnp.where(sel > 0.5, pos - 1.0, -1.0).astype(jnp.int32)


def _route(aff2d, cap):
    return pl.pallas_call(
        functools.partial(_route_kernel, cap=cap),
        out_shape=jax.ShapeDtypeStruct(aff2d.shape, jnp.int32),
        compiler_params=pltpu.CompilerParams(vmem_limit_bytes=VMEM_LIMIT_BYTES),
        name="route",
    )(aff2d)


def _dispatch_masks(slot, cap):
    eg, n = slot.shape
    iota = lax.broadcasted_iota(jnp.int32, (cap, n), 0)
    return [iota == slot[e:e + 1, :] for e in range(eg)]


def _onehot(masks):
    return jnp.concatenate([jnp.where(m, 1.0, 0.0).astype(BF16) for m in masks], axis=0)


def _gather_kernel(slot_ref, aff_ref, h_ref, xe_ref, g_ref, *, cap, eg, bt, seq):
    for b in range(bt):
        masks = _dispatch_masks(slot_ref[b], cap)
        aff = aff_ref[b]
        xe = jnp.dot(_onehot(masks), h_ref[b * seq:(b + 1) * seq, :], preferred_element_type=F32)
        xe_ref[:, b * cap:(b + 1) * cap, :] = xe.reshape(eg, cap, D).astype(BF16)
        for e in range(eg):
            gsel = jnp.sum(jnp.where(masks[e], aff[e:e + 1, :], 0.0), axis=1, keepdims=True)
            g_ref[e, b * cap:(b + 1) * cap, :] = jnp.broadcast_to(gsel, (cap, LANES))


def _gather(kind, slot3, aff3, h2):
    B, L, cap, eg, bt = kind["B"], kind["L"], kind["cap"], kind["eg"], kind["bt"]
    ng = N_EXPERTS // eg
    assert bt == 1 or ng == 1
    r_spec = pl.BlockSpec((bt, eg, L), lambda b, e: (b * ng + e, 0, 0))
    return pl.pallas_call(
        functools.partial(_gather_kernel, cap=cap, eg=eg, bt=bt, seq=L),
        grid=(B // bt, ng),
        in_specs=[r_spec, r_spec, pl.BlockSpec((bt * L, D), lambda b, e: (b, 0))],
        out_specs=[pl.BlockSpec((eg, bt * cap, D), lambda b, e: (e, b, 0)),
                   pl.BlockSpec((eg, bt * cap, LANES), lambda b, e: (e, b, 0))],
        out_shape=[jax.ShapeDtypeStruct((N_EXPERTS, B * cap, D), BF16),
                   jax.ShapeDtypeStruct((N_EXPERTS, B * cap, LANES), F32)],
        compiler_params=_cparams("parallel", "parallel"),
        name="moe_gather",
    )(slot3, aff3, h2)


def _ffn_kernel(xa_ref, xb_ref, ga_ref, gb_ref, wg_ref, wu_ref, wd_ref, ya_ref, yb_ref, acc_ref):
    f = pl.program_id(1)

    @pl.when(f == 0)
    def _():
        acc_ref[...] = jnp.zeros_like(acc_ref)

    wg = wg_ref[0, 0].astype(BF16)
    wu = wu_ref[0, 0].astype(BF16)
    wd = wd_ref[0, 0].astype(BF16)
    na = xa_ref.shape[1]
    for x_ref, r0 in ((xa_ref, 0), (xb_ref, na)):
        x = x_ref[0]
        a = jnp.dot(x, wg, preferred_element_type=F32)
        u = jnp.dot(x, wu, preferred_element_type=F32)
        mid = (a * jax.nn.sigmoid(a) * u).astype(BF16)
        acc_ref[r0:r0 + x.shape[0], :] += jnp.dot(mid, wd, preferred_element_type=F32)

    @pl.when(f == pl.num_programs(1) - 1)
    def _():
        ya_ref[0] = (acc_ref[0:na, :] * ga_ref[0][:, 0:1]).astype(BF16)
        yb_ref[0] = (acc_ref[na:, :] * gb_ref[0][:, 0:1]).astype(BF16)


def _ffn(xe_a, xe_b, g_a, g_b, w_gate, w_up, w_down, layer):
    na, nb_ = xe_a.shape[1], xe_b.shape[1]
    tf = FFN_F_TILE
    return pl.pallas_call(
        _ffn_kernel,
        grid=(N_EXPERTS, EXPERT_FF // tf),
        in_specs=[
            pl.BlockSpec((1, na, D), lambda e, f: (e, 0, 0)),
            pl.BlockSpec((1, nb_, D), lambda e, f: (e, 0, 0)),
            pl.BlockSpec((1, na, LANES), lambda e, f: (e, 0, 0)),
            pl.BlockSpec((1, nb_, LANES), lambda e, f: (e, 0, 0)),
            pl.BlockSpec((1, 1, D, tf), lambda e, f: (layer, e, 0, f)),
            pl.BlockSpec((1, 1, D, tf), lambda e, f: (layer, e, 0, f)),
            pl.BlockSpec((1, 1, tf, D), lambda e, f: (layer, e, f, 0)),
        ],
        out_specs=[pl.BlockSpec((1, na, D), lambda e, f: (e, 0, 0)),
                   pl.BlockSpec((1, nb_, D), lambda e, f: (e, 0, 0))],
        out_shape=[jax.ShapeDtypeStruct((N_EXPERTS, na, D), BF16),
                   jax.ShapeDtypeStruct((N_EXPERTS, nb_, D), BF16)],
        scratch_shapes=[pltpu.VMEM((na + nb_, D), F32)],
        compiler_params=_cparams("parallel", "arbitrary"),
        name="moe_ffn",
    )(xe_a, xe_b, g_a, g_b, w_gate, w_up, w_down)


def _scatter_kernel(slot_ref, ye_ref, x_ref, gate_ref, fg_ref, o_ref, acc_ref, *, cap, eg, bt, lt, final):
    e = pl.program_id(2)

    @pl.when(e == 0)
    def _():
        acc_ref[...] = jnp.zeros_like(acc_ref)

    tn = (((0,), (0,)), ((), ()))
    for b in range(bt):
        onehot = _onehot(_dispatch_masks(slot_ref[b], cap))
        ye = ye_ref[:, b * cap:(b + 1) * cap, :].reshape(eg * cap, D)
        acc_ref[b * lt:(b + 1) * lt, :] += lax.dot_general(onehot, ye, tn, preferred_element_type=F32)

    @pl.when(e == pl.num_programs(2) - 1)
    def _():
        x = x_ref[...] + gate_ref[0] * acc_ref[...]
        if final:
            x = x * lax.rsqrt(jnp.mean(x * x, axis=-1, keepdims=True) + NORM_EPS) * fg_ref[...]
        o_ref[...] = x


def _scatter(kind, slot3, ye, x, mi, final_g, final):
    B, L, cap, bt = kind["B"], kind["L"], kind["cap"], kind["bt"]
    eg = min(N_EXPERTS, SCATTER_GROUP_FACTOR * kind["eg"])
    ng = N_EXPERTS // eg
    slot3 = slot3.reshape(B * ng, eg, L)
    lt = min(L, DISPATCH_TOKENS)
    nl = L // lt
    assert bt == 1 or (ng == 1 and nl == 1)
    rowfn = lambda b, l, e: kind["rowfn"](1)(b)
    x_spec = pl.BlockSpec((bt * lt, D), lambda b, l, e: (b * nl + l, 0))
    return pl.pallas_call(
        functools.partial(_scatter_kernel, cap=cap, eg=eg, bt=bt, lt=lt, final=final),
        grid=(B // bt, nl, ng),
        in_specs=[pl.BlockSpec((bt, eg, lt), lambda b, l, e: (b * ng + e, 0, l)),
                  pl.BlockSpec((eg, bt * cap, D), lambda b, l, e: (e, b, 0)),
                  x_spec, _mod_spec(rowfn, 5), _row_spec(D)],
        out_specs=x_spec,
        out_shape=jax.ShapeDtypeStruct((B * L, D), F32),
        scratch_shapes=[pltpu.VMEM((bt * lt, D), F32)],
        compiler_params=_cparams("parallel", "parallel", "arbitrary"),
        name="moe_scatter",
    )(slot3, ye, x, mi, final_g)


def _rope_tables(L):
    n_rows = L // GRID_W
    rows = jnp.repeat(jnp.arange(n_rows), GRID_W).astype(F32)
    cols = jnp.tile(jnp.arange(GRID_W), n_rows).astype(F32)
    inv = ROPE_THETA ** (-jnp.arange(0, ROPE_AXIS_DIM, 2, dtype=F32) / ROPE_AXIS_DIM)
    ang = jnp.concatenate([rows[:, None] * inv, cols[:, None] * inv], axis=-1)
    cos = jnp.repeat(jnp.cos(ang), 2, axis=-1)
    sin = jnp.repeat(jnp.sin(ang), 2, axis=-1)
    sign = jnp.tile(jnp.array([-1.0, 1.0], F32), HEAD_DIM // 2)
    return cos, sin * sign


def _dft_tables(L):
    r = min(L, 64)
    s = jnp.arange(L, dtype=jnp.int32)

    def small(f):
        ang = ((f[:, None] * s[None, :]) % (2 * L)).astype(F32) * (math.pi / L)
        return jnp.cos(ang), jnp.sin(ang)

    c0, s0 = small(jnp.arange(r, dtype=jnp.int32))
    c1, s1 = small(jnp.arange(L // r, dtype=jnp.int32) * r)
    fc = (c1[:, None, :] * c0[None] - s1[:, None, :] * s0[None]).reshape(L, L)
    fs = -(s1[:, None, :] * c0[None] + c1[:, None, :] * s0[None]).reshape(L, L)
    sign = jnp.where(s % 2 == 0, 1.0, -1.0).astype(F32)
    fs_fwd = jnp.where(s[:, None] == 0, sign[None, :], fs)
    fs_inv = jnp.where(s[None, :] == 0, sign[:, None], fs)
    return fc, fs, fs_fwd, fs_inv


def _conv_dft_tables(tm):
    n = tm + 2 * HALO
    f = jnp.arange(n // 2, dtype=jnp.int32)
    r = jnp.arange(n, dtype=jnp.int32)

    def cs(pos):
        ang = ((f[:, None] * pos[None, :]) % n).astype(F32) * (2.0 * math.pi / n)
        return jnp.cos(ang), -jnp.sin(ang)

    fc, fs = cs(r)
    fs = fs.at[0, :].set(jnp.where(r % 2 == 0, 1.0, -1.0).astype(F32))
    fwd = jnp.concatenate([fc, fs], axis=0)
    inv = jnp.concatenate([fc[:, HALO:HALO + tm].T, fs[:, HALO:HALO + tm].T], axis=1)
    k = jnp.arange(2 * SUBLANES * ((CONV_WIDTH + 2 * SUBLANES - 1) // (2 * SUBLANES)), dtype=jnp.int32)
    lag = (CONV_PAD - k) % n
    cw, sw = cs(lag)
    live = (k < CONV_WIDTH)[None, :]
    cw, sw = jnp.where(live, cw, 0.0), jnp.where(live, sw, 0.0)
    sgn = jnp.where((CONV_PAD - k) % 2 == 0, 1.0, -1.0).astype(F32)[:, None]
    return _split_bf16(fwd) + _split_bf16(inv), (cw, sw, sgn)


def _filter_features(L):
    t = jnp.arange(L, dtype=F32) / L
    bands = jnp.arange(1, HY_BANDS + 1, dtype=F32)
    ph = 2.0 * math.pi * t[:, None] * bands
    z = jnp.concatenate([t[:, None], jnp.sin(ph), jnp.cos(ph)], axis=-1)
    z = jnp.pad(z, ((0, 0), (0, LANES - z.shape[1])))
    rates = jnp.abs(jnp.linspace(math.log(HY_DECAY_TARGET) / HY_LONG_PCT,
                                 math.log(HY_DECAY_TARGET) / HY_SHORT_PCT, D, dtype=F32))
    return z, jnp.exp(-t[:, None] * rates)


def _pad_to(a, shape):
    return jnp.pad(a, [(0, s - d) for d, s in zip(a.shape, shape)])


def kernel(x_prompt, x_sample, cache_k, cache_v, c, c_ctx, mod_w, mod_b, norm1_g, norm2_g, cv_w_in, cv_b_in, cv_w_dw, cv_b_dw, cv_ln_g, cv_ln_b, cv_w_out, cv_b_out, at_w_qkv, at_w_o, at_q_norm, at_k_norm, hy_w_in, hy_b_in, hy_w_short, hy_b_short, hy_f_w1, hy_f_b1, hy_f_freq1, hy_f_w2, hy_f_b2, hy_f_freq2, hy_f_w3, hy_skip, hy_w_out, hy_b_out, moe_router, moe_w_gate, moe_w_up, moe_w_down, final_g):
    b_ctx, l_ctx, _ = x_prompt.shape
    b_lat, l_lat, _ = x_sample.shape
    kinds = [
        dict(B=b_ctx, L=l_ctx, lat=False, rowfn=lambda nb: (lambda i: 0)),
        dict(B=b_lat, L=l_lat, lat=True, rowfn=lambda nb: (lambda i: 1 + i // nb)),
    ]
    for kd in kinds:
        kd["cap"] = EC_CAPACITY_FACTOR * kd["L"] // N_EXPERTS
        kd["eg"] = min(N_EXPERTS, GATHER_ROWS // kd["cap"])
        kd["bt"] = 1 if kd["lat"] else max(1, DISPATCH_TOKENS // kd["L"])
    xs = [x_prompt.reshape(b_ctx * l_ctx, D), x_sample.reshape(b_lat * l_lat, D)]

    cond = jnp.concatenate([c_ctx[None, :], c, jnp.zeros((MOD_ROWS - 1 - b_lat, D), F32)], axis=0)
    mod = _mod_all(cond, mod_w, mod_b)

    row = lambda v: v.reshape(1, -1)
    final_row = row(final_g)
    new_k = new_v = None

    for i in range(DEPTH):
        mixer, j = i % N_MIXERS, i // N_MIXERS
        mi = mod[i].reshape(MOD_ROWS * 6, 1, D)
        g1 = row(norm1_g[i])
        g2 = row(norm2_g[i])
        wr = jnp.concatenate(_split_bf16(moe_router[i].T), axis=0)
        h2s, affs = [None, None], [None, None]
        if mixer == 0:
            w_in = cv_w_in[j].astype(BF16)
            w_out = cv_w_out[j].astype(BF16)
            dft, tap_tables = _conv_dft_tables(TOKEN_TILE)
            spec = _conv_spectrum(_pad_to(cv_w_dw[j], (tap_tables[0].shape[1], D)), tap_tables)
            for n, kd in enumerate(kinds):
                xs[n], h2s[n], affs[n] = _conv_mixer(
                    kd, xs[n], g1, mi, w_in, row(cv_b_in[j]), dft, spec, row(cv_b_dw[j]),
                    row(cv_ln_g[j]), row(cv_ln_b[j]), w_out, row(cv_b_out[j]), g2, wr)
        elif mixer == 1:
            w_qkv = at_w_qkv[j].astype(BF16)
            w_o = at_w_o[j].astype(BF16)
            qg, kg = row(at_q_norm[j]), row(at_k_norm[j])
            for n, kd in enumerate(kinds):
                if kd["lat"]:
                    cos, sin = _rope_tables(kd["L"])
                    q, k, v = _proj_qkv(kd, xs[n], g1, mi, w_qkv, qg, kg, cos, sin)
                    ck = cache_k[:, j].reshape(kd["B"], -1, NK)
                    cv = cache_v[:, j].reshape(kd["B"], -1, NK)
                    k_all = jnp.concatenate([ck, k.reshape(kd["B"], kd["L"], NK)], axis=1).astype(BF16)
                    v_all = jnp.concatenate([cv, v.reshape(kd["B"], kd["L"], NK)], axis=1).astype(BF16)
                else:
                    q, k, v = _proj_qkv(kd, xs[n], g1, mi, w_qkv, qg, kg, None, None)
                    new_k = k.reshape(kd["B"], 1, kd["L"], N_KV_HEADS, HEAD_DIM)
                    new_v = v.reshape(kd["B"], 1, kd["L"], N_KV_HEADS, HEAD_DIM)
                    k_all = k.reshape(kd["B"], kd["L"], NK).astype(BF16)
                    v_all = v.reshape(kd["B"], kd["L"], NK).astype(BF16)
                ones = jnp.ones(v_all.shape[:2] + (HEAD_DIM,), BF16)
                v_ones = jnp.concatenate(
                    [part for kvh in range(N_KV_HEADS)
                     for part in (v_all[..., kvh * HEAD_DIM:(kvh + 1) * HEAD_DIM], ones)], axis=-1)
                xs[n], h2s[n], affs[n] = _attention(kd, q, k_all, v_ones, w_o, xs[n], mi, g2, wr)
        else:
            w_in = hy_w_in[j].astype(BF16)
            w_out = hy_w_out[j].astype(BF16)
            hp = LANES
            w1 = _pad_to(hy_f_w1[j], (LANES, hp))
            w2 = _pad_to(hy_f_w2[j], (hp, hp))
            w3 = _pad_to(hy_f_w3[j], (hp, 2 * D))
            b1, fr1 = _pad_to(row(hy_f_b1[j]), (1, hp)), _pad_to(row(hy_f_freq1[j]), (1, hp))
            b2, fr2 = _pad_to(row(hy_f_b2[j]), (1, hp)), _pad_to(row(hy_f_freq2[j]), (1, hp))
            for n, kd in enumerate(kinds):
                L = kd["L"]
                zfeat, decay = _filter_features(L)
                fc, fs, fs_fwd, fs_inv = _dft_tables(L)
                *taps, nyq = _hy_filter(L, zfeat, w1, b1, fr1, w2, b2, fr2, w3, decay)
                fc_split = _split_bf16(fc)
                spec = _hy_spectrum(L, fc_split, _split_bf16(fs), taps, nyq)
                ub, x0 = _hy_pre(kd, xs[n], g1, mi, w_in, row(hy_b_in[j]), hy_w_short[j], row(hy_b_short[j]))
                mats = (fc_split[0], fs_fwd.astype(BF16), fs_inv.astype(BF16))
                a = _hy_longconv(kd, ub, mats, spec, x0, row(hy_skip[j]))
                xs[n], h2s[n], affs[n] = _hy_out(kd, a, w_out, row(hy_b_out[j]), xs[n], mi, g2, wr)

        slots, xes, gs = [], [], []
        for n, kd in enumerate(kinds):
            slot = _route(affs[n].reshape(kd["B"] * N_EXPERTS, kd["L"]), kd["cap"])
            ng = N_EXPERTS // kd["eg"]
            slot3 = slot.reshape(kd["B"] * ng, kd["eg"], kd["L"])
            aff3 = affs[n].reshape(kd["B"] * ng, kd["eg"], kd["L"])
            xe, gsel = _gather(kd, slot3, aff3, h2s[n])
            slots.append(slot3)
            xes.append(xe)
            gs.append(gsel)
        yes = _ffn(xes[0], xes[1], gs[0], gs[1], moe_w_gate, moe_w_up, moe_w_down, i)
        for n, kd in enumerate(kinds):
            xs[n] = _scatter(kd, slots[n], yes[n], xs[n], mi, final_row, final=(i == DEPTH - 1))

    y_prompt = xs[0].reshape(b_ctx, l_ctx, D)
    y_sample = xs[1].reshape(b_lat, l_lat, D)
    return (y_prompt, y_sample, new_k, new_v)
```

```python
import functools
import math

import jax
import jax.numpy as jnp
from jax import lax
from jax.experimental import pallas as pl
from jax.experimental.pallas import tpu as pltpu

F32 = jnp.float32
BF16 = jnp.bfloat16
HIGHEST = lax.Precision.HIGHEST

D = 1024
DEPTH = 4
GRID_W = 64
N_MIXERS = 3
HEAD_DIM = 128
N_HEADS = 8
N_KV_HEADS = 2
KV_GROUP = N_HEADS // N_KV_HEADS
NQ = N_HEADS * HEAD_DIM
NK = N_KV_HEADS * HEAD_DIM
ROPE_AXIS_DIM = HEAD_DIM // 2
ROPE_THETA = 10000.0
CONV_WIDTH = 31
CONV_PAD = CONV_WIDTH // 2
SHORT_WIDTH = 3
HY_BANDS = 16
HY_DECAY_TARGET = 1e-2
HY_SHORT_PCT = 0.3
HY_LONG_PCT = 1.5
N_EXPERTS = 16
EXPERT_FF = 1024
EC_CAPACITY_FACTOR = 2
NORM_EPS = 1e-6

LANES = 128
SUBLANES = 8
VMEM_LIMIT_BYTES = 56 * 1024 * 1024

TOKEN_TILE = 256
HALO = 16
CONV_ROW_CHUNK = 32
MOD_ROWS = 8
GATHER_ROWS = 1024
DISPATCH_TOKENS = 1024
SCATTER_GROUP_FACTOR = 2
FFN_F_TILE = 512
CUMSUM_CHUNK = 256
ATTN_QUERY_TILE = 512
ATTN_STACK_MAX_KEYS = 512


def _cparams(*sem):
    return pltpu.CompilerParams(dimension_semantics=sem, vmem_limit_bytes=VMEM_LIMIT_BYTES)


def _norm_mod(x, g, sh, sc):
    y = x * lax.rsqrt(jnp.mean(x * x, axis=-1, keepdims=True) + NORM_EPS)
    return (y * g) * (1.0 + sc) + sh


def _split_bf16(a):
    hi = a.astype(BF16)
    return hi, (a - hi.astype(F32)).astype(BF16)


def _dot3(a_hi, a_lo, b_hi, b_lo):
    return (jnp.dot(a_hi, b_hi, preferred_element_type=F32) + jnp.dot(a_lo, b_hi, preferred_element_type=F32)
            + jnp.dot(a_hi, b_lo, preferred_element_type=F32))


def _mod_spec(rowfn, j):
    return pl.BlockSpec((1, 1, D), lambda *idx: (rowfn(*idx) * 6 + j, 0, 0))


def _row_spec(n):
    return pl.BlockSpec((1, n), lambda *idx: (0, 0))


def _full_spec(shape):
    nd = len(shape)
    return pl.BlockSpec(shape, lambda *idx: (0,) * nd)


def _tile_spec(tm, n=D):
    return pl.BlockSpec((tm, n), lambda i: (i, 0))


def _halo_specs(rows, tm, n=D):
    hb = tm // HALO
    last = rows // HALO - 1
    return (pl.BlockSpec((HALO, n), lambda i: (jnp.maximum(i * hb - 1, 0), 0)),
            pl.BlockSpec((HALO, n), lambda i: (jnp.minimum((i + 1) * hb, last), 0)))


def _mod_kernel(c_ref, w_ref, b_ref, o_ref):
    cv = c_ref[...]
    s_hi, s_lo = _split_bf16(cv * jax.nn.sigmoid(cv))
    w_hi, w_lo = _split_bf16(w_ref[0])
    o_ref[0] = _dot3(s_hi, s_lo, w_hi, w_lo) + b_ref[0]


def _mod_all(cond, mod_w, mod_b):
    tn = 1536
    n = 6 * D
    return pl.pallas_call(
        _mod_kernel,
        grid=(DEPTH, n // tn),
        in_specs=[
            pl.BlockSpec((MOD_ROWS, D), lambda i, j: (0, 0)),
            pl.BlockSpec((1, D, tn), lambda i, j: (i, 0, j)),
            pl.BlockSpec((1, 1, tn), lambda i, j: (i, 0, j)),
        ],
        out_specs=pl.BlockSpec((1, MOD_ROWS, tn), lambda i, j: (i, 0, j)),
        out_shape=jax.ShapeDtypeStruct((DEPTH, MOD_ROWS, n), F32),
        compiler_params=_cparams("parallel", "parallel"),
        name="mod_all",
    )(cond, mod_w, mod_b.reshape(DEPTH, 1, n))


def _router_tail(x_new, g2_ref, sh2_ref, sc2_ref, wr_ref, h_ref, aff_ref):
    h = _norm_mod(x_new, g2_ref[...], sh2_ref[0], sc2_ref[0])
    h_hi, h_lo = _split_bf16(h)
    h_ref[...] = h_hi
    nt = (((1,), (1,)), ((), ()))
    by_hi = lax.dot_general(wr_ref[...], h_hi, nt, preferred_element_type=F32)
    by_lo = lax.dot_general(wr_ref[0:N_EXPERTS, :], h_lo, nt, preferred_element_type=F32)
    logits = by_hi[0:N_EXPERTS] + by_hi[N_EXPERTS:] + by_lo
    e = jnp.exp(logits - jnp.max(logits, axis=0, keepdims=True))
    aff_ref[0] = e / jnp.sum(e, axis=0, keepdims=True)


def _router_in_specs(rowfn):
    return [_row_spec(D), _mod_spec(rowfn, 3), _mod_spec(rowfn, 4), _full_spec((2 * N_EXPERTS, D))]


def _router_out(kind, tm):
    nb = kind["L"] // tm
    rows = kind["B"] * kind["L"]
    specs = [_tile_spec(tm), pl.BlockSpec((1, N_EXPERTS, tm), lambda i: (i // nb, 0, i % nb))]
    shapes = [jax.ShapeDtypeStruct((rows, D), BF16),
              jax.ShapeDtypeStruct((kind["B"], N_EXPERTS, kind["L"]), F32)]
    return specs, shapes


def _project_with_halo(xp_ref, xc_ref, xn_ref, g_ref, sh_ref, sc_ref, w_ref, b_ref, hs_ref, tm):
    g, sh, sc = g_ref[...], sh_ref[0], sc_ref[0]
    hs_ref[0:HALO, :] = _norm_mod(xp_ref[...], g, sh, sc).astype(BF16)
    hs_ref[HALO:HALO + tm, :] = _norm_mod(xc_ref[...], g, sh, sc).astype(BF16)
    hs_ref[HALO + tm:2 * HALO + tm, :] = _norm_mod(xn_ref[...], g, sh, sc).astype(BF16)
    return jnp.dot(hs_ref[...], w_ref[...], preferred_element_type=F32) + b_ref[...]


def _zero_outside_sequence(zb_ref, j, nb, tm):
    @pl.when(j == 0)
    def _():
        zb_ref[0:HALO, :] = jnp.zeros((HALO, zb_ref.shape[1]), F32)

    @pl.when(j == nb - 1)
    def _():
        zb_ref[HALO + tm:2 * HALO + tm, :] = jnp.zeros((HALO, zb_ref.shape[1]), F32)


def _conv_kernel(xp_ref, xc_ref, xn_ref, g1_ref, sh1_ref, sc1_ref, win_ref, bin_ref,
                 fwd_hi_ref, fwd_lo_ref, inv_hi_ref, inv_lo_ref, kra_ref, krb_ref, ki_ref, bdw_ref,
                 lg_ref, lb_ref, wo_ref, bo_ref, gate_ref, g2_ref, sh2_ref, sc2_ref, wr_ref,
                 o_ref, h_ref, aff_ref, hs_ref, *, nb, tm):
    j = pl.program_id(0) % nb
    n = tm + 2 * HALO
    y = _project_with_halo(xp_ref, xc_ref, xn_ref, g1_ref, sh1_ref, sc1_ref, win_ref, bin_ref, hs_ref, tm)
    z = y[:, :D] * jax.nn.sigmoid(y[:, D:])
    row = lax.broadcasted_iota(jnp.int32, (n, 1), 0)
    first_valid = jnp.where(j > 0, 0, HALO)
    end_valid = jnp.where(j < nb - 1, n, HALO + tm)
    z = jnp.where((row >= first_valid) & (row < end_valid), z, 0.0)
    zb = z.astype(BF16)
    u = (jnp.dot(fwd_hi_ref[...], zb, preferred_element_type=F32)
         + jnp.dot(fwd_lo_ref[...], zb, preferred_element_type=F32))
    ur, ui = u[:n // 2], u[n // 2:]
    ki = ki_ref[...]
    yr = ur * kra_ref[...] - ui * ki
    yi = ur * ki + ui * krb_ref[...]
    yb = jnp.concatenate([yr, yi], axis=0).astype(BF16)
    acc = (jnp.dot(inv_hi_ref[...], yb, preferred_element_type=F32)
           + jnp.dot(inv_lo_ref[...], yb, preferred_element_type=F32) + bdw_ref[...])
    xc = acc - jnp.mean(acc, axis=-1, keepdims=True)
    yn = xc * lax.rsqrt(jnp.mean(xc * xc, axis=-1, keepdims=True) + NORM_EPS)
    yn = yn * lg_ref[...] + lb_ref[...]
    act = (yn * jax.nn.sigmoid(yn)).astype(BF16)
    out = jnp.dot(act, wo_ref[...], preferred_element_type=F32) + bo_ref[...]
    x_new = xc_ref[...] + gate_ref[0] * out
    o_ref[...] = x_new
    _router_tail(x_new, g2_ref, sh2_ref, sc2_ref, wr_ref, h_ref, aff_ref)


def _conv_spectrum_kernel(cw_ref, sw_ref, sgn_ref, w_ref, kra_ref, krb_ref, ki_ref, *, n):
    w = w_ref[...]
    kr = jnp.dot(cw_ref[...], w, preferred_element_type=F32, precision=HIGHEST)
    ki = jnp.dot(sw_ref[...], w, preferred_element_type=F32, precision=HIGHEST)
    nyq = jnp.sum(w * sgn_ref[...], axis=0, keepdims=True)
    row = lax.broadcasted_iota(jnp.int32, kr.shape, 0)
    scale = jnp.where(row == 0, 1.0 / n, 2.0 / n)
    kr = kr * scale
    kra_ref[...] = kr
    krb_ref[...] = jnp.where(row == 0, nyq * (1.0 / n), kr)
    ki_ref[...] = ki * scale


def _conv_spectrum(w_dw_padded, tables):
    cw, sw, sgn = tables
    n = 2 * cw.shape[0]
    out = jax.ShapeDtypeStruct((n // 2, D), F32)
    return pl.pallas_call(
        functools.partial(_conv_spectrum_kernel, n=n),
        out_shape=[out, out, out],
        compiler_params=pltpu.CompilerParams(vmem_limit_bytes=VMEM_LIMIT_BYTES),
        name="conv_spectrum",
    )(cw, sw, sgn, w_dw_padded)


def _conv_mixer(kind, x, g1, mi, w_in, b_in, dft, spec, b_dw, ln_g, ln_b, w_out, b_out, g2, wr):
    rows = x.shape[0]
    tm = TOKEN_TILE
    n = tm + 2 * HALO
    nb = kind["L"] // tm
    rowfn = kind["rowfn"](nb)
    prev_spec, next_spec = _halo_specs(rows, tm)
    r_specs, r_shapes = _router_out(kind, tm)
    return pl.pallas_call(
        functools.partial(_conv_kernel, nb=nb, tm=tm),
        grid=(rows // tm,),
        in_specs=[prev_spec, _tile_spec(tm), next_spec,
                  _row_spec(D), _mod_spec(rowfn, 0), _mod_spec(rowfn, 1),
                  _full_spec((D, 2 * D)), _row_spec(2 * D),
                  _full_spec((n, n)), _full_spec((n, n)), _full_spec((tm, n)), _full_spec((tm, n)),
                  _full_spec((n // 2, D)), _full_spec((n // 2, D)), _full_spec((n // 2, D)),
                  _row_spec(D), _row_spec(D), _row_spec(D),
                  _full_spec((D, D)), _row_spec(D), _mod_spec(rowfn, 2)] + _router_in_specs(rowfn),
        out_specs=[_tile_spec(tm)] + r_specs,
        out_shape=[jax.ShapeDtypeStruct((rows, D), F32)] + r_shapes,
        scratch_shapes=[pltpu.VMEM((n, D), BF16)],
        compiler_params=_cparams("parallel"),
        name="conv_mixer",
    )(x, x, x, g1, mi, mi, w_in, b_in, *dft, *spec, b_dw, ln_g, ln_b, w_out, b_out, mi, g2, mi, mi, wr)


def _head_norm(seg, g):
    return seg * lax.rsqrt(jnp.mean(seg * seg, axis=-1, keepdims=True) + NORM_EPS) * g


def _proj_qkv_kernel(x_ref, g_ref, sh_ref, sc_ref, w_ref, qg_ref, kg_ref, *rest, rope):
    if rope:
        cos_ref, sin_ref, q_ref, k_ref, v_ref = rest
    else:
        q_ref, k_ref, v_ref = rest
    h = _norm_mod(x_ref[...], g_ref[...], sh_ref[0], sc_ref[0])
    y = jnp.dot(h.astype(BF16), w_ref[...], preferred_element_type=F32)
    if rope:
        cos = cos_ref[...]
        sin = sin_ref[...]
        lane = lax.broadcasted_iota(jnp.int32, (y.shape[0], HEAD_DIM), 1)
        even = (lane & 1) == 0
    for hd in range(N_HEADS + N_KV_HEADS):
        seg = y[:, hd * HEAD_DIM:(hd + 1) * HEAD_DIM]
        nrm = _head_norm(seg, qg_ref[...] if hd < N_HEADS else kg_ref[...])
        if rope:
            partner = jnp.where(even, pltpu.roll(nrm, HEAD_DIM - 1, 1), pltpu.roll(nrm, 1, 1))
            nrm = nrm * cos + partner * sin
        if hd < N_HEADS:
            q_ref[:, hd * HEAD_DIM:(hd + 1) * HEAD_DIM] = (nrm * HEAD_DIM ** -0.5).astype(BF16)
        else:
            k_ref[:, (hd - N_HEADS) * HEAD_DIM:(hd - N_HEADS + 1) * HEAD_DIM] = nrm
    v_ref[...] = y[:, NQ + NK:]


def _proj_qkv(kind, x, g, mi, w, qg, kg, cos, sin):
    rows = x.shape[0]
    tm = TOKEN_TILE
    nb = kind["L"] // tm
    rowfn = kind["rowfn"](nb)
    rope = cos is not None
    specs = [_tile_spec(tm), _row_spec(D), _mod_spec(rowfn, 0), _mod_spec(rowfn, 1),
             _full_spec((D, NQ + 2 * NK)), _row_spec(HEAD_DIM), _row_spec(HEAD_DIM)]
    args = [x, g, mi, mi, w, qg, kg]
    if rope:
        specs += [pl.BlockSpec((tm, HEAD_DIM), lambda i: (i % nb, 0))] * 2
        args += [cos, sin]
    return pl.pallas_call(
        functools.partial(_proj_qkv_kernel, rope=rope),
        grid=(rows // tm,),
        in_specs=specs,
        out_specs=[_tile_spec(tm, NQ), _tile_spec(tm, NK), _tile_spec(tm, NK)],
        out_shape=[jax.ShapeDtypeStruct((rows, NQ), BF16),
                   jax.ShapeDtypeStruct((rows, NK), F32),
                   jax.ShapeDtypeStruct((rows, NK), F32)],
        compiler_params=_cparams("parallel"),
        name="proj_qkv",
    )(*args)


def _attn_kernel(q_ref, k_ref, v_ref, wo_ref, x_ref, gate_ref, g2_ref, sh2_ref, sc2_ref, wr_ref,
                 o_ref, h_ref, aff_ref, *, stack_heads):
    tq = q_ref.shape[0]
    outs = [None] * N_HEADS
    if stack_heads:
        groups = [list(range(kv * KV_GROUP, (kv + 1) * KV_GROUP)) for kv in range(N_KV_HEADS)]
    else:
        groups = [[hd] for hd in range(N_HEADS)]
    for heads in groups:
        kv = heads[0] // KV_GROUP
        qs = jnp.concatenate([q_ref[:, hd * HEAD_DIM:(hd + 1) * HEAD_DIM] for hd in heads], axis=0)
        kh = k_ref[0, :, kv * HEAD_DIM:(kv + 1) * HEAD_DIM]
        vh = v_ref[0, :, kv * 2 * HEAD_DIM:(kv + 1) * 2 * HEAD_DIM]
        s = lax.dot_general(qs, kh, (((1,), (1,)), ((), ())), preferred_element_type=F32)
        p = jnp.exp((s - jnp.max(s, axis=-1, keepdims=True)).astype(BF16))
        oa = jnp.dot(p, vh, preferred_element_type=F32)
        on = oa[:, :HEAD_DIM] / oa[:, HEAD_DIM:HEAD_DIM + 1]
        for g, hd in enumerate(heads):
            outs[hd] = on[g * tq:(g + 1) * tq, :]
    o = jnp.concatenate(outs, axis=1).astype(BF16)
    y = jnp.dot(o, wo_ref[...], preferred_element_type=F32)
    x_new = x_ref[...] + gate_ref[0] * y
    o_ref[...] = x_new
    _router_tail(x_new, g2_ref, sh2_ref, sc2_ref, wr_ref, h_ref, aff_ref)


def _attention(kind, q, k_all, v_all, w_o, x, mi, g2, wr):
    rows = x.shape[0]
    tq = min(kind["L"], ATTN_QUERY_TILE)
    nb = kind["L"] // tq
    rowfn = kind["rowfn"](nb)
    s_len = k_all.shape[1]
    kv_spec = pl.BlockSpec((1, s_len, NK), lambda i: (i // nb, 0, 0))
    v_spec = pl.BlockSpec((1, s_len, 2 * NK), lambda i: (i // nb, 0, 0))
    r_specs, r_shapes = _router_out(kind, tq)
    return pl.pallas_call(
        functools.partial(_attn_kernel, stack_heads=s_len <= ATTN_STACK_MAX_KEYS),
        grid=(rows // tq,),
        in_specs=[_tile_spec(tq), kv_spec, v_spec, _full_spec((D, D)), _tile_spec(tq),
                  _mod_spec(rowfn, 2)] + _router_in_specs(rowfn),
        out_specs=[_tile_spec(tq)] + r_specs,
        out_shape=[jax.ShapeDtypeStruct((rows, D), F32)] + r_shapes,
        compiler_params=_cparams("parallel"),
        name="attention",
    )(q, k_all, v_all, w_o, x, mi, g2, mi, mi, wr)


def _hy_pre_kernel(xp_ref, xc_ref, xn_ref, g1_ref, sh1_ref, sc1_ref, win_ref, bin_ref, w_ref, b_ref,
                   ub_ref, x0_ref, hs_ref, zb_ref, *, nb, tm):
    j = pl.program_id(0) % nb
    zb_ref[...] = _project_with_halo(xp_ref, xc_ref, xn_ref, g1_ref, sh1_ref, sc1_ref, win_ref, bin_ref,
                                     hs_ref, tm)
    _zero_outside_sequence(zb_ref, j, nb, tm)
    rc = CONV_ROW_CHUNK
    for r0 in range(0, tm, rc):
        parts = []
        for part in range(3):
            cs = slice(part * D, (part + 1) * D)
            lo = r0 + HALO - SUBLANES
            nwin = rc + 2 * SUBLANES
            win = zb_ref[lo:lo + nwin, cs]
            z = b_ref[:, cs]
            for k in range(SHORT_WIDTH):
                shift = (SHORT_WIDTH // 2 - k) % nwin
                moved = win if shift == 0 else pltpu.roll(win, shift, 0)
                z = z + moved[SUBLANES:SUBLANES + rc, :] * w_ref[k:k + 1, cs]
            parts.append(z)
        x0_ref[r0:r0 + rc, :] = parts[0]
        ub_ref[r0:r0 + rc, :] = (parts[2] * parts[1]).astype(BF16)


def _hy_pre(kind, x, g1, mi, w_in, b_in, w_short, b_short):
    rows = x.shape[0]
    tm = TOKEN_TILE
    nb = kind["L"] // tm
    rowfn = kind["rowfn"](nb)
    prev_spec, next_spec = _halo_specs(rows, tm)
    return pl.pallas_call(
        functools.partial(_hy_pre_kernel, nb=nb, tm=tm),
        grid=(rows // tm,),
        in_specs=[prev_spec, _tile_spec(tm), next_spec,
                  _row_spec(D), _mod_spec(rowfn, 0), _mod_spec(rowfn, 1),
                  _full_spec((D, 3 * D)), _row_spec(3 * D),
                  _full_spec((SHORT_WIDTH, 3 * D)), _row_spec(3 * D)],
        out_specs=[_tile_spec(tm)] * 2,
        out_shape=[jax.ShapeDtypeStruct((rows, D), BF16), jax.ShapeDtypeStruct((rows, D), F32)],
        scratch_shapes=[pltpu.VMEM((tm + 2 * HALO, D), BF16), pltpu.VMEM((tm + 2 * HALO, 3 * D), F32)],
        compiler_params=_cparams("parallel"),
        name="hy_pre",
    )(x, x, x, g1, mi, mi, w_in, b_in, w_short, b_short)


def _hy_filter_kernel(z_ref, w1_ref, b1_ref, f1_ref, w2_ref, b2_ref, f2_ref, w3f_ref, w3b_ref,
                      dec_ref, ha_hi_ref, ha_lo_ref, hb_hi_ref, hb_lo_ref, nyq_ref, hid_ref):
    @pl.when(pl.program_id(0) == 0)
    def _():
        h1 = jnp.sin(f1_ref[...] * (jnp.dot(z_ref[...], w1_ref[...], preferred_element_type=F32,
                                            precision=HIGHEST) + b1_ref[...]))
        hid_ref[...] = jnp.sin(f2_ref[...] * (jnp.dot(h1, w2_ref[...], preferred_element_type=F32,
                                                      precision=HIGHEST) + b2_ref[...]))

    f = hid_ref[...]
    dec = dec_ref[...]
    hf = jnp.dot(f, w3f_ref[...], preferred_element_type=F32, precision=HIGHEST) * dec
    hb = jnp.dot(f, w3b_ref[...], preferred_element_type=F32, precision=HIGHEST) * dec
    row = lax.broadcasted_iota(jnp.int32, hf.shape, 0)
    hb = jnp.where(row == 0, 0.0, hb)
    ha = hf + hb
    ha_hi_ref[...], ha_lo_ref[...] = _split_bf16(ha)
    hb_hi_ref[...], hb_lo_ref[...] = _split_bf16(hf - hb)
    sign = jnp.where((row & 1) == 0, 1.0, -1.0)
    nyq_ref[...] = jnp.sum(ha * sign, axis=0, keepdims=True)


def _hy_filter(L, zfeat, w1, b1, fr1, w2, b2, fr2, w3, decay):
    tc = 256
    nct = D // tc
    hp = w1.shape[1]
    c_spec = pl.BlockSpec((L, tc), lambda c: (0, c))
    return pl.pallas_call(
        _hy_filter_kernel,
        grid=(nct,),
        in_specs=[
            _full_spec(zfeat.shape), _full_spec(w1.shape), _row_spec(hp), _row_spec(hp),
            _full_spec(w2.shape), _row_spec(hp), _row_spec(hp),
            pl.BlockSpec((hp, tc), lambda c: (0, c)),
            pl.BlockSpec((hp, tc), lambda c: (0, nct + c)),
            c_spec,
        ],
        out_specs=[c_spec] * 4 + [pl.BlockSpec((1, tc), lambda c: (0, c))],
        out_shape=[jax.ShapeDtypeStruct((L, D), BF16)] * 4 + [jax.ShapeDtypeStruct((1, D), F32)],
        scratch_shapes=[pltpu.VMEM((L, hp), F32)],
        compiler_params=_cparams("arbitrary"),
        name="hy_filter",
    )(zfeat, w1, b1, fr1, w2, b2, fr2, w3, w3, decay)


def _hy_spectrum_kernel(fc_hi_ref, fc_lo_ref, fs_hi_ref, fs_lo_ref, ha_hi_ref, ha_lo_ref, hb_hi_ref, hb_lo_ref,
                        nyq_ref, kra_ref, krb_ref, ki_ref, *, L, ft):
    kr = _dot3(fc_hi_ref[...], fc_lo_ref[...], ha_hi_ref[...], ha_lo_ref[...])
    ki = _dot3(fs_hi_ref[...], fs_lo_ref[...], hb_hi_ref[...], hb_lo_ref[...])
    row = lax.broadcasted_iota(jnp.int32, kr.shape, 0) + pl.program_id(1) * ft
    scale = jnp.where(row == 0, 0.5 / L, 1.0 / L)
    kr = kr * scale
    kra_ref[...] = kr
    krb_ref[...] = jnp.where(row == 0, nyq_ref[...] * (0.5 / L), kr)
    ki_ref[...] = ki * scale


def _hy_spectrum(L, fc_split, fs_split, taps, nyq):
    ft = min(L, 256)
    tc = 512
    mat_spec = pl.BlockSpec((ft, L), lambda c, k: (k, 0))
    h_spec = pl.BlockSpec((L, tc), lambda c, k: (0, c))
    o_spec = pl.BlockSpec((ft, tc), lambda c, k: (k, c))
    return pl.pallas_call(
        functools.partial(_hy_spectrum_kernel, L=L, ft=ft),
        grid=(D // tc, L // ft),
        in_specs=[mat_spec] * 4 + [h_spec] * 4 + [pl.BlockSpec((1, tc), lambda c, k: (0, c))],
        out_specs=[o_spec, o_spec, o_spec],
        out_shape=[jax.ShapeDtypeStruct((L, D), F32)] * 3,
        compiler_params=_cparams("parallel", "parallel"),
        name="hy_spectrum",
    )(*fc_split, *fs_split, *taps, nyq)


def _hy_longconv_kernel(ub_ref, fc_ref, fs_ref, gc_ref, gs_ref, kra_ref, krb_ref, ki_ref, x0_ref, skip_ref,
                        o_ref, acc_ref):
    k = pl.program_id(2)

    @pl.when(k == 0)
    def _():
        acc_ref[...] = jnp.zeros_like(acc_ref)

    ub = ub_ref[...]
    ur = jnp.dot(fc_ref[...], ub, preferred_element_type=F32)
    ui = jnp.dot(fs_ref[...], ub, preferred_element_type=F32)
    kra = kra_ref[...]
    krb = krb_ref[...]
    ki = ki_ref[...]
    yr = (ur * kra - ui * ki).astype(BF16)
    yi = (ur * ki + ui * krb).astype(BF16)
    acc_ref[...] += jnp.dot(gc_ref[...], yr, preferred_element_type=F32)
    acc_ref[...] += jnp.dot(gs_ref[...], yi, preferred_element_type=F32)

    @pl.when(k == pl.num_programs(2) - 1)
    def _():
        y = acc_ref[...] + ub_ref[...].astype(F32) * skip_ref[...]
        o_ref[...] = (y * x0_ref[...]).astype(BF16)


def _hy_longconv(kind, ub, mats, spec, x0, skip):
    B, L = kind["B"], kind["L"]
    fc, fs, fst = mats
    kra, krb, ki = spec
    ft = min(L, 512)
    tc = 1024 if L <= 256 else 512
    d_spec = pl.BlockSpec((L, tc), lambda b, c, k: (b, c))
    fwd_spec = pl.BlockSpec((ft, L), lambda b, c, k: (k, 0))
    inv_spec = pl.BlockSpec((L, ft), lambda b, c, k: (0, k))
    k_spec = pl.BlockSpec((ft, tc), lambda b, c, k: (k, c))
    return pl.pallas_call(
        _hy_longconv_kernel,
        grid=(B, D // tc, L // ft),
        in_specs=[d_spec, fwd_spec, fwd_spec, inv_spec, inv_spec, k_spec, k_spec, k_spec, d_spec,
                  pl.BlockSpec((1, tc), lambda b, c, k: (0, c))],
        out_specs=d_spec,
        out_shape=jax.ShapeDtypeStruct((B * L, D), BF16),
        scratch_shapes=[pltpu.VMEM((L, tc), F32)],
        compiler_params=_cparams("parallel", "parallel", "arbitrary"),
        name="hy_longconv",
    )(ub, fc, fs, fc, fst, kra, krb, ki, x0, skip)


def _hy_out_kernel(a_ref, w_ref, b_ref, x_ref, gate_ref,
                   g2_ref, sh2_ref, sc2_ref, wr_ref, o_ref, h_ref, aff_ref):
    y = jnp.dot(a_ref[...], w_ref[...], preferred_element_type=F32) + b_ref[...]
    x_new = x_ref[...] + gate_ref[0] * y
    o_ref[...] = x_new
    _router_tail(x_new, g2_ref, sh2_ref, sc2_ref, wr_ref, h_ref, aff_ref)


def _hy_out(kind, a, w, b, x, mi, g2, wr):
    rows = x.shape[0]
    tm = TOKEN_TILE
    rowfn = kind["rowfn"](kind["L"] // tm)
    t_spec = _tile_spec(tm)
    r_specs, r_shapes = _router_out(kind, tm)
    return pl.pallas_call(
        _hy_out_kernel,
        grid=(rows // tm,),
        in_specs=[t_spec, _full_spec((D, D)), _row_spec(D), t_spec,
                  _mod_spec(rowfn, 2)] + _router_in_specs(rowfn),
        out_specs=[t_spec] + r_specs,
        out_shape=[jax.ShapeDtypeStruct((rows, D), F32)] + r_shapes,
        compiler_params=_cparams("parallel"),
        name="hy_out",
    )(a, w, b, x, mi, g2, mi, mi, wr)


def _lane_cumsum(mask_f32, tri):
    rows, n = mask_f32.shape
    run = jnp.zeros((rows, 1), F32)
    pieces = []
    for c0 in range(0, n, CUMSUM_CHUNK):
        chunk = mask_f32[:, c0:c0 + CUMSUM_CHUNK]
        pieces.append(jnp.dot(chunk.astype(BF16), tri, preferred_element_type=F32) + run)
        run = run + jnp.sum(chunk, axis=1, keepdims=True)
    return jnp.concatenate(pieces, axis=1) if len(pieces) > 1 else pieces[0]


def _route_kernel(aff_ref, slot_ref, *, cap):
    bits = pltpu.bitcast(aff_ref[...], jnp.int32)
    rows = bits.shape[0]
    thr = jnp.zeros((rows, 1), jnp.int32)
    capf = float(cap)
    for bit in range(30, -1, -1):
        cand = thr | (1 << bit)
        cnt = jnp.sum(jnp.where(bits >= cand, 1.0, 0.0), axis=1, keepdims=True)
        thr = jnp.where(cnt >= capf, cand, thr)
    r_i = lax.broadcasted_iota(jnp.int32, (CUMSUM_CHUNK, CUMSUM_CHUNK), 0)
    c_i = lax.broadcasted_iota(jnp.int32, (CUMSUM_CHUNK, CUMSUM_CHUNK), 1)
    tri = jnp.where(r_i <= c_i, 1.0, 0.0).astype(BF16)
    gt = jnp.where(bits > thr, 1.0, 0.0)
    eq = jnp.where(bits == thr, 1.0, 0.0)
    need = capf - jnp.sum(gt, axis=1, keepdims=True)
    eq_rank = _lane_cumsum(eq, tri)
    sel = gt + eq * jnp.where(eq_rank <= need, 1.0, 0.0)
    pos = _lane_cumsum(sel, tri)
    slot_ref[...] = jnp.where(sel > 0.5, pos - 1.0, -1.0).astype(jnp.int32)


def _route(aff2d, cap):
    return pl.pallas_call(
        functools.partial(_route_kernel, cap=cap),
        out_shape=jax.ShapeDtypeStruct(aff2d.shape, jnp.int32),
        compiler_params=pltpu.CompilerParams(vmem_limit_bytes=VMEM_LIMIT_BYTES),
        name="route",
    )(aff2d)


def _dispatch_masks(slot, cap):
    eg, n = slot.shape
    iota = lax.broadcasted_iota(jnp.int32, (cap, n), 0)
    return [iota == slot[e:e + 1, :] for e in range(eg)]


def _onehot(masks):
    return jnp.concatenate([jnp.where(m, 1.0, 0.0).astype(BF16) for m in masks], axis=0)


def _gather_kernel(slot_ref, aff_ref, h_ref, xe_ref, g_ref, *, cap, eg, bt, seq):
    for b in range(bt):
        masks = _dispatch_masks(slot_ref[b], cap)
        aff = aff_ref[b]
        xe = jnp.dot(_onehot(masks), h_ref[b * seq:(b + 1) * seq, :], preferred_element_type=F32)
        xe_ref[:, b * cap:(b + 1) * cap, :] = xe.reshape(eg, cap, D).astype(BF16)
        for e in range(eg):
            gsel = jnp.sum(jnp.where(masks[e], aff[e:e + 1, :], 0.0), axis=1, keepdims=True)
            g_ref[e, b * cap:(b + 1) * cap, :] = jnp.broadcast_to(gsel, (cap, LANES))


def _gather(kind, slot3, aff3, h2):
    B, L, cap, eg, bt = kind["B"], kind["L"], kind["cap"], kind["eg"], kind["bt"]
    ng = N_EXPERTS // eg
    assert bt == 1 or ng == 1
    r_spec = pl.BlockSpec((bt, eg, L), lambda b, e: (b * ng + e, 0, 0))
    return pl.pallas_call(
        functools.partial(_gather_kernel, cap=cap, eg=eg, bt=bt, seq=L),
        grid=(B // bt, ng),
        in_specs=[r_spec, r_spec, pl.BlockSpec((bt * L, D), lambda b, e: (b, 0))],
        out_specs=[pl.BlockSpec((eg, bt * cap, D), lambda b, e: (e, b, 0)),
                   pl.BlockSpec((eg, bt * cap, LANES), lambda b, e: (e, b, 0))],
        out_shape=[jax.ShapeDtypeStruct((N_EXPERTS, B * cap, D), BF16),
                   jax.ShapeDtypeStruct((N_EXPERTS, B * cap, LANES), F32)],
        compiler_params=_cparams("parallel", "parallel"),
        name="moe_gather",
    )(slot3, aff3, h2)


def _ffn_kernel(xa_ref, xb_ref, ga_ref, gb_ref, wg_ref, wu_ref, wd_ref, ya_ref, yb_ref, acc_ref):
    f = pl.program_id(1)

    @pl.when(f == 0)
    def _():
        acc_ref[...] = jnp.zeros_like(acc_ref)

    wg = wg_ref[0, 0].astype(BF16)
    wu = wu_ref[0, 0].astype(BF16)
    wd = wd_ref[0, 0].astype(BF16)
    na = xa_ref.shape[1]
    for x_ref, r0 in ((xa_ref, 0), (xb_ref, na)):
        x = x_ref[0]
        a = jnp.dot(x, wg, preferred_element_type=F32)
        u = jnp.dot(x, wu, preferred_element_type=F32)
        mid = (a * jax.nn.sigmoid(a) * u).astype(BF16)
        acc_ref[r0:r0 + x.shape[0], :] += jnp.dot(mid, wd, preferred_element_type=F32)

    @pl.when(f == pl.num_programs(1) - 1)
    def _():
        ya_ref[0] = (acc_ref[0:na, :] * ga_ref[0][:, 0:1]).astype(BF16)
        yb_ref[0] = (acc_ref[na:, :] * gb_ref[0][:, 0:1]).astype(BF16)


def _ffn(xe_a, xe_b, g_a, g_b, w_gate, w_up, w_down, layer):
    na, nb_ = xe_a.shape[1], xe_b.shape[1]
    tf = FFN_F_TILE
    return pl.pallas_call(
        _ffn_kernel,
        grid=(N_EXPERTS, EXPERT_FF // tf),
        in_specs=[
            pl.BlockSpec((1, na, D), lambda e, f: (e, 0, 0)),
            pl.BlockSpec((1, nb_, D), lambda e, f: (e, 0, 0)),
            pl.BlockSpec((1, na, LANES), lambda e, f: (e, 0, 0)),
            pl.BlockSpec((1, nb_, LANES), lambda e, f: (e, 0, 0)),
            pl.BlockSpec((1, 1, D, tf), lambda e, f: (layer, e, 0, f)),
            pl.BlockSpec((1, 1, D, tf), lambda e, f: (layer, e, 0, f)),
            pl.BlockSpec((1, 1, tf, D), lambda e, f: (layer, e, f, 0)),
        ],
        out_specs=[pl.BlockSpec((1, na, D), lambda e, f: (e, 0, 0)),
                   pl.BlockSpec((1, nb_, D), lambda e, f: (e, 0, 0))],
        out_shape=[jax.ShapeDtypeStruct((N_EXPERTS, na, D), BF16),
                   jax.ShapeDtypeStruct((N_EXPERTS, nb_, D), BF16)],
        scratch_shapes=[pltpu.VMEM((na + nb_, D), F32)],
        compiler_params=_cparams("parallel", "arbitrary"),
        name="moe_ffn",
    )(xe_a, xe_b, g_a, g_b, w_gate, w_up, w_down)


def _scatter_kernel(slot_ref, ye_ref, x_ref, gate_ref, fg_ref, o_ref, acc_ref, *, cap, eg, bt, lt, final):
    e = pl.program_id(2)

    @pl.when(e == 0)
    def _():
        acc_ref[...] = jnp.zeros_like(acc_ref)

    tn = (((0,), (0,)), ((), ()))
    for b in range(bt):
        onehot = _onehot(_dispatch_masks(slot_ref[b], cap))
        ye = ye_ref[:, b * cap:(b + 1) * cap, :].reshape(eg * cap, D)
        acc_ref[b * lt:(b + 1) * lt, :] += lax.dot_general(onehot, ye, tn, preferred_element_type=F32)

    @pl.when(e == pl.num_programs(2) - 1)
    def _():
        x = x_ref[...] + gate_ref[0] * acc_ref[...]
        if final:
            x = x * lax.rsqrt(jnp.mean(x * x, axis=-1, keepdims=True) + NORM_EPS) * fg_ref[...]
        o_ref[...] = x


def _scatter(kind, slot3, ye, x, mi, final_g, final):
    B, L, cap, bt = kind["B"], kind["L"], kind["cap"], kind["bt"]
    eg = min(N_EXPERTS, SCATTER_GROUP_FACTOR * kind["eg"])
    ng = N_EXPERTS // eg
    slot3 = slot3.reshape(B * ng, eg, L)
    lt = min(L, DISPATCH_TOKENS)
    nl = L // lt
    assert bt == 1 or (ng == 1 and nl == 1)
    rowfn = lambda b, l, e: kind["rowfn"](1)(b)
    x_spec = pl.BlockSpec((bt * lt, D), lambda b, l, e: (b * nl + l, 0))
    return pl.pallas_call(
        functools.partial(_scatter_kernel, cap=cap, eg=eg, bt=bt, lt=lt, final=final),
        grid=(B // bt, nl, ng),
        in_specs=[pl.BlockSpec((bt, eg, lt), lambda b, l, e: (b * ng + e, 0, l)),
                  pl.BlockSpec((eg, bt * cap, D), lambda b, l, e: (e, b, 0)),
                  x_spec, _mod_spec(rowfn, 5), _row_spec(D)],
        out_specs=x_spec,
        out_shape=jax.ShapeDtypeStruct((B * L, D), F32),
        scratch_shapes=[pltpu.VMEM((bt * lt, D), F32)],
        compiler_params=_cparams("parallel", "parallel", "arbitrary"),
        name="moe_scatter",
    )(slot3, ye, x, mi, final_g)


def _rope_tables(L):
    n_rows = L // GRID_W
    rows = jnp.repeat(jnp.arange(n_rows), GRID_W).astype(F32)
    cols = jnp.tile(jnp.arange(GRID_W), n_rows).astype(F32)
    inv = ROPE_THETA ** (-jnp.arange(0, ROPE_AXIS_DIM, 2, dtype=F32) / ROPE_AXIS_DIM)
    ang = jnp.concatenate([rows[:, None] * inv, cols[:, None] * inv], axis=-1)
    cos = jnp.repeat(jnp.cos(ang), 2, axis=-1)
    sin = jnp.repeat(jnp.sin(ang), 2, axis=-1)
    sign = jnp.tile(jnp.array([-1.0, 1.0], F32), HEAD_DIM // 2)
    return cos, sin * sign


def _dft_tables(L):
    r = min(L, 64)
    s = jnp.arange(L, dtype=jnp.int32)

    def small(f):
        ang = ((f[:, None] * s[None, :]) % (2 * L)).astype(F32) * (math.pi / L)
        return jnp.cos(ang), jnp.sin(ang)

    c0, s0 = small(jnp.arange(r, dtype=jnp.int32))
    c1, s1 = small(jnp.arange(L // r, dtype=jnp.int32) * r)
    fc = (c1[:, None, :] * c0[None] - s1[:, None, :] * s0[None]).reshape(L, L)
    fs = -(s1[:, None, :] * c0[None] + c1[:, None, :] * s0[None]).reshape(L, L)
    sign = jnp.where(s % 2 == 0, 1.0, -1.0).astype(F32)
    fs_fwd = jnp.where(s[:, None] == 0, sign[None, :], fs)
    fs_inv = jnp.where(s[None, :] == 0, sign[:, None], fs)
    return fc, fs, fs_fwd, fs_inv


def _conv_dft_tables(tm):
    n = tm + 2 * HALO
    f = jnp.arange(n // 2, dtype=jnp.int32)
    r = jnp.arange(n, dtype=jnp.int32)

    def cs(pos):
        ang = ((f[:, None] * pos[None, :]) % n).astype(F32) * (2.0 * math.pi / n)
        return jnp.cos(ang), -jnp.sin(ang)

    fc, fs = cs(r)
    fs = fs.at[0, :].set(jnp.where(r % 2 == 0, 1.0, -1.0).astype(F32))
    fwd = jnp.concatenate([fc, fs], axis=0)
    inv = jnp.concatenate([fc[:, HALO:HALO + tm].T, fs[:, HALO:HALO + tm].T], axis=1)
    k = jnp.arange(2 * SUBLANES * ((CONV_WIDTH + 2 * SUBLANES - 1) // (2 * SUBLANES)), dtype=jnp.int32)
    lag = (CONV_PAD - k) % n
    cw, sw = cs(lag)
    live = (k < CONV_WIDTH)[None, :]
    cw, sw = jnp.where(live, cw, 0.0), jnp.where(live, sw, 0.0)
    sgn = jnp.where((CONV_PAD - k) % 2 == 0, 1.0, -1.0).astype(F32)[:, None]
    return _split_bf16(fwd) + _split_bf16(inv), (cw, sw, sgn)


def _filter_features(L):
    t = jnp.arange(L, dtype=F32) / L
    bands = jnp.arange(1, HY_BANDS + 1, dtype=F32)
    ph = 2.0 * math.pi * t[:, None] * bands
    z = jnp.concatenate([t[:, None], jnp.sin(ph), jnp.cos(ph)], axis=-1)
    z = jnp.pad(z, ((0, 0), (0, LANES - z.shape[1])))
    rates = jnp.abs(jnp.linspace(math.log(HY_DECAY_TARGET) / HY_LONG_PCT,
                                 math.log(HY_DECAY_TARGET) / HY_SHORT_PCT, D, dtype=F32))
    return z, jnp.exp(-t[:, None] * rates)


def _pad_to(a, shape):
    return jnp.pad(a, [(0, s - d) for d, s in zip(a.shape, shape)])


def kernel(x_prompt, x_sample, cache_k, cache_v, c, c_ctx, mod_w, mod_b, norm1_g, norm2_g, cv_w_in, cv_b_in, cv_w_dw, cv_b_dw, cv_ln_g, cv_ln_b, cv_w_out, cv_b_out, at_w_qkv, at_w_o, at_q_norm, at_k_norm, hy_w_in, hy_b_in, hy_w_short, hy_b_short, hy_f_w1, hy_f_b1, hy_f_freq1, hy_f_w2, hy_f_b2, hy_f_freq2, hy_f_w3, hy_skip, hy_w_out, hy_b_out, moe_router, moe_w_gate, moe_w_up, moe_w_down, final_g):
    b_ctx, l_ctx, _ = x_prompt.shape
    b_lat, l_lat, _ = x_sample.shape
    kinds = [
        dict(B=b_ctx, L=l_ctx, lat=False, rowfn=lambda nb: (lambda i: 0)),
        dict(B=b_lat, L=l_lat, lat=True, rowfn=lambda nb: (lambda i: 1 + i // nb)),
    ]
    for kd in kinds:
        kd["cap"] = EC_CAPACITY_FACTOR * kd["L"] // N_EXPERTS
        kd["eg"] = min(N_EXPERTS, GATHER_ROWS // kd["cap"])
        kd["bt"] = 1 if kd["lat"] else max(1, DISPATCH_TOKENS // kd["L"])
    xs = [x_prompt.reshape(b_ctx * l_ctx, D), x_sample.reshape(b_lat * l_lat, D)]

    cond = jnp.concatenate([c_ctx[None, :], c, jnp.zeros((MOD_ROWS - 1 - b_lat, D), F32)], axis=0)
    mod = _mod_all(cond, mod_w, mod_b)

    row = lambda v: v.reshape(1, -1)
    final_row = row(final_g)
    new_k = new_v = None

    for i in range(DEPTH):
        mixer, j = i % N_MIXERS, i // N_MIXERS
        mi = mod[i].reshape(MOD_ROWS * 6, 1, D)
        g1 = row(norm1_g[i])
        g2 = row(norm2_g[i])
        wr = jnp.concatenate(_split_bf16(moe_router[i].T), axis=0)
        h2s, affs = [None, None], [None, None]
        if mixer == 0:
            w_in = cv_w_in[j].astype(BF16)
            w_out = cv_w_out[j].astype(BF16)
            dft, tap_tables = _conv_dft_tables(TOKEN_TILE)
            spec = _conv_spectrum(_pad_to(cv_w_dw[j], (tap_tables[0].shape[1], D)), tap_tables)
            for n, kd in enumerate(kinds):
                xs[n], h2s[n], affs[n] = _conv_mixer(
                    kd, xs[n], g1, mi, w_in, row(cv_b_in[j]), dft, spec, row(cv_b_dw[j]),
                    row(cv_ln_g[j]), row(cv_ln_b[j]), w_out, row(cv_b_out[j]), g2, wr)
        elif mixer == 1:
            w_qkv = at_w_qkv[j].astype(BF16)
            w_o = at_w_o[j].astype(BF16)
            qg, kg = row(at_q_norm[j]), row(at_k_norm[j])
            for n, kd in enumerate(kinds):
                if kd["lat"]:
                    cos, sin = _rope_tables(kd["L"])
                    q, k, v = _proj_qkv(kd, xs[n], g1, mi, w_qkv, qg, kg, cos, sin)
                    ck = cache_k[:, j].reshape(kd["B"], -1, NK)
                    cv = cache_v[:, j].reshape(kd["B"], -1, NK)
                    k_all = jnp.concatenate([ck, k.reshape(kd["B"], kd["L"], NK)], axis=1).astype(BF16)
                    v_all = jnp.concatenate([cv, v.reshape(kd["B"], kd["L"], NK)], axis=1).astype(BF16)
                else:
                    q, k, v = _proj_qkv(kd, xs[n], g1, mi, w_qkv, qg, kg, None, None)
                    new_k = k.reshape(kd["B"], 1, kd["L"], N_KV_HEADS, HEAD_DIM)
                    new_v = v.reshape(kd["B"], 1, kd["L"], N_KV_HEADS, HEAD_DIM)
                    k_all = k.reshape(kd["B"], kd["L"], NK).astype(BF16)
                    v_all = v.reshape(kd["B"], kd["L"], NK).astype(BF16)
                ones = jnp.ones(v_all.shape[:2] + (HEAD_DIM,), BF16)
                v_ones = jnp.concatenate(
                    [part for kvh in range(N_KV_HEADS)
                     for part in (v_all[..., kvh * HEAD_DIM:(kvh + 1) * HEAD_DIM], ones)], axis=-1)
                xs[n], h2s[n], affs[n] = _attention(kd, q, k_all, v_ones, w_o, xs[n], mi, g2, wr)
        else:
            w_in = hy_w_in[j].astype(BF16)
            w_out = hy_w_out[j].astype(BF16)
            hp = LANES
            w1 = _pad_to(hy_f_w1[j], (LANES, hp))
            w2 = _pad_to(hy_f_w2[j], (hp, hp))
            w3 = _pad_to(hy_f_w3[j], (hp, 2 * D))
            b1, fr1 = _pad_to(row(hy_f_b1[j]), (1, hp)), _pad_to(row(hy_f_freq1[j]), (1, hp))
            b2, fr2 = _pad_to(row(hy_f_b2[j]), (1, hp)), _pad_to(row(hy_f_freq2[j]), (1, hp))
            for n, kd in enumerate(kinds):
                L = kd["L"]
                zfeat, decay = _filter_features(L)
                fc, fs, fs_fwd, fs_inv = _dft_tables(L)
                *taps, nyq = _hy_filter(L, zfeat, w1, b1, fr1, w2, b2, fr2, w3, decay)
                fc_split = _split_bf16(fc)
                spec = _hy_spectrum(L, fc_split, _split_bf16(fs), taps, nyq)
                ub, x0 = _hy_pre(kd, xs[n], g1, mi, w_in, row(hy_b_in[j]), hy_w_short[j], row(hy_b_short[j]))
                mats = (fc_split[0], fs_fwd.astype(BF16), fs_inv.astype(BF16))
                a = _hy_longconv(kd, ub, mats, spec, x0, row(hy_skip[j]))
                xs[n], h2s[n], affs[n] = _hy_out(kd, a, w_out, row(hy_b_out[j]), xs[n], mi, g2, wr)

        slots, xes, gs = [], [], []
        for n, kd in enumerate(kinds):
            slot = _route(affs[n].reshape(kd["B"] * N_EXPERTS, kd["L"]), kd["cap"])
            ng = N_EXPERTS // kd["eg"]
            slot3 = slot.reshape(kd["B"] * ng, kd["eg"], kd["L"])
            aff3 = affs[n].reshape(kd["B"] * ng, kd["eg"], kd["L"])
            xe, gsel = _gather(kd, slot3, aff3, h2s[n])
            slots.append(slot3)
            xes.append(xe)
            gs.append(gsel)
        yes = _ffn(xes[0], xes[1], gs[0], gs[1], moe_w_gate, moe_w_up, moe_w_down, i)
        for n, kd in enumerate(kinds):
            xs[n] = _scatter(kd, slots[n], yes[n], xs[n], mi, final_row, final=(i == DEPTH - 1))

    y_prompt = xs[0].reshape(b_ctx, l_ctx, D)
    y_sample = xs[1].reshape(b_lat, l_lat, D)
    return (y_prompt, y_sample, new_k, new_v)
```
